```python
import math
import jax, jax.numpy as jnp
from jax import lax
import numpy as np

D_MODEL = 2048
BATCH = 1
SEQ = 16384
DEPTH = 2

HEAD_DIM = 128
ATTN_WIDTH = D_MODEL // 2
N_ATTN_HEADS = ATTN_WIDTH // HEAD_DIM
SGU_WIDTH = D_MODEL // 4
SGU_GROUPS = 4
SGU_GROUP_DIM = SGU_WIDTH // SGU_GROUPS
SGU_CHUNK = 128
POOL_WIDTH = D_MODEL // 4
POOL_WINDOWS = (2, 4, 8, 16)
POOL_GROUPS = len(POOL_WINDOWS)
POOL_GROUP_DIM = POOL_WIDTH // POOL_GROUPS
MIX_WIDTH = ATTN_WIDTH + SGU_WIDTH + POOL_WIDTH
IN_COLS = 3 * ATTN_WIDTH + 2 * SGU_WIDTH + POOL_WIDTH
D_FF = ((8 * D_MODEL // 3 + 127) // 128) * 128
BLOCK_Q = 128
EPS = 1e-6
FFN_RES_WEIGHT = 0.5

kernel_name = "hybrid_stickbreak_sgu_pool_macaron"


def rmsnorm(x, g):
    xf = x.astype(jnp.float32)
    inv = lax.rsqrt(jnp.mean(xf * xf, axis=-1, keepdims=True) + EPS)
    return (xf * inv * g.astype(jnp.float32)).astype(x.dtype)


def swiglu(x, w_gu, w_down):
    gate, up = jnp.split(x @ w_gu, 2, axis=-1)
    return (jax.nn.silu(gate) * up) @ w_down


def stick_breaking_attention(q, k, v):
    B, H, S, Dh = q.shape
    scale = 1.0 / math.sqrt(Dh)
    kf = k.astype(jnp.float32)
    vf = v.astype(jnp.float32)
    s_idx = jnp.arange(S)
    n_blocks = S // BLOCK_Q

    def one_block(i):
        start = i * BLOCK_Q
        qb = lax.dynamic_slice_in_dim(q, start, BLOCK_Q, axis=2).astype(jnp.float32)
        z = jnp.einsum('bhqd,bhkd->bhqk', qb, kf) * scale
        t_idx = start + jnp.arange(BLOCK_Q)
        causal = s_idx[None, :] < t_idx[:, None]
        log_1m = jnp.where(causal, jax.nn.log_sigmoid(-z), 0.0)
        between = lax.cumsum(log_1m, axis=3, reverse=True) - log_1m
        log_a = jnp.where(causal, jax.nn.log_sigmoid(z) + between, -jnp.inf)
        a = jnp.exp(log_a)
        return jnp.einsum('bhqk,bhkd->bhqd', a, vf)

    out = lax.map(one_block, jnp.arange(n_blocks))
    out = out.transpose(1, 0, 3, 2, 4).reshape(B, S, H * Dh)
    return out.astype(q.dtype)


def spatial_gating(u, vs, norm_g, w_s, b_s):
    B, S, _ = u.shape
    u = jax.nn.gelu(u)
    vg = jax.nn.gelu(vs).reshape(B, S, SGU_GROUPS, SGU_GROUP_DIM)
    vg = rmsnorm(vg, norm_g.reshape(SGU_GROUPS, SGU_GROUP_DIM))
    vc = vg.reshape(B, S // SGU_CHUNK, SGU_CHUNK, SGU_GROUPS, SGU_GROUP_DIM)
    tril = jnp.tril(jnp.ones((SGU_CHUNK, SGU_CHUNK), dtype=w_s.dtype))
    w_causal = w_s * tril[None]
    mixed = jnp.einsum('gts,bnsgc->bntgc', w_causal, vc) + b_s.T[:, :, None]
    return u * mixed.reshape(B, S, SGU_WIDTH)


def multiscale_pool(p, w_pool, scale):
    B, S, _ = p.shape
    pg = p.astype(jnp.float32).reshape(B, S, POOL_GROUPS, POOL_GROUP_DIM)
    csum = jnp.cumsum(pg, axis=1)
    pos = jnp.arange(S)
    outs = []
    for g, w in enumerate(POOL_WINDOWS):
        c = csum[:, :, g]
        c_prev = jnp.pad(c, ((0, 0), (w, 0), (0, 0)))[:, :S]
        count = jnp.minimum(pos + 1, w).astype(jnp.float32)[None, :, None]
        outs.append((c - c_prev) / count - pg[:, :, g])
    d = jnp.stack(outs, axis=2)
    y = jnp.einsum('bsgc,gcd->bsgd', d, w_pool.astype(jnp.float32)).reshape(B, S, POOL_WIDTH)
    return (y * scale.astype(jnp.float32)).astype(p.dtype)


def token_mixing(h, w_in, sgu_norm, sgu_w, sgu_b, pool_w, pool_scale, w_out):
    B, S, _ = h.shape
    proj = h @ w_in
    a0, a1, a2 = ATTN_WIDTH, 2 * ATTN_WIDTH, 3 * ATTN_WIDTH
    b0, b1 = a2 + SGU_WIDTH, a2 + 2 * SGU_WIDTH
    q, k, v = proj[..., :a0], proj[..., a0:a1], proj[..., a1:a2]
    u, vs, p = proj[..., a2:b0], proj[..., b0:b1], proj[..., b1:]
    to_heads = lambda t: t.reshape(B, S, N_ATTN_HEADS, HEAD_DIM).transpose(0, 2, 1, 3)
    y_a = stick_breaking_attention(to_heads(q), to_heads(k), to_heads(v))
    y_b = spatial_gating(u, vs, sgu_norm, sgu_w, sgu_b)
    y_c = multiscale_pool(p, pool_w, pool_scale)
    return jnp.concatenate([y_a, y_b, y_c], axis=-1) @ w_out


def setup_inputs(seed: int = 0) -> dict:
    key = jax.random.key(seed)
    ks = jax.random.split(key, 20)
    f32 = jnp.float32
    nrm = lambda k, shape, s: jax.random.normal(k, shape, f32) * s
    gain = lambda k, shape: 1.0 + 0.02 * jax.random.normal(k, shape, f32)
    return {
        "x": jax.random.normal(ks[0], (BATCH, SEQ, D_MODEL), f32),
        "ffn1_norm": gain(ks[1], (DEPTH, D_MODEL)),
        "ffn1_w_gu": nrm(ks[2], (DEPTH, D_MODEL, 2 * D_FF), D_MODEL ** -0.5),
        "ffn1_w_down": nrm(ks[3], (DEPTH, D_FF, D_MODEL), D_FF ** -0.5),
        "mix_norm": gain(ks[4], (DEPTH, D_MODEL)),
        "w_in": nrm(ks[5], (DEPTH, D_MODEL, IN_COLS), D_MODEL ** -0.5),
        "sgu_norm": gain(ks[6], (DEPTH, SGU_WIDTH)),
        "sgu_w": nrm(ks[7], (DEPTH, SGU_GROUPS, SGU_CHUNK, SGU_CHUNK), SGU_CHUNK ** -0.5),
        "sgu_b": gain(ks[8], (DEPTH, SGU_GROUPS, SGU_CHUNK)),
        "pool_w": nrm(ks[9], (DEPTH, POOL_GROUPS, POOL_GROUP_DIM, POOL_GROUP_DIM), POOL_GROUP_DIM ** -0.5),
        "pool_scale": gain(ks[10], (DEPTH, POOL_WIDTH)),
        "w_out": nrm(ks[11], (DEPTH, MIX_WIDTH, D_MODEL), MIX_WIDTH ** -0.5),
        "ffn2_norm": gain(ks[12], (DEPTH, D_MODEL)),
        "ffn2_w_gu": nrm(ks[13], (DEPTH, D_MODEL, 2 * D_FF), D_MODEL ** -0.5),
        "ffn2_w_down": nrm(ks[14], (DEPTH, D_FF, D_MODEL), D_FF ** -0.5),
        "final_norm": gain(ks[15], (D_MODEL,)),
    }


def reference(x, ffn1_norm, ffn1_w_gu, ffn1_w_down, mix_norm, w_in, sgu_norm, sgu_w, sgu_b,
              pool_w, pool_scale, w_out, ffn2_norm, ffn2_w_gu, ffn2_w_down, final_norm):
    for l in range(DEPTH):
        x = x + FFN_RES_WEIGHT * swiglu(rmsnorm(x, ffn1_norm[l]), ffn1_w_gu[l], ffn1_w_down[l])
        x = x + token_mixing(rmsnorm(x, mix_norm[l]), w_in[l], sgu_norm[l], sgu_w[l], sgu_b[l],
                             pool_w[l], pool_scale[l], w_out[l])
        x = x + FFN_RES_WEIGHT * swiglu(rmsnorm(x, ffn2_norm[l]), ffn2_w_gu[l], ffn2_w_down[l])
    return rmsnorm(x, final_norm)
```

```python
import functools
import math

import numpy as np
import jax
import jax.numpy as jnp
from jax import lax
from jax.experimental import pallas as pl
from jax.experimental.pallas import tpu as pltpu

F32 = jnp.float32
BF16 = jnp.bfloat16

HEAD_DIM = 128
SGU_GROUPS = 4
SGU_CHUNK = 128
POOL_WINDOWS = (2, 4, 8, 16)
GROUP_DIM = 128
EPS = 1e-6
FFN_RES_WEIGHT = 0.5

V7X_VMEM_BYTES = 64 * 1024 * 1024
VMEM_LIMIT = V7X_VMEM_BYTES - 8 * 1024 * 1024

FFN_ROWS = 512
FFN_COLS = 512
PROJ_ROWS = 512
QKV_COLS = 1024
ATTN_Q = 256
ATTN_K = 256
LOG_ZERO_BOUND = -104.0


def _rmsnorm(x, g):
    inv = lax.rsqrt(jnp.mean(x * x, axis=-1, keepdims=True) + EPS)
    return x * inv * g


def _dot(a, b):
    return jnp.dot(a, b, preferred_element_type=F32)


def _ffn_kernel(x_ref, g_ref, wg_ref, wu_ref, wd_ref, fg_ref, o_ref, xn_ref, *, nj, final):
    j = pl.program_id(1)

    @pl.when(j == 0)
    def _():
        xn_ref[...] = _rmsnorm(x_ref[...], g_ref[...]).astype(BF16)

    xn = xn_ref[...]
    gate = _dot(xn, wg_ref[...])
    up = _dot(xn, wu_ref[...])
    h = (gate * jax.nn.sigmoid(gate) * up).astype(BF16)
    part = _dot(h, wd_ref[...])

    @pl.when(j == 0)
    def _():
        o_ref[...] = part

    @pl.when(j > 0)
    def _():
        o_ref[...] += part

    @pl.when(j == nj - 1)
    def _():
        y = x_ref[...] + FFN_RES_WEIGHT * o_ref[...]
        if final:
            y = _rmsnorm(y, fg_ref[...])
        o_ref[...] = y


def _ffn(x, g, w_gu_p, w_down_p, fg, *, final):
    s, d = x.shape
    ffp = w_down_p.shape[0]
    nj = ffp // FFN_COLS
    tm = min(FFN_ROWS, s)
    return pl.pallas_call(
        functools.partial(_ffn_kernel, nj=nj, final=final),
        grid=(s // tm, nj),
        in_specs=[
            pl.BlockSpec((tm, d), lambda i, j: (i, 0)),
            pl.BlockSpec((1, d), lambda i, j: (0, 0)),
            pl.BlockSpec((d, FFN_COLS), lambda i, j: (0, j)),
            pl.BlockSpec((d, FFN_COLS), lambda i, j: (0, nj + j)),
            pl.BlockSpec((FFN_COLS, d), lambda i, j: (j, 0)),
            pl.BlockSpec((1, d), lambda i, j: (0, 0)),
        ],
        out_specs=pl.BlockSpec((tm, d), lambda i, j: (i, 0)),
        out_shape=jax.ShapeDtypeStruct((s, d), F32),
        scratch_shapes=[pltpu.VMEM((tm, d), BF16)],
        compiler_params=pltpu.CompilerParams(
            dimension_semantics=("parallel", "arbitrary"), vmem_limit_bytes=VMEM_LIMIT),
        name="ffn",
    )(x, g, w_gu_p, w_gu_p, w_down_p, fg)


def _qkv_kernel(x_ref, g_ref, w_ref, o_ref, xn_ref, *, q_blocks, scale):
    j = pl.program_id(1)

    @pl.when(j == 0)
    def _():
        xn_ref[...] = _rmsnorm(x_ref[...], g_ref[...]).astype(BF16)

    acc = _dot(xn_ref[...], w_ref[...])
    col_scale = jnp.where(j < q_blocks, jnp.float32(scale), jnp.float32(1.0))
    o_ref[...] = (acc * col_scale).astype(BF16)


def _qkv(x, g, w_in, attn_width):
    s, d = x.shape
    n = 3 * attn_width
    tm = min(PROJ_ROWS, s)
    return pl.pallas_call(
        functools.partial(_qkv_kernel, q_blocks=attn_width // QKV_COLS,
                          scale=1.0 / math.sqrt(HEAD_DIM)),
        grid=(s // tm, n // QKV_COLS),
        in_specs=[
            pl.BlockSpec((tm, d), lambda i, j: (i, 0)),
            pl.BlockSpec((1, d), lambda i, j: (0, 0)),
            pl.BlockSpec((d, QKV_COLS), lambda i, j: (0, j)),
        ],
        out_specs=pl.BlockSpec((tm, QKV_COLS), lambda i, j: (i, j)),
        out_shape=jax.ShapeDtypeStruct((s, n), BF16),
        scratch_shapes=[pltpu.VMEM((tm, d), BF16)],
        compiler_params=pltpu.CompilerParams(
            dimension_semantics=("parallel", "arbitrary"), vmem_limit_bytes=VMEM_LIMIT),
        name="qkv_proj",
    )(x, g, w_in)


def _sgu_pool_kernel(x_ref, g_ref, w_ref, sn_ref, sw_ref, sb_ref, band_ref, pw_ref, ps_ref,
                     o_ref, ph_ref, plo_ref, *, tm):
    i = pl.program_id(0)
    ch = SGU_CHUNK
    width = SGU_GROUPS * GROUP_DIM
    xn = _rmsnorm(x_ref[...], g_ref[...]).astype(BF16)
    proj = _dot(xn, w_ref[...])
    u = jax.nn.gelu(proj[:, :width])
    v = jax.nn.gelu(proj[:, width:2 * width])
    p = proj[:, 2 * width:]

    @pl.when(i == 0)
    def _():
        ph_ref[0:ch, :] = jnp.zeros((ch, width), BF16)
        plo_ref[0:ch, :] = jnp.zeros((ch, width), BF16)

    p_hi = p.astype(BF16)
    ph_ref[ch:ch + tm, :] = p_hi
    plo_ref[ch:ch + tm, :] = (p - p_hi.astype(F32)).astype(BF16)

    row = lax.broadcasted_iota(jnp.int32, (ch, ch), 0)
    col = lax.broadcasted_iota(jnp.int32, (ch, ch), 1)
    tril = col <= row
    for grp in range(SGU_GROUPS):
        lanes = slice(grp * GROUP_DIM, (grp + 1) * GROUP_DIM)
        vg = _rmsnorm(v[:, lanes], sn_ref[:, lanes]).astype(BF16)
        w_causal = jnp.where(tril, sw_ref[grp], 0.0).astype(BF16)
        window = POOL_WINDOWS[grp]
        band = band_ref[grp]
        pool_w = pw_ref[grp]
        for c in range(tm // ch):
            rows = slice(c * ch, (c + 1) * ch)
            mixed = _dot(w_causal, vg[rows]) + sb_ref[:, lanes]
            o_ref[rows, lanes] = (u[rows, lanes] * mixed).astype(BF16)

            ext = slice(c * ch, (c + 2) * ch)
            wsum = _dot(band, ph_ref[ext, lanes]) + _dot(band, plo_ref[ext, lanes])
            pos = i * tm + c * ch + row
            count = jnp.minimum(pos + 1, window).astype(F32)
            dlt = wsum / count - p[rows, lanes]
            y = _dot(dlt.astype(BF16), pool_w) * ps_ref[:, lanes]
            o_ref[rows, width + grp * GROUP_DIM:width + (grp + 1) * GROUP_DIM] = y.astype(BF16)

    ph_ref[0:ch, :] = ph_ref[tm:tm + ch, :]
    plo_ref[0:ch, :] = plo_ref[tm:tm + ch, :]


def _pool_bands():
    t = np.arange(SGU_CHUNK)[:, None] + SGU_CHUNK
    j = np.arange(2 * SGU_CHUNK)[None, :]
    bands = [((t - j >= 0) & (t - j < w)).astype(np.float32) for w in POOL_WINDOWS]
    return jnp.asarray(np.stack(bands), dtype=BF16)


def _sgu_pool(x, g, w_in, sgu_norm, sgu_w, sgu_bias, pool_w, pool_scale):
    s, d = x.shape
    width = SGU_GROUPS * GROUP_DIM
    tm = min(PROJ_ROWS, s)
    last_block = w_in.shape[1] // (3 * width) - 1
    assert (last_block + 1) * 3 * width == w_in.shape[1]
    full = lambda shape: pl.BlockSpec(shape, lambda i: (0,) * len(shape))
    return pl.pallas_call(
        functools.partial(_sgu_pool_kernel, tm=tm),
        grid=(s // tm,),
        in_specs=[
            pl.BlockSpec((tm, d), lambda i: (i, 0)),
            full((1, d)),
            pl.BlockSpec((d, 3 * width), lambda i: (0, last_block)),
            full((1, width)),
            full((SGU_GROUPS, SGU_CHUNK, SGU_CHUNK)),
            full((SGU_CHUNK, width)),
            full((SGU_GROUPS, SGU_CHUNK, 2 * SGU_CHUNK)),
            full((SGU_GROUPS, GROUP_DIM, GROUP_DIM)),
            full((1, width)),
        ],
        out_specs=pl.BlockSpec((tm, 2 * width), lambda i: (i, 0)),
        out_shape=jax.ShapeDtypeStruct((s, 2 * width), BF16),
        scratch_shapes=[pltpu.VMEM((tm + SGU_CHUNK, width), BF16),
                        pltpu.VMEM((tm + SGU_CHUNK, width), BF16)],
        compiler_params=pltpu.CompilerParams(
            dimension_semantics=("arbitrary",), vmem_limit_bytes=VMEM_LIMIT),
        name="sgu_pool",
    )(x, g, w_in, sgu_norm, sgu_w, sgu_bias, _pool_bands(), pool_w, pool_scale)


def _attn_kernel(q_ref, k_ref, v_ref, m_ref, o_ref, r_ref, acc_ref, *, tq, ts):
    i = pl.program_id(1)
    q = q_ref[...]
    m = m_ref[...]
    row = lax.broadcasted_iota(jnp.int32, (tq, ts), 0)
    col = lax.broadcasted_iota(jnp.int32, (tq, ts), 1)
    causal = col < row

    def step(j, diag):
        start = pl.multiple_of(j * ts, ts)
        ks = k_ref[pl.ds(start, ts), :]
        vs = v_ref[pl.ds(start, ts), :]
        z = lax.dot_general(q, ks, (((1,), (1,)), ((), ())), preferred_element_type=F32)
        log_1m = jnp.minimum(-z, 0.0) - jnp.log1p(jnp.exp(-jnp.abs(z)))
        if diag:
            log_1m = jnp.where(causal, log_1m, 0.0)
        hi = log_1m.astype(BF16)
        lo = (log_1m - hi.astype(F32)).astype(BF16)
        sums = _dot(hi, m) + _dot(lo, m)
        log_a = z + log_1m + sums[:, :ts]
        if not diag:
            log_a = log_a + r_ref[...]
        a = jnp.exp(log_a)
        if diag:
            a = jnp.where(causal, a, 0.0)
        pv = _dot(a.astype(BF16), vs)
        if diag:
            r_new = sums[:, ts:]
            acc_ref[...] = pv
        else:
            r_new = r_ref[...] + sums[:, ts:]
            acc_ref[...] += pv
        r_ref[...] = r_new
        return jnp.max(r_new)

    rmax0 = step(i, True)

    def cond(carry):
        j, rmax = carry
        return jnp.logical_and(j >= 0, rmax > LOG_ZERO_BOUND)

    def body(carry):
        j, _ = carry
        return j - 1, step(j, False)

    lax.while_loop(cond, body, (i - 1, rmax0))
    o_ref[...] = acc_ref[...].astype(BF16)


def _cumsum_matrix(ts):
    j = np.arange(ts)[:, None]
    s = np.arange(ts)[None, :]
    later = (j > s).astype(np.float32)
    return jnp.asarray(np.concatenate([later, np.ones((ts, ts), np.float32)], axis=1), dtype=BF16)


def _attention(qkv, n_heads):
    s = qkv.shape[0]
    tq = min(ATTN_Q, s)
    ts = tq
    return pl.pallas_call(
        functools.partial(_attn_kernel, tq=tq, ts=ts),
        grid=(n_heads, s // tq),
        in_specs=[
            pl.BlockSpec((tq, HEAD_DIM), lambda h, i: (i, h)),
            pl.BlockSpec((s, HEAD_DIM), lambda h, i: (0, n_heads + h)),
            pl.BlockSpec((s, HEAD_DIM), lambda h, i: (0, 2 * n_heads + h)),
            pl.BlockSpec((ts, 2 * ts), lambda h, i: (0, 0)),
        ],
        out_specs=pl.BlockSpec((tq, HEAD_DIM), lambda h, i: (i, h)),
        out_shape=jax.ShapeDtypeStruct((s, n_heads * HEAD_DIM), BF16),
        scratch_shapes=[pltpu.VMEM((tq, ts), F32), pltpu.VMEM((tq, HEAD_DIM), F32)],
        compiler_params=pltpu.CompilerParams(
            dimension_semantics=("parallel", "arbitrary"), vmem_limit_bytes=VMEM_LIMIT),
        name="stickbreak_attn",
    )(qkv, qkv, qkv, _cumsum_matrix(ts))


def _outproj_kernel(x_ref, ya_ref, ybc_ref, wa_ref, wb_ref, o_ref):
    o_ref[...] = x_ref[...] + _dot(ya_ref[...], wa_ref[...]) + _dot(ybc_ref[...], wb_ref[...])


def _outproj(x, y_a, y_bc, w_out):
    s, d = x.shape
    ka = y_a.shape[1]
    kb = y_bc.shape[1]
    assert ka == kb
    tm = min(PROJ_ROWS, s)
    return pl.pallas_call(
        _outproj_kernel,
        grid=(s // tm,),
        in_specs=[
            pl.BlockSpec((tm, d), lambda i: (i, 0)),
            pl.BlockSpec((tm, ka), lambda i: (i, 0)),
            pl.BlockSpec((tm, kb), lambda i: (i, 0)),
            pl.BlockSpec((ka, d), lambda i: (0, 0)),
            pl.BlockSpec((kb, d), lambda i: (1, 0)),
        ],
        out_specs=pl.BlockSpec((tm, d), lambda i: (i, 0)),
        out_shape=jax.ShapeDtypeStruct((s, d), F32),
        compiler_params=pltpu.CompilerParams(
            dimension_semantics=("parallel",), vmem_limit_bytes=VMEM_LIMIT),
        name="out_proj",
    )(x, y_a, y_bc, w_out, w_out)


def _prep_ffn_weights(w_gu, w_down):
    d_ff = w_down.shape[0]
    pad = (-d_ff) % FFN_COLS
    gate = jnp.pad(w_gu[:, :d_ff], ((0, 0), (0, pad)))
    up = jnp.pad(w_gu[:, d_ff:], ((0, 0), (0, pad)))
    w_gu_p = jnp.concatenate([gate, up], axis=1).astype(BF16)
    w_down_p = jnp.pad(w_down, ((0, pad), (0, 0))).astype(BF16)
    return w_gu_p, w_down_p


def kernel(x, ffn1_norm, ffn1_w_gu, ffn1_w_down, mix_norm, w_in, sgu_norm, sgu_w, sgu_b, pool_w,
           pool_scale, w_out, ffn2_norm, ffn2_w_gu, ffn2_w_down, final_norm):
    b, s, d = x.shape
    assert b == 1
    depth = w_in.shape[0]
    attn_width = w_out.shape[1] // 2
    n_heads = attn_width // HEAD_DIM
    width = SGU_GROUPS * GROUP_DIM
    h = x.reshape(s, d)
    fg = final_norm.reshape(1, d)
    for l in range(depth):
        wgu1, wd1 = _prep_ffn_weights(ffn1_w_gu[l], ffn1_w_down[l])
        wgu2, wd2 = _prep_ffn_weights(ffn2_w_gu[l], ffn2_w_down[l])
        w_in_l = w_in[l].astype(BF16)
        mix_g = mix_norm[l].reshape(1, d)
        sgu_bias = jnp.repeat(sgu_b[l].T, GROUP_DIM, axis=1)

        h = _ffn(h, ffn1_norm[l].reshape(1, d), wgu1, wd1, fg, final=False)
        qkv = _qkv(h, mix_g, w_in_l, attn_width)
        y_bc = _sgu_pool(h, mix_g, w_in_l, sgu_norm[l].reshape(1, width),
                         sgu_w[l], sgu_bias, pool_w[l].astype(BF16), pool_scale[l].reshape(1, width))
        y_a = _attention(qkv, n_heads)
        h = _outproj(h, y_a, y_bc, w_out[l].astype(BF16))
        h = _ffn(h, ffn2_norm[l].reshape(1, d), wgu2, wd2, fg, final=(l == depth - 1))
    return h.reshape(b, s, d)
```

```python
import functools
import math

import numpy as np
import jax
import jax.numpy as jnp
from jax import lax
from jax.experimental import pallas as pl
from jax.experimental.pallas import tpu as pltpu

F32 = jnp.float32
BF16 = jnp.bfloat16

HEAD_DIM = 128
SGU_GROUPS = 4
SGU_CHUNK = 128
POOL_WINDOWS = (2, 4, 8, 16)
GROUP_DIM = 128
EPS = 1e-6
FFN_RES_WEIGHT = 0.5

V7X_VMEM_BYTES = 64 * 1024 * 1024
VMEM_LIMIT = V7X_VMEM_BYTES - 8 * 1024 * 1024

FFN_ROWS = 512
FFN_COLS = 512
PROJ_ROWS = 512
QKV_COLS = 1024
ATTN_Q = 256
ATTN_HEADS = 2
LOG2_E = 1.4426950408889634
LOG2_ZERO_BOUND = -151.0


def _rmsnorm(x, g):
    inv = lax.rsqrt(jnp.mean(x * x, axis=-1, keepdims=True) + EPS)
    return x * inv * g


def _dot(a, b):
    return jnp.dot(a, b, preferred_element_type=F32)


def _ffn_kernel(x_ref, g_ref, wg_ref, wu_ref, wd_ref, fg_ref, o_ref, xn_ref, *, nj, final):
    j = pl.program_id(1)

    @pl.when(j == 0)
    def _():
        xn_ref[...] = _rmsnorm(x_ref[...], g_ref[...]).astype(BF16)

    def down_proj():
        xn = xn_ref[...]
        gate = _dot(xn, wg_ref[...])
        up = _dot(xn, wu_ref[...])
        h = (gate * jax.nn.sigmoid(gate) * up).astype(BF16)
        return _dot(h, wd_ref[...])

    @pl.when(j == 0)
    def _():
        o_ref[...] = down_proj()

    @pl.when(j > 0)
    def _():
        o_ref[...] += down_proj()

    @pl.when(j == nj - 1)
    def _():
        y = x_ref[...] + FFN_RES_WEIGHT * o_ref[...]
        if final:
            y = _rmsnorm(y, fg_ref[...])
        o_ref[...] = y


def _ffn(x, g, w_gu_p, w_down_p, fg, *, final):
    s, d = x.shape
    ffp = w_down_p.shape[0]
    nj = ffp // FFN_COLS
    tm = min(FFN_ROWS, s)
    return pl.pallas_call(
        functools.partial(_ffn_kernel, nj=nj, final=final),
        grid=(s // tm, nj),
        in_specs=[
            pl.BlockSpec((tm, d), lambda i, j: (i, 0)),
            pl.BlockSpec((1, d), lambda i, j: (0, 0)),
            pl.BlockSpec((d, FFN_COLS), lambda i, j: (0, j)),
            pl.BlockSpec((d, FFN_COLS), lambda i, j: (0, nj + j)),
            pl.BlockSpec((FFN_COLS, d), lambda i, j: (j, 0)),
            pl.BlockSpec((1, d), lambda i, j: (0, 0)),
        ],
        out_specs=pl.BlockSpec((tm, d), lambda i, j: (i, 0)),
        out_shape=jax.ShapeDtypeStruct((s, d), F32),
        scratch_shapes=[pltpu.VMEM((tm, d), BF16)],
        compiler_params=pltpu.CompilerParams(
            dimension_semantics=("parallel", "arbitrary"), vmem_limit_bytes=VMEM_LIMIT),
        name="ffn",
    )(x, g, w_gu_p, w_gu_p, w_down_p, fg)


def _qkv_kernel(x_ref, g_ref, w_ref, o_ref, xn_ref, *, q_blocks, scale):
    j = pl.program_id(1)

    @pl.when(j == 0)
    def _():
        xn_ref[...] = _rmsnorm(x_ref[...], g_ref[...]).astype(BF16)

    acc = _dot(xn_ref[...], w_ref[...])
    col_scale = jnp.where(j < q_blocks, jnp.float32(scale), jnp.float32(1.0))
    o_ref[...] = (acc * col_scale).astype(BF16)


def _qkv(x, g, w_in, attn_width):
    s, d = x.shape
    n = 3 * attn_width
    tm = min(PROJ_ROWS, s)
    return pl.pallas_call(
        functools.partial(_qkv_kernel, q_blocks=attn_width // QKV_COLS,
                          scale=LOG2_E / math.sqrt(HEAD_DIM)),
        grid=(s // tm, n // QKV_COLS),
        in_specs=[
            pl.BlockSpec((tm, d), lambda i, j: (i, 0)),
            pl.BlockSpec((1, d), lambda i, j: (0, 0)),
            pl.BlockSpec((d, QKV_COLS), lambda i, j: (0, j)),
        ],
        out_specs=pl.BlockSpec((tm, QKV_COLS), lambda i, j: (i, j)),
        out_shape=jax.ShapeDtypeStruct((s, n), BF16),
        scratch_shapes=[pltpu.VMEM((tm, d), BF16)],
        compiler_params=pltpu.CompilerParams(
            dimension_semantics=("parallel", "arbitrary"), vmem_limit_bytes=VMEM_LIMIT),
        name="qkv_proj",
    )(x, g, w_in)


def _sgu_pool_kernel(x_ref, g_ref, w_ref, sn_ref, sw_ref, sb_ref, band_ref, pw_ref, ps_ref,
                     o_ref, ph_ref, plo_ref, *, tm):
    i = pl.program_id(0)
    ch = SGU_CHUNK
    width = SGU_GROUPS * GROUP_DIM
    xn = _rmsnorm(x_ref[...], g_ref[...]).astype(BF16)
    proj = _dot(xn, w_ref[...])
    u = jax.nn.gelu(proj[:, :width])
    v = jax.nn.gelu(proj[:, width:2 * width])
    p = proj[:, 2 * width:]

    @pl.when(i == 0)
    def _():
        ph_ref[0:ch, :] = jnp.zeros((ch, width), BF16)
        plo_ref[0:ch, :] = jnp.zeros((ch, width), BF16)

    p_hi = p.astype(BF16)
    ph_ref[ch:ch + tm, :] = p_hi
    plo_ref[ch:ch + tm, :] = (p - p_hi.astype(F32)).astype(BF16)

    row = lax.broadcasted_iota(jnp.int32, (ch, ch), 0)
    col = lax.broadcasted_iota(jnp.int32, (ch, ch), 1)
    tril = col <= row
    for grp in range(SGU_GROUPS):
        lanes = slice(grp * GROUP_DIM, (grp + 1) * GROUP_DIM)
        vg = _rmsnorm(v[:, lanes], sn_ref[:, lanes]).astype(BF16)
        w_causal = jnp.where(tril, sw_ref[grp], 0.0).astype(BF16)
        window = POOL_WINDOWS[grp]
        band = band_ref[grp]
        pool_w = pw_ref[grp]
        for c in range(tm // ch):
            rows = slice(c * ch, (c + 1) * ch)
            mixed = _dot(w_causal, vg[rows]) + sb_ref[:, lanes]
            o_ref[rows, lanes] = (u[rows, lanes] * mixed).astype(BF16)

            ext = slice(c * ch, (c + 2) * ch)
            wsum = _dot(band, ph_ref[ext, lanes]) + _dot(band, plo_ref[ext, lanes])
            pos = i * tm + c * ch + row
            count = jnp.minimum(pos + 1, window).astype(F32)
            dlt = wsum / count - p[rows, lanes]
            y = _dot(dlt.astype(BF16), pool_w) * ps_ref[:, lanes]
            o_ref[rows, width + grp * GROUP_DIM:width + (grp + 1) * GROUP_DIM] = y.astype(BF16)

    ph_ref[0:ch, :] = ph_ref[tm:tm + ch, :]
    plo_ref[0:ch, :] = plo_ref[tm:tm + ch, :]


def _pool_bands():
    t = np.arange(SGU_CHUNK)[:, None] + SGU_CHUNK
    j = np.arange(2 * SGU_CHUNK)[None, :]
    bands = [((t - j >= 0) & (t - j < w)).astype(np.float32) for w in POOL_WINDOWS]
    return jnp.asarray(np.stack(bands), dtype=BF16)


def _sgu_pool(x, g, w_in, sgu_norm, sgu_w, sgu_bias, pool_w, pool_scale):
    s, d = x.shape
    width = SGU_GROUPS * GROUP_DIM
    tm = min(PROJ_ROWS, s)
    last_block = w_in.shape[1] // (3 * width) - 1
    assert (last_block + 1) * 3 * width == w_in.shape[1]
    full = lambda shape: pl.BlockSpec(shape, lambda i: (0,) * len(shape))
    return pl.pallas_call(
        functools.partial(_sgu_pool_kernel, tm=tm),
        grid=(s // tm,),
        in_specs=[
            pl.BlockSpec((tm, d), lambda i: (i, 0)),
            full((1, d)),
            pl.BlockSpec((d, 3 * width), lambda i: (0, last_block)),
            full((1, width)),
            full((SGU_GROUPS, SGU_CHUNK, SGU_CHUNK)),
            full((SGU_CHUNK, width)),
            full((SGU_GROUPS, SGU_CHUNK, 2 * SGU_CHUNK)),
            full((SGU_GROUPS, GROUP_DIM, GROUP_DIM)),
            full((1, width)),
        ],
        out_specs=pl.BlockSpec((tm, 2 * width), lambda i: (i, 0)),
        out_shape=jax.ShapeDtypeStruct((s, 2 * width), BF16),
        scratch_shapes=[pltpu.VMEM((tm + SGU_CHUNK, width), BF16),
                        pltpu.VMEM((tm + SGU_CHUNK, width), BF16)],
        compiler_params=pltpu.CompilerParams(
            dimension_semantics=("arbitrary",), vmem_limit_bytes=VMEM_LIMIT),
        name="sgu_pool",
    )(x, g, w_in, sgu_norm, sgu_w, sgu_bias, _pool_bands(), pool_w, pool_scale)


def _attn_block(q, ks, vs, m_neg, r, causal):
    tq, tk = q.shape[0], ks.shape[0]
    z2 = lax.dot_general(q, ks, (((1,), (1,)), ((), ())), preferred_element_type=F32)
    neg_abs = pltpu.bitcast(pltpu.bitcast(z2, jnp.uint32) | jnp.uint32(0x80000000), F32)
    sp = jnp.maximum(z2, 0.0) + jnp.log(1.0 + jnp.exp2(neg_abs)) * LOG2_E
    if causal is not None:
        sp = jnp.where(causal, sp, 0.0)
    hi = pltpu.bitcast(pltpu.bitcast(sp, jnp.uint32) & jnp.uint32(0xFFFF0000), F32)
    lo = sp - hi
    from_here = _dot(jnp.concatenate([hi.astype(BF16), lo.astype(BF16)], axis=1), m_neg)
    log_a = z2 + from_here
    if r is not None:
        log_a = log_a + jnp.concatenate([r] * (tk // HEAD_DIM), axis=1)
    a = jnp.exp2(log_a)
    if causal is not None:
        a = jnp.where(causal, a, 0.0)
    total = jnp.broadcast_to(-jnp.sum(sp, axis=1, keepdims=True), (tq, HEAD_DIM))
    return _dot(a.astype(BF16), vs), (total if r is None else r + total)


def _attn_kernel(q_ref, k_ref, v_ref, m_ref, o_ref, r_ref, acc_ref, *, tq):
    i = pl.program_id(1)
    m_neg = m_ref[...]
    heads = r_ref.shape[0]
    row = lax.broadcasted_iota(jnp.int32, (tq, tq), 0)
    col = lax.broadcasted_iota(jnp.int32, (tq, tq), 1)
    causal = col < row
    diag = pl.multiple_of(i * tq, tq)
    prev = pl.multiple_of(jnp.maximum(i - 1, 0) * tq, tq)
    no_prev = jnp.where(i == 0, jnp.float32(-1e30), jnp.float32(0.0))

    rmax0 = None
    for h in range(heads):
        hs = slice(h * HEAD_DIM, (h + 1) * HEAD_DIM)
        q = q_ref[:, hs]
        pv0, r = _attn_block(q, k_ref[pl.ds(diag, tq), hs], v_ref[pl.ds(diag, tq), hs], m_neg, None, causal)
        pv1, r = _attn_block(q, k_ref[pl.ds(prev, tq), hs], v_ref[pl.ds(prev, tq), hs], m_neg,
                             r + no_prev, None)
        acc_ref[h] = pv0 + pv1
        r_ref[h] = r
        hmax = jnp.max(r)
        rmax0 = hmax if rmax0 is None else jnp.maximum(rmax0, hmax)

    def cond(carry):
        j, rmax = carry
        return jnp.logical_and(j >= 0, rmax > LOG2_ZERO_BOUND)

    def body(carry):
        j, _ = carry
        start = pl.multiple_of(j * tq, tq)
        rmax = None
        for h in range(heads):
            hs = slice(h * HEAD_DIM, (h + 1) * HEAD_DIM)
            pv, r = _attn_block(q_ref[:, hs], k_ref[pl.ds(start, tq), hs], v_ref[pl.ds(start, tq), hs],
                                m_neg, r_ref[h], None)
            acc_ref[h] += pv
            r_ref[h] = r
            hmax = jnp.max(r)
            rmax = hmax if rmax is None else jnp.maximum(rmax, hmax)
        return j - 1, rmax

    lax.while_loop(cond, body, (i - 2, rmax0))
    for h in range(heads):
        o_ref[:, h * HEAD_DIM:(h + 1) * HEAD_DIM] = acc_ref[h].astype(BF16)


def _cumsum_matrix(ch):
    j = np.arange(ch)[:, None]
    s = np.arange(ch)[None, :]
    m = -(j >= s).astype(np.float32)
    return jnp.asarray(np.concatenate([m, m], axis=0), dtype=BF16)


def _attention(qkv, n_heads):
    s = qkv.shape[0]
    tq = ATTN_Q
    assert s % tq == 0 and s >= 2 * tq and n_heads % ATTN_HEADS == 0
    width = ATTN_HEADS * HEAD_DIM
    groups = n_heads // ATTN_HEADS
    return pl.pallas_call(
        functools.partial(_attn_kernel, tq=tq),
        grid=(groups, s // tq),
        in_specs=[
            pl.BlockSpec((tq, width), lambda g, i: (i, g)),
            pl.BlockSpec((s, width), lambda g, i: (0, groups + g)),
            pl.BlockSpec((s, width), lambda g, i: (0, 2 * groups + g)),
            pl.BlockSpec((2 * tq, tq), lambda g, i: (0, 0)),
        ],
        out_specs=pl.BlockSpec((tq, width), lambda g, i: (i, g)),
        out_shape=jax.ShapeDtypeStruct((s, n_heads * HEAD_DIM), BF16),
        scratch_shapes=[pltpu.VMEM((ATTN_HEADS, tq, HEAD_DIM), F32),
                        pltpu.VMEM((ATTN_HEADS, tq, HEAD_DIM), F32)],
        compiler_params=pltpu.CompilerParams(
            dimension_semantics=("parallel", "arbitrary"), vmem_limit_bytes=VMEM_LIMIT),
        name="stickbreak_attn",
    )(qkv, qkv, qkv, _cumsum_matrix(tq))


def _outproj_kernel(x_ref, ya_ref, ybc_ref, wa_ref, wb_ref, o_ref):
    o_ref[...] = x_ref[...] + _dot(ya_ref[...], wa_ref[...]) + _dot(ybc_ref[...], wb_ref[...])


def _outproj(x, y_a, y_bc, w_out):
    s, d = x.shape
    ka = y_a.shape[1]
    kb = y_bc.shape[1]
    assert ka == kb
    tm = min(PROJ_ROWS, s)
    return pl.pallas_call(
        _outproj_kernel,
        grid=(s // tm,),
        in_specs=[
            pl.BlockSpec((tm, d), lambda i: (i, 0)),
            pl.BlockSpec((tm, ka), lambda i: (i, 0)),
            pl.BlockSpec((tm, kb), lambda i: (i, 0)),
            pl.BlockSpec((ka, d), lambda i: (0, 0)),
            pl.BlockSpec((kb, d), lambda i: (1, 0)),
        ],
        out_specs=pl.BlockSpec((tm, d), lambda i: (i, 0)),
        out_shape=jax.ShapeDtypeStruct((s, d), F32),
        compiler_params=pltpu.CompilerParams(
            dimension_semantics=("parallel",), vmem_limit_bytes=VMEM_LIMIT),
        name="out_proj",
    )(x, y_a, y_bc, w_out, w_out)


def _prep_ffn_weights(w_gu, w_down):
    d_ff = w_down.shape[0]
    pad = (-d_ff) % FFN_COLS
    gate = jnp.pad(w_gu[:, :d_ff], ((0, 0), (0, pad)))
    up = jnp.pad(w_gu[:, d_ff:], ((0, 0), (0, pad)))
    w_gu_p = jnp.concatenate([gate, up], axis=1).astype(BF16)
    w_down_p = jnp.pad(w_down, ((0, pad), (0, 0))).astype(BF16)
    return w_gu_p, w_down_p


def kernel(x, ffn1_norm, ffn1_w_gu, ffn1_w_down, mix_norm, w_in, sgu_norm, sgu_w, sgu_b, pool_w,
           pool_scale, w_out, ffn2_norm, ffn2_w_gu, ffn2_w_down, final_norm):
    b, s, d = x.shape
    assert b == 1
    depth = w_in.shape[0]
    attn_width = w_out.shape[1] // 2
    n_heads = attn_width // HEAD_DIM
    width = SGU_GROUPS * GROUP_DIM
    h = x.reshape(s, d)
    fg = final_norm.reshape(1, d)
    for l in range(depth):
        wgu1, wd1 = _prep_ffn_weights(ffn1_w_gu[l], ffn1_w_down[l])
        wgu2, wd2 = _prep_ffn_weights(ffn2_w_gu[l], ffn2_w_down[l])
        w_in_l = w_in[l].astype(BF16)
        mix_g = mix_norm[l].reshape(1, d)
        sgu_bias = jnp.repeat(sgu_b[l].T, GROUP_DIM, axis=1)

        h = _ffn(h, ffn1_norm[l].reshape(1, d), wgu1, wd1, fg, final=False)
        qkv = _qkv(h, mix_g, w_in_l, attn_width)
        y_bc = _sgu_pool(h, mix_g, w_in_l, sgu_norm[l].reshape(1, width),
                         sgu_w[l], sgu_bias, pool_w[l].astype(BF16), pool_scale[l].reshape(1, width))
        y_a = _attention(qkv, n_heads)
        h = _outproj(h, y_a, y_bc, w_out[l].astype(BF16))
        h = _ffn(h, ffn2_norm[l].reshape(1, d), wgu2, wd2, fg, final=(l == depth - 1))
    return h.reshape(b, s, d)
```

```python
import functools
import math

import numpy as np
import jax
import jax.numpy as jnp
from jax import lax
from jax.experimental import pallas as pl
from jax.experimental.pallas import tpu as pltpu

F32 = jnp.float32
BF16 = jnp.bfloat16

HEAD_DIM = 128
SGU_GROUPS = 4
SGU_CHUNK = 128
POOL_WINDOWS = (2, 4, 8, 16)
GROUP_DIM = 128
EPS = 1e-6
FFN_RES_WEIGHT = 0.5

V7X_VMEM_BYTES = 64 * 1024 * 1024
VMEM_LIMIT = V7X_VMEM_BYTES - 8 * 1024 * 1024

FFN_ROWS = 1024
FFN_COLS = 256
PROJ_ROWS = 512
QKV_COLS = 1024
CAST_ROWS = 512
CAST_COLS = 512
ATTN_Q = 256
ATTN_HEADS = 2
LOG2_E = 1.4426950408889634
LOG2_ZERO_BOUND = -151.0


def _rmsnorm(x, g):
    inv = lax.rsqrt(jnp.mean(x * x, axis=-1, keepdims=True) + EPS)
    return x * inv * g


def _dot(a, b):
    return jnp.dot(a, b, preferred_element_type=F32)


def _ffn_kernel(x_ref, g_ref, wg_ref, wu_ref, wd_ref, fg_ref, o_ref, xn_ref, *, nj, final):
    j = pl.program_id(1)

    @pl.when(j == 0)
    def _():
        xn_ref[...] = _rmsnorm(x_ref[...], g_ref[...]).astype(BF16)

    def down_proj():
        xn = xn_ref[...]
        gate = _dot(xn, wg_ref[...])
        up = _dot(xn, wu_ref[...])
        h = (gate * jax.nn.sigmoid(gate) * up).astype(BF16)
        return _dot(h, wd_ref[...])

    @pl.when(j == 0)
    def _():
        o_ref[...] = down_proj()

    @pl.when(j > 0)
    def _():
        o_ref[...] += down_proj()

    @pl.when(j == nj - 1)
    def _():
        y = x_ref[...] + FFN_RES_WEIGHT * o_ref[...]
        if final:
            y = _rmsnorm(y, fg_ref[...])
        o_ref[...] = y


def _ffn(x, g, w_gu_p, w_down_p, fg, *, final):
    s, d = x.shape
    ffp = w_down_p.shape[0]
    nj = ffp // FFN_COLS
    tm = min(FFN_ROWS, s)
    return pl.pallas_call(
        functools.partial(_ffn_kernel, nj=nj, final=final),
        grid=(s // tm, nj),
        in_specs=[
            pl.BlockSpec((tm, d), lambda i, j: (i, 0)),
            pl.BlockSpec((1, d), lambda i, j: (0, 0)),
            pl.BlockSpec((d, FFN_COLS), lambda i, j: (0, j)),
            pl.BlockSpec((d, FFN_COLS), lambda i, j: (0, nj + j)),
            pl.BlockSpec((FFN_COLS, d), lambda i, j: (j, 0)),
            pl.BlockSpec((1, d), lambda i, j: (0, 0)),
        ],
        out_specs=pl.BlockSpec((tm, d), lambda i, j: (i, 0)),
        out_shape=jax.ShapeDtypeStruct((s, d), F32),
        scratch_shapes=[pltpu.VMEM((tm, d), BF16)],
        compiler_params=pltpu.CompilerParams(
            dimension_semantics=("parallel", "arbitrary"), vmem_limit_bytes=VMEM_LIMIT),
        name="ffn",
    )(x, g, w_gu_p, w_gu_p, w_down_p, fg)


def _inproj_kernel(x_ref, g_ref, w_ref, sn_ref, sw_ref, sb_ref, band_ref, pw_ref, ps_ref,
                   qkv_ref, o_ref, ph_ref, plo_ref, *, tm, attn_width, scale):
    i = pl.program_id(0)
    ch = SGU_CHUNK
    width = SGU_GROUPS * GROUP_DIM
    xn = _rmsnorm(x_ref[...], g_ref[...]).astype(BF16)
    for c in range(3 * attn_width // QKV_COLS):
        cols = slice(c * QKV_COLS, (c + 1) * QKV_COLS)
        acc = _dot(xn, w_ref[:, cols])
        if (c + 1) * QKV_COLS <= attn_width:
            acc = acc * scale
        qkv_ref[:, cols] = acc.astype(BF16)

    proj = _dot(xn, w_ref[:, 3 * attn_width:])
    u = jax.nn.gelu(proj[:, :width])
    v = jax.nn.gelu(proj[:, width:2 * width])
    p = proj[:, 2 * width:]

    @pl.when(i == 0)
    def _():
        ph_ref[0:ch, :] = jnp.zeros((ch, width), BF16)
        plo_ref[0:ch, :] = jnp.zeros((ch, width), BF16)

    p_hi = p.astype(BF16)
    ph_ref[ch:ch + tm, :] = p_hi
    plo_ref[ch:ch + tm, :] = (p - p_hi.astype(F32)).astype(BF16)

    row = lax.broadcasted_iota(jnp.int32, (ch, ch), 0)
    col = lax.broadcasted_iota(jnp.int32, (ch, ch), 1)
    tril = col <= row
    for grp in range(SGU_GROUPS):
        lanes = slice(grp * GROUP_DIM, (grp + 1) * GROUP_DIM)
        vg = _rmsnorm(v[:, lanes], sn_ref[:, lanes]).astype(BF16)
        w_causal = jnp.where(tril, sw_ref[grp], 0.0).astype(BF16)
        window = POOL_WINDOWS[grp]
        band = band_ref[grp]
        pool_w = pw_ref[grp]
        for c in range(tm // ch):
            rows = slice(c * ch, (c + 1) * ch)
            mixed = _dot(w_causal, vg[rows]) + sb_ref[:, lanes]
            o_ref[rows, lanes] = (u[rows, lanes] * mixed).astype(BF16)

            ext = slice(c * ch, (c + 2) * ch)
            wsum = _dot(band, ph_ref[ext, lanes]) + _dot(band, plo_ref[ext, lanes])
            pos = i * tm + c * ch + row
            count = jnp.minimum(pos + 1, window).astype(F32)
            dlt = wsum / count - p[rows, lanes]
            y = _dot(dlt.astype(BF16), pool_w) * ps_ref[:, lanes]
            o_ref[rows, width + grp * GROUP_DIM:width + (grp + 1) * GROUP_DIM] = y.astype(BF16)

    ph_ref[0:ch, :] = ph_ref[tm:tm + ch, :]
    plo_ref[0:ch, :] = plo_ref[tm:tm + ch, :]


def _pool_bands():
    t = np.arange(SGU_CHUNK)[:, None] + SGU_CHUNK
    j = np.arange(2 * SGU_CHUNK)[None, :]
    bands = [((t - j >= 0) & (t - j < w)).astype(np.float32) for w in POOL_WINDOWS]
    return jnp.asarray(np.stack(bands), dtype=BF16)


def _inproj(x, g, w_in, attn_width, sgu_norm, sgu_w, sgu_bias, pool_w, pool_scale):
    s, d = x.shape
    width = SGU_GROUPS * GROUP_DIM
    n_in = w_in.shape[1]
    assert n_in == 3 * attn_width + 3 * width and attn_width % QKV_COLS == 0
    tm = min(PROJ_ROWS, s)
    full = lambda shape: pl.BlockSpec(shape, lambda i: (0,) * len(shape))
    return pl.pallas_call(
        functools.partial(_inproj_kernel, tm=tm, attn_width=attn_width,
                          scale=LOG2_E / math.sqrt(HEAD_DIM)),
        grid=(s // tm,),
        in_specs=[
            pl.BlockSpec((tm, d), lambda i: (i, 0)),
            full((1, d)),
            pl.BlockSpec((d, n_in), lambda i: (0, 0), pipeline_mode=pl.Buffered(1)),
            full((1, width)),
            full((SGU_GROUPS, SGU_CHUNK, SGU_CHUNK)),
            full((SGU_CHUNK, width)),
            full((SGU_GROUPS, SGU_CHUNK, 2 * SGU_CHUNK)),
            full((SGU_GROUPS, GROUP_DIM, GROUP_DIM)),
            full((1, width)),
        ],
        out_specs=[pl.BlockSpec((tm, 3 * attn_width), lambda i: (i, 0)),
                   pl.BlockSpec((tm, 2 * width), lambda i: (i, 0))],
        out_shape=[jax.ShapeDtypeStruct((s, 3 * attn_width), BF16),
                   jax.ShapeDtypeStruct((s, 2 * width), BF16)],
        scratch_shapes=[pltpu.VMEM((tm + SGU_CHUNK, width), BF16),
                        pltpu.VMEM((tm + SGU_CHUNK, width), BF16)],
        compiler_params=pltpu.CompilerParams(
            dimension_semantics=("arbitrary",), vmem_limit_bytes=VMEM_LIMIT),
        name="in_proj",
    )(x, g, w_in, sgu_norm, sgu_w, sgu_bias, _pool_bands(), pool_w, pool_scale)


def _attn_block(q, ks, vs, m_neg, r, causal):
    tq, tk = q.shape[0], ks.shape[0]
    z2 = lax.dot_general(q, ks, (((1,), (1,)), ((), ())), preferred_element_type=F32)
    neg_abs = pltpu.bitcast(pltpu.bitcast(z2, jnp.uint32) | jnp.uint32(0x80000000), F32)
    sp = jnp.maximum(z2, 0.0) + jnp.log(1.0 + jnp.exp2(neg_abs)) * LOG2_E
    if causal is not None:
        sp = jnp.where(causal, sp, 0.0)
    hi = pltpu.bitcast(pltpu.bitcast(sp, jnp.uint32) & jnp.uint32(0xFFFF0000), F32)
    lo = sp - hi
    from_here = _dot(jnp.concatenate([hi.astype(BF16), lo.astype(BF16)], axis=1), m_neg)
    log_a = z2 + from_here
    if r is not None:
        log_a = log_a + jnp.concatenate([r] * (tk // HEAD_DIM), axis=1)
    a = jnp.exp2(log_a)
    if causal is not None:
        a = jnp.where(causal, a, 0.0)
    total = jnp.broadcast_to(-jnp.sum(sp, axis=1, keepdims=True), (tq, HEAD_DIM))
    return _dot(a.astype(BF16), vs), (total if r is None else r + total)


def _attn_kernel(q_ref, k_ref, v_ref, m_ref, o_ref, r_ref, acc_ref, *, tq):
    i = pl.program_id(1)
    m_neg = m_ref[...]
    heads = r_ref.shape[0]
    row = lax.broadcasted_iota(jnp.int32, (tq, tq), 0)
    col = lax.broadcasted_iota(jnp.int32, (tq, tq), 1)
    causal = col < row
    diag = pl.multiple_of(i * tq, tq)
    prev = pl.multiple_of(jnp.maximum(i - 1, 0) * tq, tq)
    no_prev = jnp.where(i == 0, jnp.float32(-1e30), jnp.float32(0.0))

    rmax0 = None
    for h in range(heads):
        hs = slice(h * HEAD_DIM, (h + 1) * HEAD_DIM)
        q = q_ref[:, hs]
        pv0, r = _attn_block(q, k_ref[pl.ds(diag, tq), hs], v_ref[pl.ds(diag, tq), hs], m_neg, None, causal)
        pv1, r = _attn_block(q, k_ref[pl.ds(prev, tq), hs], v_ref[pl.ds(prev, tq), hs], m_neg,
                             r + no_prev, None)
        acc_ref[h] = pv0 + pv1
        r_ref[h] = r
        hmax = jnp.max(r)
        rmax0 = hmax if rmax0 is None else jnp.maximum(rmax0, hmax)

    def cond(carry):
        j, rmax = carry
        return jnp.logical_and(j >= 0, rmax > LOG2_ZERO_BOUND)

    def body(carry):
        j, _ = carry
        start = pl.multiple_of(j * tq, tq)
        rmax = None
        for h in range(heads):
            hs = slice(h * HEAD_DIM, (h + 1) * HEAD_DIM)
            pv, r = _attn_block(q_ref[:, hs], k_ref[pl.ds(start, tq), hs], v_ref[pl.ds(start, tq), hs],
                                m_neg, r_ref[h], None)
            acc_ref[h] += pv
            r_ref[h] = r
            hmax = jnp.max(r)
            rmax = hmax if rmax is None else jnp.maximum(rmax, hmax)
        return j - 1, rmax

    lax.while_loop(cond, body, (i - 2, rmax0))
    for h in range(heads):
        o_ref[:, h * HEAD_DIM:(h + 1) * HEAD_DIM] = acc_ref[h].astype(BF16)


def _cumsum_matrix(ch):
    j = np.arange(ch)[:, None]
    s = np.arange(ch)[None, :]
    m = -(j >= s).astype(np.float32)
    return jnp.asarray(np.concatenate([m, m], axis=0), dtype=BF16)


def _attention(qkv, n_heads):
    s = qkv.shape[0]
    tq = ATTN_Q
    assert s % tq == 0 and s >= 2 * tq and n_heads % ATTN_HEADS == 0
    width = ATTN_HEADS * HEAD_DIM
    groups = n_heads // ATTN_HEADS
    return pl.pallas_call(
        functools.partial(_attn_kernel, tq=tq),
        grid=(groups, s // tq),
        in_specs=[
            pl.BlockSpec((tq, width), lambda g, i: (i, g)),
            pl.BlockSpec((s, width), lambda g, i: (0, groups + g)),
            pl.BlockSpec((s, width), lambda g, i: (0, 2 * groups + g)),
            pl.BlockSpec((2 * tq, tq), lambda g, i: (0, 0)),
        ],
        out_specs=pl.BlockSpec((tq, width), lambda g, i: (i, g)),
        out_shape=jax.ShapeDtypeStruct((s, n_heads * HEAD_DIM), BF16),
        scratch_shapes=[pltpu.VMEM((ATTN_HEADS, tq, HEAD_DIM), F32),
                        pltpu.VMEM((ATTN_HEADS, tq, HEAD_DIM), F32)],
        compiler_params=pltpu.CompilerParams(
            dimension_semantics=("parallel", "arbitrary"), vmem_limit_bytes=VMEM_LIMIT),
        name="stickbreak_attn",
    )(qkv, qkv, qkv, _cumsum_matrix(tq))


def _outproj_kernel(x_ref, ya_ref, ybc_ref, wa_ref, wb_ref, o_ref):
    o_ref[...] = x_ref[...] + _dot(ya_ref[...], wa_ref[...]) + _dot(ybc_ref[...], wb_ref[...])


def _outproj(x, y_a, y_bc, w_out):
    s, d = x.shape
    ka = y_a.shape[1]
    kb = y_bc.shape[1]
    assert ka == kb
    tm = min(PROJ_ROWS, s)
    return pl.pallas_call(
        _outproj_kernel,
        grid=(s // tm,),
        in_specs=[
            pl.BlockSpec((tm, d), lambda i: (i, 0)),
            pl.BlockSpec((tm, ka), lambda i: (i, 0)),
            pl.BlockSpec((tm, kb), lambda i: (i, 0)),
            pl.BlockSpec((ka, d), lambda i: (0, 0)),
            pl.BlockSpec((kb, d), lambda i: (1, 0)),
        ],
        out_specs=pl.BlockSpec((tm, d), lambda i: (i, 0)),
        out_shape=jax.ShapeDtypeStruct((s, d), F32),
        compiler_params=pltpu.CompilerParams(
            dimension_semantics=("parallel",), vmem_limit_bytes=VMEM_LIMIT),
        name="out_proj",
    )(x, y_a, y_bc, w_out, w_out)


def _cast_kernel(x_ref, o_ref):
    o_ref[...] = x_ref[...].astype(BF16)


def _cast_gate_up_kernel(x_ref, o_ref, *, d_ff, d_ffp):
    o_ref[:, :d_ff] = x_ref[:, :d_ff].astype(BF16)
    o_ref[:, d_ffp:d_ffp + d_ff] = x_ref[:, d_ff:].astype(BF16)
    if d_ffp > d_ff:
        zeros = jnp.zeros((o_ref.shape[0], d_ffp - d_ff), BF16)
        o_ref[:, d_ff:d_ffp] = zeros
        o_ref[:, d_ffp + d_ff:] = zeros


def _cast_down_kernel(x_ref, o_ref, *, d_ff):
    o_ref[:d_ff, :] = x_ref[...].astype(BF16)
    if o_ref.shape[0] > d_ff:
        o_ref[d_ff:, :] = jnp.zeros((o_ref.shape[0] - d_ff, o_ref.shape[1]), BF16)


def _cast_call(body, w, layer, in_block, out_block, out_shape, grid, index, name):
    return pl.pallas_call(
        body,
        grid=(grid,),
        in_specs=[pl.BlockSpec((None,) + in_block, lambda i: (layer,) + index(i))],
        out_specs=pl.BlockSpec(out_block, index),
        out_shape=jax.ShapeDtypeStruct(out_shape, BF16),
        compiler_params=pltpu.CompilerParams(
            dimension_semantics=("parallel",), vmem_limit_bytes=VMEM_LIMIT),
        name=name,
    )(w)


def _cast_plain(w, layer):
    _, rows, cols = w.shape
    rb = CAST_ROWS
    assert rows % rb == 0
    return _cast_call(_cast_kernel, w, layer, (rb, cols), (rb, cols), (rows, cols), rows // rb,
                      lambda i: (i, 0), "cast_bf16")


def _cast_ffn_weights(w_gu, w_down, layer):
    _, d, two_ff = w_gu.shape
    d_ff = two_ff // 2
    d_ffp = d_ff + (-d_ff) % FFN_COLS
    rb = CAST_ROWS // 2
    assert d % rb == 0 and d % CAST_COLS == 0
    w_gu_p = _cast_call(functools.partial(_cast_gate_up_kernel, d_ff=d_ff, d_ffp=d_ffp), w_gu, layer,
                        (rb, two_ff), (rb, 2 * d_ffp), (d, 2 * d_ffp), d // rb,
                        lambda i: (i, 0), "cast_gate_up")
    w_down_p = _cast_call(functools.partial(_cast_down_kernel, d_ff=d_ff), w_down, layer,
                          (d_ff, CAST_COLS), (d_ffp, CAST_COLS), (d_ffp, d), d // CAST_COLS,
                          lambda i: (0, i), "cast_down")
    return w_gu_p, w_down_p


def kernel(x, ffn1_norm, ffn1_w_gu, ffn1_w_down, mix_norm, w_in, sgu_norm, sgu_w, sgu_b, pool_w,
           pool_scale, w_out, ffn2_norm, ffn2_w_gu, ffn2_w_down, final_norm):
    b, s, d = x.shape
    assert b == 1
    depth = w_in.shape[0]
    attn_width = w_out.shape[1] // 2
    n_heads = attn_width // HEAD_DIM
    width = SGU_GROUPS * GROUP_DIM
    h = x.reshape(s, d)
    fg = final_norm.reshape(1, d)
    for l in range(depth):
        wgu1, wd1 = _cast_ffn_weights(ffn1_w_gu, ffn1_w_down, l)
        wgu2, wd2 = _cast_ffn_weights(ffn2_w_gu, ffn2_w_down, l)
        w_in_l = _cast_plain(w_in, l)
        mix_g = mix_norm[l].reshape(1, d)
        sgu_bias = jnp.repeat(sgu_b[l].T, GROUP_DIM, axis=1)

        h = _ffn(h, ffn1_norm[l].reshape(1, d), wgu1, wd1, fg, final=False)
        qkv, y_bc = _inproj(h, mix_g, w_in_l, attn_width, sgu_norm[l].reshape(1, width),
                            sgu_w[l], sgu_bias, pool_w[l].astype(BF16), pool_scale[l].reshape(1, width))
        y_a = _attention(qkv, n_heads)
        h = _outproj(h, y_a, y_bc, _cast_plain(w_out, l))
        h = _ffn(h, ffn2_norm[l].reshape(1, d), wgu2, wd2, fg, final=(l == depth - 1))
    return h.reshape(b, s, d)
```

```python
import functools
import math

import numpy as np
import jax
import jax.numpy as jnp
from jax import lax
from jax.experimental import pallas as pl
from jax.experimental.pallas import tpu as pltpu

F32 = jnp.float32
BF16 = jnp.bfloat16

HEAD_DIM = 128
SGU_GROUPS = 4
SGU_CHUNK = 128
POOL_WINDOWS = (2, 4, 8, 16)
GROUP_DIM = 128
EPS = 1e-6
FFN_RES_WEIGHT = 0.5

V7X_VMEM_BYTES = 64 * 1024 * 1024
VMEM_LIMIT = V7X_VMEM_BYTES - 6 * 1024 * 1024

FFN_ROWS = 1024
FFN_COLS = 512
PROJ_ROWS = 512
QKV_COLS = 1024
CAST_ROWS = 512
CAST_COLS = 512
ATTN_Q = 256
ATTN_HEADS = 2
LOG2_E = 1.4426950408889634
LOG2_ZERO_BOUND = -151.0


def _rmsnorm(x, g):
    inv = lax.rsqrt(jnp.mean(x * x, axis=-1, keepdims=True) + EPS)
    return x * inv * g


def _dot(a, b):
    return jnp.dot(a, b, preferred_element_type=F32)


def _ffn_kernel(x_ref, g_ref, wgu_ref, wd_ref, fg_ref, o_ref, xn_ref, *, nj, final):
    j = pl.program_id(1)
    tf = wd_ref.shape[0]

    @pl.when(j == 0)
    def _():
        xn_ref[...] = _rmsnorm(x_ref[...], g_ref[...]).astype(BF16)

    def down_proj():
        gu = _dot(xn_ref[...], wgu_ref[...])
        gate = gu[:, :tf]
        h = (gate * jax.nn.sigmoid(gate) * gu[:, tf:]).astype(BF16)
        return _dot(h, wd_ref[...])

    @pl.when(j == 0)
    def _():
        o_ref[...] = down_proj()

    @pl.when(j > 0)
    def _():
        o_ref[...] += down_proj()

    @pl.when(j == nj - 1)
    def _():
        y = x_ref[...] + FFN_RES_WEIGHT * o_ref[...]
        if final:
            y = _rmsnorm(y, fg_ref[...])
        o_ref[...] = y


def _ffn(x, g, w_gu_p, w_down_p, fg, *, final):
    s, d = x.shape
    nj = w_gu_p.shape[0]
    assert w_gu_p.shape == (nj, d, 2 * FFN_COLS) and w_down_p.shape == (nj * FFN_COLS, d)
    tm = min(FFN_ROWS, s)
    return pl.pallas_call(
        functools.partial(_ffn_kernel, nj=nj, final=final),
        grid=(s // tm, nj),
        in_specs=[
            pl.BlockSpec((tm, d), lambda i, j: (i, 0)),
            pl.BlockSpec((1, d), lambda i, j: (0, 0)),
            pl.BlockSpec((None, d, 2 * FFN_COLS), lambda i, j: (j, 0, 0)),
            pl.BlockSpec((FFN_COLS, d), lambda i, j: (j, 0)),
            pl.BlockSpec((1, d), lambda i, j: (0, 0)),
        ],
        out_specs=pl.BlockSpec((tm, d), lambda i, j: (i, 0)),
        out_shape=jax.ShapeDtypeStruct((s, d), F32),
        scratch_shapes=[pltpu.VMEM((tm, d), BF16)],
        compiler_params=pltpu.CompilerParams(
            dimension_semantics=("parallel", "arbitrary"), vmem_limit_bytes=VMEM_LIMIT),
        name="ffn",
    )(x, g, w_gu_p, w_down_p, fg)


def _inproj_kernel(x_ref, g_ref, w_ref, sn_ref, sw_ref, sb_ref, band_ref, pw_ref, ps_ref,
                   qkv_ref, o_ref, ph_ref, plo_ref, *, tm, attn_width, scale):
    i = pl.program_id(0)
    ch = SGU_CHUNK
    width = SGU_GROUPS * GROUP_DIM
    xn = _rmsnorm(x_ref[...], g_ref[...]).astype(BF16)
    for c in range(3 * attn_width // QKV_COLS):
        cols = slice(c * QKV_COLS, (c + 1) * QKV_COLS)
        acc = _dot(xn, w_ref[:, cols])
        if (c + 1) * QKV_COLS <= attn_width:
            acc = acc * scale
        qkv_ref[:, cols] = acc.astype(BF16)

    proj = _dot(xn, w_ref[:, 3 * attn_width:])
    u = jax.nn.gelu(proj[:, :width])
    v = jax.nn.gelu(proj[:, width:2 * width])
    p = proj[:, 2 * width:]

    @pl.when(i == 0)
    def _():
        ph_ref[0:ch, :] = jnp.zeros((ch, width), BF16)
        plo_ref[0:ch, :] = jnp.zeros((ch, width), BF16)

    p_hi = p.astype(BF16)
    ph_ref[ch:ch + tm, :] = p_hi
    plo_ref[ch:ch + tm, :] = (p - p_hi.astype(F32)).astype(BF16)

    row = lax.broadcasted_iota(jnp.int32, (ch, ch), 0)
    col = lax.broadcasted_iota(jnp.int32, (ch, ch), 1)
    tril = col <= row
    for grp in range(SGU_GROUPS):
        lanes = slice(grp * GROUP_DIM, (grp + 1) * GROUP_DIM)
        vg = _rmsnorm(v[:, lanes], sn_ref[:, lanes]).astype(BF16)
        w_causal = jnp.where(tril, sw_ref[grp], 0.0).astype(BF16)
        window = POOL_WINDOWS[grp]
        band = band_ref[grp]
        pool_w = pw_ref[grp]
        for c in range(tm // ch):
            rows = slice(c * ch, (c + 1) * ch)
            mixed = _dot(w_causal, vg[rows]) + sb_ref[:, lanes]
            o_ref[rows, lanes] = (u[rows, lanes] * mixed).astype(BF16)

            ext = slice(c * ch, (c + 2) * ch)
            wsum = _dot(band, ph_ref[ext, lanes]) + _dot(band, plo_ref[ext, lanes])
            pos = i * tm + c * ch + row
            count = jnp.minimum(pos + 1, window).astype(F32)
            dlt = wsum / count - p[rows, lanes]
            y = _dot(dlt.astype(BF16), pool_w) * ps_ref[:, lanes]
            o_ref[rows, width + grp * GROUP_DIM:width + (grp + 1) * GROUP_DIM] = y.astype(BF16)

    ph_ref[0:ch, :] = ph_ref[tm:tm + ch, :]
    plo_ref[0:ch, :] = plo_ref[tm:tm + ch, :]


def _pool_bands():
    t = np.arange(SGU_CHUNK)[:, None] + SGU_CHUNK
    j = np.arange(2 * SGU_CHUNK)[None, :]
    bands = [((t - j >= 0) & (t - j < w)).astype(np.float32) for w in POOL_WINDOWS]
    return jnp.asarray(np.stack(bands), dtype=BF16)


def _inproj(x, g, w_in, attn_width, sgu_norm, sgu_w, sgu_bias, pool_w, pool_scale):
    s, d = x.shape
    width = SGU_GROUPS * GROUP_DIM
    n_in = w_in.shape[1]
    assert n_in == 3 * attn_width + 3 * width and attn_width % QKV_COLS == 0
    tm = min(PROJ_ROWS, s)
    full = lambda shape: pl.BlockSpec(shape, lambda i: (0,) * len(shape))
    return pl.pallas_call(
        functools.partial(_inproj_kernel, tm=tm, attn_width=attn_width,
                          scale=LOG2_E / math.sqrt(HEAD_DIM)),
        grid=(s // tm,),
        in_specs=[
            pl.BlockSpec((tm, d), lambda i: (i, 0)),
            full((1, d)),
            pl.BlockSpec((d, n_in), lambda i: (0, 0), pipeline_mode=pl.Buffered(1)),
            full((1, width)),
            full((SGU_GROUPS, SGU_CHUNK, SGU_CHUNK)),
            full((SGU_CHUNK, width)),
            full((SGU_GROUPS, SGU_CHUNK, 2 * SGU_CHUNK)),
            full((SGU_GROUPS, GROUP_DIM, GROUP_DIM)),
            full((1, width)),
        ],
        out_specs=[pl.BlockSpec((tm, 3 * attn_width), lambda i: (i, 0)),
                   pl.BlockSpec((tm, 2 * width), lambda i: (i, 0))],
        out_shape=[jax.ShapeDtypeStruct((s, 3 * attn_width), BF16),
                   jax.ShapeDtypeStruct((s, 2 * width), BF16)],
        scratch_shapes=[pltpu.VMEM((tm + SGU_CHUNK, width), BF16),
                        pltpu.VMEM((tm + SGU_CHUNK, width), BF16)],
        compiler_params=pltpu.CompilerParams(
            dimension_semantics=("arbitrary",), vmem_limit_bytes=VMEM_LIMIT),
        name="in_proj",
    )(x, g, w_in, sgu_norm, sgu_w, sgu_bias, _pool_bands(), pool_w, pool_scale)


def _attn_block(q, ks, vs, m_neg, r, causal):
    tq, tk = q.shape[0], ks.shape[0]
    z2 = lax.dot_general(q, ks, (((1,), (1,)), ((), ())), preferred_element_type=F32)
    neg_abs = pltpu.bitcast(pltpu.bitcast(z2, jnp.uint32) | jnp.uint32(0x80000000), F32)
    sp = jnp.maximum(z2, 0.0) + jnp.log(1.0 + jnp.exp2(neg_abs)) * LOG2_E
    if causal is not None:
        sp = jnp.where(causal, sp, 0.0)
    hi = pltpu.bitcast(pltpu.bitcast(sp, jnp.uint32) & jnp.uint32(0xFFFF0000), F32)
    lo = sp - hi
    from_here = _dot(jnp.concatenate([hi.astype(BF16), lo.astype(BF16)], axis=1), m_neg)
    log_a = z2 + from_here
    if r is not None:
        log_a = log_a + jnp.concatenate([r] * (tk // HEAD_DIM), axis=1)
    a = jnp.exp2(log_a)
    if causal is not None:
        a = jnp.where(causal, a, 0.0)
    total = jnp.broadcast_to(-jnp.sum(sp, axis=1, keepdims=True), (tq, HEAD_DIM))
    return _dot(a.astype(BF16), vs), (total if r is None else r + total)


def _attn_kernel(q_ref, k_ref, v_ref, m_ref, o_ref, r_ref, acc_ref, *, tq):
    i = pl.program_id(1)
    m_neg = m_ref[...]
    heads = r_ref.shape[0]
    row = lax.broadcasted_iota(jnp.int32, (tq, tq), 0)
    col = lax.broadcasted_iota(jnp.int32, (tq, tq), 1)
    causal = col < row
    diag = pl.multiple_of(i * tq, tq)
    prev = pl.multiple_of(jnp.maximum(i - 1, 0) * tq, tq)
    no_prev = jnp.where(i == 0, jnp.float32(-1e30), jnp.float32(0.0))

    rmax0 = None
    for h in range(heads):
        hs = slice(h * HEAD_DIM, (h + 1) * HEAD_DIM)
        q = q_ref[:, hs]
        pv0, r = _attn_block(q, k_ref[pl.ds(diag, tq), hs], v_ref[pl.ds(diag, tq), hs], m_neg, None, causal)
        pv1, r = _attn_block(q, k_ref[pl.ds(prev, tq), hs], v_ref[pl.ds(prev, tq), hs], m_neg,
                             r + no_prev, None)
        acc_ref[h] = pv0 + pv1
        r_ref[h] = r
        hmax = jnp.max(r)
        rmax0 = hmax if rmax0 is None else jnp.maximum(rmax0, hmax)

    def cond(carry):
        j, rmax = carry
        return jnp.logical_and(j >= 0, rmax > LOG2_ZERO_BOUND)

    def body(carry):
        j, _ = carry
        start = pl.multiple_of(j * tq, tq)
        rmax = None
        for h in range(heads):
            hs = slice(h * HEAD_DIM, (h + 1) * HEAD_DIM)
            pv, r = _attn_block(q_ref[:, hs], k_ref[pl.ds(start, tq), hs], v_ref[pl.ds(start, tq), hs],
                                m_neg, r_ref[h], None)
            acc_ref[h] += pv
            r_ref[h] = r
            hmax = jnp.max(r)
            rmax = hmax if rmax is None else jnp.maximum(rmax, hmax)
        return j - 1, rmax

    lax.while_loop(cond, body, (i - 2, rmax0))
    for h in range(heads):
        o_ref[:, h * HEAD_DIM:(h + 1) * HEAD_DIM] = acc_ref[h].astype(BF16)


def _cumsum_matrix(ch):
    j = np.arange(ch)[:, None]
    s = np.arange(ch)[None, :]
    m = -(j >= s).astype(np.float32)
    return jnp.asarray(np.concatenate([m, m], axis=0), dtype=BF16)


def _attention(qkv, n_heads):
    s = qkv.shape[0]
    tq = ATTN_Q
    assert s % tq == 0 and s >= 2 * tq and n_heads % ATTN_HEADS == 0
    width = ATTN_HEADS * HEAD_DIM
    groups = n_heads // ATTN_HEADS
    return pl.pallas_call(
        functools.partial(_attn_kernel, tq=tq),
        grid=(groups, s // tq),
        in_specs=[
            pl.BlockSpec((tq, width), lambda g, i: (i, g)),
            pl.BlockSpec((s, width), lambda g, i: (0, groups + g)),
            pl.BlockSpec((s, width), lambda g, i: (0, 2 * groups + g)),
            pl.BlockSpec((2 * tq, tq), lambda g, i: (0, 0)),
        ],
        out_specs=pl.BlockSpec((tq, width), lambda g, i: (i, g)),
        out_shape=jax.ShapeDtypeStruct((s, n_heads * HEAD_DIM), BF16),
        scratch_shapes=[pltpu.VMEM((ATTN_HEADS, tq, HEAD_DIM), F32),
                        pltpu.VMEM((ATTN_HEADS, tq, HEAD_DIM), F32)],
        compiler_params=pltpu.CompilerParams(
            dimension_semantics=("parallel", "arbitrary"), vmem_limit_bytes=VMEM_LIMIT),
        name="stickbreak_attn",
    )(qkv, qkv, qkv, _cumsum_matrix(tq))


def _outproj_kernel(x_ref, ya_ref, ybc_ref, wa_ref, wb_ref, o_ref):
    o_ref[...] = x_ref[...] + _dot(ya_ref[...], wa_ref[...]) + _dot(ybc_ref[...], wb_ref[...])


def _outproj(x, y_a, y_bc, w_out):
    s, d = x.shape
    ka = y_a.shape[1]
    kb = y_bc.shape[1]
    assert ka == kb
    tm = min(PROJ_ROWS, s)
    return pl.pallas_call(
        _outproj_kernel,
        grid=(s // tm,),
        in_specs=[
            pl.BlockSpec((tm, d), lambda i: (i, 0)),
            pl.BlockSpec((tm, ka), lambda i: (i, 0)),
            pl.BlockSpec((tm, kb), lambda i: (i, 0)),
            pl.BlockSpec((ka, d), lambda i: (0, 0)),
            pl.BlockSpec((kb, d), lambda i: (1, 0)),
        ],
        out_specs=pl.BlockSpec((tm, d), lambda i: (i, 0)),
        out_shape=jax.ShapeDtypeStruct((s, d), F32),
        compiler_params=pltpu.CompilerParams(
            dimension_semantics=("parallel",), vmem_limit_bytes=VMEM_LIMIT),
        name="out_proj",
    )(x, y_a, y_bc, w_out, w_out)


def _cast_kernel(x_ref, o_ref):
    o_ref[...] = x_ref[...].astype(BF16)


def _cast_gate_up_kernel(x_ref, o_ref, *, d_ff):
    nj, rows, two_tf = o_ref.shape
    tf = two_tf // 2
    for j in range(nj):
        n = min(tf, d_ff - j * tf)
        o_ref[j, :, :n] = x_ref[:, j * tf:j * tf + n].astype(BF16)
        o_ref[j, :, tf:tf + n] = x_ref[:, d_ff + j * tf:d_ff + j * tf + n].astype(BF16)
        if n < tf:
            zeros = jnp.zeros((rows, tf - n), BF16)
            o_ref[j, :, n:tf] = zeros
            o_ref[j, :, tf + n:] = zeros


def _cast_down_kernel(x_ref, o_ref, *, d_ff):
    o_ref[:d_ff, :] = x_ref[...].astype(BF16)
    if o_ref.shape[0] > d_ff:
        o_ref[d_ff:, :] = jnp.zeros((o_ref.shape[0] - d_ff, o_ref.shape[1]), BF16)


def _cast_call(body, w, layer, in_block, out_block, out_shape, grid, index, name, out_index=None):
    return pl.pallas_call(
        body,
        grid=(grid,),
        in_specs=[pl.BlockSpec((None,) + in_block, lambda i: (layer,) + index(i))],
        out_specs=pl.BlockSpec(out_block, out_index or index),
        out_shape=jax.ShapeDtypeStruct(out_shape, BF16),
        compiler_params=pltpu.CompilerParams(
            dimension_semantics=("parallel",), vmem_limit_bytes=VMEM_LIMIT),
        name=name,
    )(w)


def _cast_plain(w, layer):
    _, rows, cols = w.shape
    rb = CAST_ROWS
    assert rows % rb == 0
    return _cast_call(_cast_kernel, w, layer, (rb, cols), (rb, cols), (rows, cols), rows // rb,
                      lambda i: (i, 0), "cast_bf16")


def _cast_ffn_weights(w_gu, w_down, layer):
    _, d, two_ff = w_gu.shape
    d_ff = two_ff // 2
    d_ffp = d_ff + (-d_ff) % FFN_COLS
    rb = CAST_ROWS // 2
    nj = d_ffp // FFN_COLS
    assert d % rb == 0 and d % CAST_COLS == 0
    w_gu_p = _cast_call(functools.partial(_cast_gate_up_kernel, d_ff=d_ff), w_gu, layer,
                        (rb, two_ff), (nj, rb, 2 * FFN_COLS), (nj, d, 2 * FFN_COLS), d // rb,
                        lambda i: (i, 0), "cast_gate_up", out_index=lambda i: (0, i, 0))
    w_down_p = _cast_call(functools.partial(_cast_down_kernel, d_ff=d_ff), w_down, layer,
                          (d_ff, CAST_COLS), (d_ffp, CAST_COLS), (d_ffp, d), d // CAST_COLS,
                          lambda i: (0, i), "cast_down")
    return w_gu_p, w_down_p


def kernel(x, ffn1_norm, ffn1_w_gu, ffn1_w_down, mix_norm, w_in, sgu_norm, sgu_w, sgu_b, pool_w,
           pool_scale, w_out, ffn2_norm, ffn2_w_gu, ffn2_w_down, final_norm):
    b, s, d = x.shape
    assert b == 1
    depth = w_in.shape[0]
    attn_width = w_out.shape[1] // 2
    n_heads = attn_width // HEAD_DIM
    width = SGU_GROUPS * GROUP_DIM
    h = x.reshape(s, d)
    fg = final_norm.reshape(1, d)
    for l in range(depth):
        wgu1, wd1 = _cast_ffn_weights(ffn1_w_gu, ffn1_w_down, l)
        wgu2, wd2 = _cast_ffn_weights(ffn2_w_gu, ffn2_w_down, l)
        w_in_l = _cast_plain(w_in, l)
        mix_g = mix_norm[l].reshape(1, d)
        sgu_bias = jnp.repeat(sgu_b[l].T, GROUP_DIM, axis=1)

        h = _ffn(h, ffn1_norm[l].reshape(1, d), wgu1, wd1, fg, final=False)
        qkv, y_bc = _inproj(h, mix_g, w_in_l, attn_width, sgu_norm[l].reshape(1, width),
                            sgu_w[l], sgu_bias, pool_w[l].astype(BF16), pool_scale[l].reshape(1, width))
        y_a = _attention(qkv, n_heads)
        h = _outproj(h, y_a, y_bc, _cast_plain(w_out, l))
        h = _ffn(h, ffn2_norm[l].reshape(1, d), wgu2, wd2, fg, final=(l == depth - 1))
    return h.reshape(b, s, d)
```

```python
import functools
import math

import numpy as np
import jax
import jax.numpy as jnp
from jax import lax
from jax.experimental import pallas as pl
from jax.experimental.pallas import tpu as pltpu

F32 = jnp.float32
BF16 = jnp.bfloat16

HEAD_DIM = 128
SGU_GROUPS = 4
SGU_CHUNK = 128
POOL_WINDOWS = (2, 4, 8, 16)
GROUP_DIM = 128
EPS = 1e-6
FFN_RES_WEIGHT = 0.5
assert math.frexp(FFN_RES_WEIGHT)[0] == 0.5

V7X_VMEM_BYTES = 64 * 1024 * 1024
VMEM_LIMIT = V7X_VMEM_BYTES - 6 * 1024 * 1024

FFN_ROWS = 1024
FFN_COLS = 512
PROJ_ROWS = 512
QKV_COLS = 1024
CAST_ROWS = 512
CAST_COLS = 512
ATTN_Q = 256
ATTN_HEADS = 2
LOG2_E = 1.4426950408889634
LOG2_ZERO_BOUND = -151.0


def _rmsnorm(x, g):
    inv = lax.rsqrt(jnp.mean(x * x, axis=-1, keepdims=True) + EPS)
    return x * inv * g


def _dot(a, b):
    return jnp.dot(a, b, preferred_element_type=F32)


def _ffn_kernel(x_ref, g_ref, wgu_ref, wd_ref, fg_ref, o_ref, xn_ref, *, nj, final):
    j = pl.program_id(1)
    tf = wd_ref.shape[0]

    @pl.when(j == 0)
    def _():
        xn_ref[...] = _rmsnorm(x_ref[...], g_ref[...]).astype(BF16)

    def down_proj():
        gu = _dot(xn_ref[...], wgu_ref[...])
        gate = gu[:, :tf]
        h = (gate * jax.nn.sigmoid(gate) * gu[:, tf:]).astype(BF16)
        return _dot(h, wd_ref[...])

    @pl.when(j == 0)
    def _():
        o_ref[...] = x_ref[...] + down_proj()

    @pl.when(j > 0)
    def _():
        o_ref[...] += down_proj()

    if final:
        @pl.when(j == nj - 1)
        def _():
            o_ref[...] = _rmsnorm(o_ref[...], fg_ref[...])


def _ffn(x, g, w_gu_p, w_down_p, fg, *, final):
    s, d = x.shape
    nj = w_gu_p.shape[0]
    assert w_gu_p.shape == (nj, d, 2 * FFN_COLS) and w_down_p.shape == (nj * FFN_COLS, d)
    tm = min(FFN_ROWS, s)
    ni = s // tm
    x_index = lambda i, j: (jnp.minimum(i + jnp.minimum(j, 1), ni - 1), 0)
    return pl.pallas_call(
        functools.partial(_ffn_kernel, nj=nj, final=final),
        grid=(ni, nj),
        in_specs=[
            pl.BlockSpec((tm, d), x_index),
            pl.BlockSpec((1, d), lambda i, j: (0, 0)),
            pl.BlockSpec((None, d, 2 * FFN_COLS), lambda i, j: (j, 0, 0)),
            pl.BlockSpec((FFN_COLS, d), lambda i, j: (j, 0)),
            pl.BlockSpec((1, d), lambda i, j: (0, 0)),
        ],
        out_specs=pl.BlockSpec((tm, d), lambda i, j: (i, 0)),
        out_shape=jax.ShapeDtypeStruct((s, d), F32),
        scratch_shapes=[pltpu.VMEM((tm, d), BF16)],
        compiler_params=pltpu.CompilerParams(
            dimension_semantics=("arbitrary", "arbitrary"), vmem_limit_bytes=VMEM_LIMIT),
        name="ffn",
    )(x, g, w_gu_p, w_down_p, fg)


def _inproj_kernel(x_ref, g_ref, w_ref, sn_ref, sw_ref, sb_ref, band_ref, pw_ref, ps_ref,
                   qkv_ref, o_ref, ph_ref, plo_ref, *, tm, attn_width, scale):
    i = pl.program_id(0)
    ch = SGU_CHUNK
    width = SGU_GROUPS * GROUP_DIM
    xn = _rmsnorm(x_ref[...], g_ref[...]).astype(BF16)
    proj = _dot(xn, w_ref[:, 3 * attn_width:])
    u = jax.nn.gelu(proj[:, :width])
    v = jax.nn.gelu(proj[:, width:2 * width])
    p = proj[:, 2 * width:]

    @pl.when(i == 0)
    def _():
        ph_ref[0:ch, :] = jnp.zeros((ch, width), BF16)
        plo_ref[0:ch, :] = jnp.zeros((ch, width), BF16)

    p_hi = p.astype(BF16)
    ph_ref[ch:ch + tm, :] = p_hi
    plo_ref[ch:ch + tm, :] = (p - p_hi.astype(F32)).astype(BF16)

    row = lax.broadcasted_iota(jnp.int32, (ch, ch), 0)
    col = lax.broadcasted_iota(jnp.int32, (ch, ch), 1)
    tril = col <= row
    for grp in range(SGU_GROUPS):
        lanes = slice(grp * GROUP_DIM, (grp + 1) * GROUP_DIM)
        vg = _rmsnorm(v[:, lanes], sn_ref[:, lanes]).astype(BF16)
        w_causal = jnp.where(tril, sw_ref[grp], 0.0).astype(BF16)
        window = POOL_WINDOWS[grp]
        band = band_ref[grp]
        pool_w = pw_ref[grp]
        for c in range(tm // ch):
            rows = slice(c * ch, (c + 1) * ch)
            mixed = _dot(w_causal, vg[rows]) + sb_ref[:, lanes]
            o_ref[rows, lanes] = (u[rows, lanes] * mixed).astype(BF16)

            ext = slice(c * ch, (c + 2) * ch)
            wsum = _dot(band, ph_ref[ext, lanes]) + _dot(band, plo_ref[ext, lanes])
            pos = i * tm + c * ch + row
            count = jnp.minimum(pos + 1, window).astype(F32)
            dlt = wsum / count - p[rows, lanes]
            y = _dot(dlt.astype(BF16), pool_w) * ps_ref[:, lanes]
            o_ref[rows, width + grp * GROUP_DIM:width + (grp + 1) * GROUP_DIM] = y.astype(BF16)

    ph_ref[0:ch, :] = ph_ref[tm:tm + ch, :]
    plo_ref[0:ch, :] = plo_ref[tm:tm + ch, :]

    for c in range(3 * attn_width // QKV_COLS):
        cols = slice(c * QKV_COLS, (c + 1) * QKV_COLS)
        acc = _dot(xn, w_ref[:, cols])
        if (c + 1) * QKV_COLS <= attn_width:
            acc = acc * scale
        qkv_ref[:, cols] = acc.astype(BF16)


def _pool_bands():
    t = np.arange(SGU_CHUNK)[:, None] + SGU_CHUNK
    j = np.arange(2 * SGU_CHUNK)[None, :]
    bands = [((t - j >= 0) & (t - j < w)).astype(np.float32) for w in POOL_WINDOWS]
    return jnp.asarray(np.stack(bands), dtype=BF16)


def _inproj(x, g, w_in, attn_width, sgu_norm, sgu_w, sgu_bias, pool_w, pool_scale):
    s, d = x.shape
    width = SGU_GROUPS * GROUP_DIM
    n_in = w_in.shape[1]
    assert n_in == 3 * attn_width + 3 * width and attn_width % QKV_COLS == 0
    tm = min(PROJ_ROWS, s)
    full = lambda shape: pl.BlockSpec(shape, lambda i: (0,) * len(shape))
    return pl.pallas_call(
        functools.partial(_inproj_kernel, tm=tm, attn_width=attn_width,
                          scale=LOG2_E / math.sqrt(HEAD_DIM)),
        grid=(s // tm,),
        in_specs=[
            pl.BlockSpec((tm, d), lambda i: (i, 0)),
            full((1, d)),
            pl.BlockSpec((d, n_in), lambda i: (0, 0), pipeline_mode=pl.Buffered(1)),
            full((1, width)),
            full((SGU_GROUPS, SGU_CHUNK, SGU_CHUNK)),
            full((SGU_CHUNK, width)),
            full((SGU_GROUPS, SGU_CHUNK, 2 * SGU_CHUNK)),
            full((SGU_GROUPS, GROUP_DIM, GROUP_DIM)),
            full((1, width)),
        ],
        out_specs=[pl.BlockSpec((tm, 3 * attn_width), lambda i: (i, 0)),
                   pl.BlockSpec((tm, 2 * width), lambda i: (i, 0))],
        out_shape=[jax.ShapeDtypeStruct((s, 3 * attn_width), BF16),
                   jax.ShapeDtypeStruct((s, 2 * width), BF16)],
        scratch_shapes=[pltpu.VMEM((tm + SGU_CHUNK, width), BF16),
                        pltpu.VMEM((tm + SGU_CHUNK, width), BF16)],
        compiler_params=pltpu.CompilerParams(
            dimension_semantics=("arbitrary",), vmem_limit_bytes=VMEM_LIMIT),
        name="in_proj",
    )(x, g, w_in, sgu_norm, sgu_w, sgu_bias, _pool_bands(), pool_w, pool_scale)


def _attn_block(q, ks, vs, m_neg, r, causal):
    tq, tk = q.shape[0], ks.shape[0]
    z2 = lax.dot_general(q, ks, (((1,), (1,)), ((), ())), preferred_element_type=F32)
    neg_abs = pltpu.bitcast(pltpu.bitcast(z2, jnp.uint32) | jnp.uint32(0x80000000), F32)
    sp = jnp.maximum(z2, 0.0) + jnp.log(1.0 + jnp.exp2(neg_abs)) * LOG2_E
    if causal is not None:
        sp = jnp.where(causal, sp, 0.0)
    hi = pltpu.bitcast(pltpu.bitcast(sp, jnp.uint32) & jnp.uint32(0xFFFF0000), F32)
    lo = sp - hi
    from_here = _dot(jnp.concatenate([hi.astype(BF16), lo.astype(BF16)], axis=1), m_neg)
    log_a = z2 + from_here
    if r is not None:
        log_a = log_a + jnp.concatenate([r] * (tk // HEAD_DIM), axis=1)
    a = jnp.exp2(log_a)
    if causal is not None:
        a = jnp.where(causal, a, 0.0)
    total = jnp.broadcast_to(-jnp.sum(sp, axis=1, keepdims=True), (tq, HEAD_DIM))
    return _dot(a.astype(BF16), vs), (total if r is None else r + total)


def _attn_kernel(q_ref, k_ref, v_ref, m_ref, o_ref, r_ref, acc_ref, *, tq):
    i = pl.program_id(1)
    m_neg = m_ref[...]
    heads = r_ref.shape[0]
    row = lax.broadcasted_iota(jnp.int32, (tq, tq), 0)
    col = lax.broadcasted_iota(jnp.int32, (tq, tq), 1)
    causal = col < row
    diag = pl.multiple_of(i * tq, tq)
    prev = pl.multiple_of(jnp.maximum(i - 1, 0) * tq, tq)
    no_prev = jnp.where(i == 0, jnp.float32(-1e30), jnp.float32(0.0))

    rmax0 = None
    for h in range(heads):
        hs = slice(h * HEAD_DIM, (h + 1) * HEAD_DIM)
        q = q_ref[:, hs]
        pv0, r = _attn_block(q, k_ref[pl.ds(diag, tq), hs], v_ref[pl.ds(diag, tq), hs], m_neg, None, causal)
        pv1, r = _attn_block(q, k_ref[pl.ds(prev, tq), hs], v_ref[pl.ds(prev, tq), hs], m_neg,
                             r + no_prev, None)
        acc_ref[h] = pv0 + pv1
        r_ref[h] = r
        hmax = jnp.max(r)
        rmax0 = hmax if rmax0 is None else jnp.maximum(rmax0, hmax)

    def cond(carry):
        j, rmax = carry
        return jnp.logical_and(j >= 0, rmax > LOG2_ZERO_BOUND)

    def body(carry):
        j, _ = carry
        start = pl.multiple_of(j * tq, tq)
        rmax = None
        for h in range(heads):
            hs = slice(h * HEAD_DIM, (h + 1) * HEAD_DIM)
            pv, r = _attn_block(q_ref[:, hs], k_ref[pl.ds(start, tq), hs], v_ref[pl.ds(start, tq), hs],
                                m_neg, r_ref[h], None)
            acc_ref[h] += pv
            r_ref[h] = r
            hmax = jnp.max(r)
            rmax = hmax if rmax is None else jnp.maximum(rmax, hmax)
        return j - 1, rmax

    lax.while_loop(cond, body, (i - 2, rmax0))
    for h in range(heads):
        o_ref[:, h * HEAD_DIM:(h + 1) * HEAD_DIM] = acc_ref[h].astype(BF16)


def _cumsum_matrix(ch):
    j = np.arange(ch)[:, None]
    s = np.arange(ch)[None, :]
    m = -(j >= s).astype(np.float32)
    return jnp.asarray(np.concatenate([m, m], axis=0), dtype=BF16)


def _attn_near_kernel(q_ref, kd_ref, kp_ref, vd_ref, vp_ref, m_ref, o_ref, rmax_ref, *, tq, n_heads):
    i = pl.program_id(0)
    m_neg = m_ref[...]
    row = lax.broadcasted_iota(jnp.int32, (tq, tq), 0)
    col = lax.broadcasted_iota(jnp.int32, (tq, tq), 1)
    causal = col < row
    no_prev = jnp.where(i == 0, jnp.float32(-1e30), jnp.float32(0.0))
    rm = None
    for h in range(n_heads):
        hs = slice(h * HEAD_DIM, (h + 1) * HEAD_DIM)
        q = q_ref[:, hs]
        pv0, r = _attn_block(q, kd_ref[:, hs], vd_ref[:, hs], m_neg, None, causal)
        pv1, r = _attn_block(q, kp_ref[:, hs], vp_ref[:, hs], m_neg, r + no_prev, None)
        o_ref[:, hs] = (pv0 + pv1).astype(BF16)
        rm = r if rm is None else jnp.maximum(rm, r)
    rm = jnp.max(rm.reshape(tq // 8, 8, HEAD_DIM), axis=0)
    rmax_ref[0] = jnp.where(i <= 1, jnp.float32(-1e30), rm)


def _attention_near(qkv, n_heads):
    s = qkv.shape[0]
    tq = ATTN_Q
    width = n_heads * HEAD_DIM
    assert s % tq == 0 and qkv.shape[1] == 3 * width
    nq = s // tq
    prev = lambda i: jnp.maximum(i - 1, 0)
    return pl.pallas_call(
        functools.partial(_attn_near_kernel, tq=tq, n_heads=n_heads),
        grid=(nq,),
        in_specs=[
            pl.BlockSpec((tq, width), lambda i: (i, 0)),
            pl.BlockSpec((tq, width), lambda i: (i, 1)),
            pl.BlockSpec((tq, width), lambda i: (prev(i), 1)),
            pl.BlockSpec((tq, width), lambda i: (i, 2)),
            pl.BlockSpec((tq, width), lambda i: (prev(i), 2)),
            pl.BlockSpec((2 * tq, tq), lambda i: (0, 0)),
        ],
        out_specs=[pl.BlockSpec((tq, width), lambda i: (i, 0)),
                   pl.BlockSpec((1, 8, HEAD_DIM), lambda i: (i, 0, 0))],
        out_shape=[jax.ShapeDtypeStruct((s, width), BF16),
                   jax.ShapeDtypeStruct((nq, 8, HEAD_DIM), F32)],
        compiler_params=pltpu.CompilerParams(
            dimension_semantics=("parallel",), vmem_limit_bytes=VMEM_LIMIT),
        name="stickbreak_attn_near",
    )(qkv, qkv, qkv, qkv, qkv, _cumsum_matrix(tq))


def _attention(qkv, n_heads):
    y_near, rmax = _attention_near(qkv, n_heads)
    need_far = jnp.max(rmax) > LOG2_ZERO_BOUND
    return lax.cond(need_far, lambda qkv_, y_: _attention_walk(qkv_, n_heads),
                    lambda qkv_, y_: y_, qkv, y_near)


def _attention_walk(qkv, n_heads):
    s = qkv.shape[0]
    tq = ATTN_Q
    assert s % tq == 0 and s >= 2 * tq and n_heads % ATTN_HEADS == 0
    width = ATTN_HEADS * HEAD_DIM
    groups = n_heads // ATTN_HEADS
    return pl.pallas_call(
        functools.partial(_attn_kernel, tq=tq),
        grid=(groups, s // tq),
        in_specs=[
            pl.BlockSpec((tq, width), lambda g, i: (i, g)),
            pl.BlockSpec((s, width), lambda g, i: (0, groups + g)),
            pl.BlockSpec((s, width), lambda g, i: (0, 2 * groups + g)),
            pl.BlockSpec((2 * tq, tq), lambda g, i: (0, 0)),
        ],
        out_specs=pl.BlockSpec((tq, width), lambda g, i: (i, g)),
        out_shape=jax.ShapeDtypeStruct((s, n_heads * HEAD_DIM), BF16),
        scratch_shapes=[pltpu.VMEM((ATTN_HEADS, tq, HEAD_DIM), F32),
                        pltpu.VMEM((ATTN_HEADS, tq, HEAD_DIM), F32)],
        compiler_params=pltpu.CompilerParams(
            dimension_semantics=("parallel", "arbitrary"), vmem_limit_bytes=VMEM_LIMIT),
        name="stickbreak_attn",
    )(qkv, qkv, qkv, _cumsum_matrix(tq))


def _outproj_kernel(x_ref, ya_ref, ybc_ref, wa_ref, wb_ref, o_ref):
    o_ref[...] = x_ref[...] + _dot(ya_ref[...], wa_ref[...]) + _dot(ybc_ref[...], wb_ref[...])


def _outproj(x, y_a, y_bc, w_out):
    s, d = x.shape
    ka = y_a.shape[1]
    kb = y_bc.shape[1]
    assert ka == kb
    tm = min(PROJ_ROWS, s)
    return pl.pallas_call(
        _outproj_kernel,
        grid=(s // tm,),
        in_specs=[
            pl.BlockSpec((tm, d), lambda i: (i, 0)),
            pl.BlockSpec((tm, ka), lambda i: (i, 0)),
            pl.BlockSpec((tm, kb), lambda i: (i, 0)),
            pl.BlockSpec((ka, d), lambda i: (0, 0)),
            pl.BlockSpec((kb, d), lambda i: (1, 0)),
        ],
        out_specs=pl.BlockSpec((tm, d), lambda i: (i, 0)),
        out_shape=jax.ShapeDtypeStruct((s, d), F32),
        compiler_params=pltpu.CompilerParams(
            dimension_semantics=("parallel",), vmem_limit_bytes=VMEM_LIMIT),
        name="out_proj",
    )(x, y_a, y_bc, w_out, w_out)


def _cast_kernel(x_ref, o_ref):
    o_ref[...] = x_ref[...].astype(BF16)


def _cast_gate_up_kernel(x_ref, o_ref, *, d_ff):
    nj, rows, two_tf = o_ref.shape
    tf = two_tf // 2
    for j in range(nj):
        n = min(tf, d_ff - j * tf)
        o_ref[j, :, :n] = x_ref[:, j * tf:j * tf + n].astype(BF16)
        o_ref[j, :, tf:tf + n] = x_ref[:, d_ff + j * tf:d_ff + j * tf + n].astype(BF16)
        if n < tf:
            zeros = jnp.zeros((rows, tf - n), BF16)
            o_ref[j, :, n:tf] = zeros
            o_ref[j, :, tf + n:] = zeros


def _cast_down_kernel(x_ref, o_ref, *, d_ff):
    o_ref[:d_ff, :] = (x_ref[...] * FFN_RES_WEIGHT).astype(BF16)
    if o_ref.shape[0] > d_ff:
        o_ref[d_ff:, :] = jnp.zeros((o_ref.shape[0] - d_ff, o_ref.shape[1]), BF16)


def _cast_call(body, w, layer, in_block, out_block, out_shape, grid, index, name, out_index=None):
    return pl.pallas_call(
        body,
        grid=(grid,),
        in_specs=[pl.BlockSpec((None,) + in_block, lambda i: (layer,) + index(i))],
        out_specs=pl.BlockSpec(out_block, out_index or index),
        out_shape=jax.ShapeDtypeStruct(out_shape, BF16),
        compiler_params=pltpu.CompilerParams(
            dimension_semantics=("parallel",), vmem_limit_bytes=VMEM_LIMIT),
        name=name,
    )(w)


def _cast_plain(w, layer):
    _, rows, cols = w.shape
    rb = CAST_ROWS
    assert rows % rb == 0
    return _cast_call(_cast_kernel, w, layer, (rb, cols), (rb, cols), (rows, cols), rows // rb,
                      lambda i: (i, 0), "cast_bf16")


def _cast_ffn_weights(w_gu, w_down, layer):
    _, d, two_ff = w_gu.shape
    d_ff = two_ff // 2
    d_ffp = d_ff + (-d_ff) % FFN_COLS
    rb = CAST_ROWS // 2
    nj = d_ffp // FFN_COLS
    assert d % rb == 0 and d % CAST_COLS == 0
    w_gu_p = _cast_call(functools.partial(_cast_gate_up_kernel, d_ff=d_ff), w_gu, layer,
                        (rb, two_ff), (nj, rb, 2 * FFN_COLS), (nj, d, 2 * FFN_COLS), d // rb,
                        lambda i: (i, 0), "cast_gate_up", out_index=lambda i: (0, i, 0))
    w_down_p = _cast_call(functools.partial(_cast_down_kernel, d_ff=d_ff), w_down, layer,
                          (d_ff, CAST_COLS), (d_ffp, CAST_COLS), (d_ffp, d), d // CAST_COLS,
                          lambda i: (0, i), "cast_down")
    return w_gu_p, w_down_p


def kernel(x, ffn1_norm, ffn1_w_gu, ffn1_w_down, mix_norm, w_in, sgu_norm, sgu_w, sgu_b, pool_w,
           pool_scale, w_out, ffn2_norm, ffn2_w_gu, ffn2_w_down, final_norm):
    b, s, d = x.shape
    assert b == 1
    depth = w_in.shape[0]
    attn_width = w_out.shape[1] // 2
    n_heads = attn_width // HEAD_DIM
    width = SGU_GROUPS * GROUP_DIM
    h = x.reshape(s, d)
    fg = final_norm.reshape(1, d)
    for l in range(depth):
        wgu1, wd1 = _cast_ffn_weights(ffn1_w_gu, ffn1_w_down, l)
        wgu2, wd2 = _cast_ffn_weights(ffn2_w_gu, ffn2_w_down, l)
        w_in_l = _cast_plain(w_in, l)
        mix_g = mix_norm[l].reshape(1, d)
        sgu_bias = jnp.repeat(sgu_b[l].T, GROUP_DIM, axis=1)

        h = _ffn(h, ffn1_norm[l].reshape(1, d), wgu1, wd1, fg, final=False)
        qkv, y_bc = _inproj(h, mix_g, w_in_l, attn_width, sgu_norm[l].reshape(1, width),
                            sgu_w[l], sgu_bias, pool_w[l].astype(BF16), pool_scale[l].reshape(1, width))
        y_a = _attention(qkv, n_heads)
        h = _outproj(h, y_a, y_bc, _cast_plain(w_out, l))
        h = _ffn(h, ffn2_norm[l].reshape(1, d), wgu2, wd2, fg, final=(l == depth - 1))
    return h.reshape(b, s, d)
```

```python
import functools
import math

import numpy as np
import jax
import jax.numpy as jnp
from jax import lax
from jax.experimental import pallas as pl
from jax.experimental.pallas import tpu as pltpu

F32 = jnp.float32
BF16 = jnp.bfloat16

HEAD_DIM = 128
SGU_GROUPS = 4
SGU_CHUNK = 128
POOL_WINDOWS = (2, 4, 8, 16)
GROUP_DIM = 128
EPS = 1e-6
FFN_RES_WEIGHT = 0.5
assert math.frexp(FFN_RES_WEIGHT)[0] == 0.5

V7X_VMEM_BYTES = 64 * 1024 * 1024
VMEM_LIMIT = V7X_VMEM_BYTES - 6 * 1024 * 1024

FFN_ROWS = 1024
FFN_COLS = 512
PROJ_ROWS = 512
QKV_COLS = 1024
CAST_ROWS = 512
CAST_COLS = 512
ATTN_Q = 256
ATTN_HEADS = 2
LOG2_E = 1.4426950408889634
LOG2_ZERO_BOUND = -151.0


def _rmsnorm(x, g):
    inv = lax.rsqrt(jnp.mean(x * x, axis=-1, keepdims=True) + EPS)
    return x * inv * g


def _dot(a, b):
    return jnp.dot(a, b, preferred_element_type=F32)


def _ffn_kernel(x_ref, g_ref, wgu_ref, wd_ref, fg_ref, o_ref, xn_ref, *, nj, final):
    j = pl.program_id(1)
    tf = wd_ref.shape[0]

    @pl.when(j == 0)
    def _():
        xn_ref[...] = _rmsnorm(x_ref[...], g_ref[...]).astype(BF16)

    def down_proj():
        gu = _dot(xn_ref[...], wgu_ref[...])
        gate = gu[:, :tf]
        h = (gate * jax.nn.sigmoid(gate) * gu[:, tf:]).astype(BF16)
        return _dot(h, wd_ref[...])

    @pl.when(j == 0)
    def _():
        o_ref[...] = x_ref[...] + down_proj()

    @pl.when(j > 0)
    def _():
        o_ref[...] += down_proj()

    if final:
        @pl.when(j == nj - 1)
        def _():
            o_ref[...] = _rmsnorm(o_ref[...], fg_ref[...])


def _ffn(x, g, w_gu_p, w_down_p, fg, *, final):
    s, d = x.shape
    nj = w_gu_p.shape[0]
    assert w_gu_p.shape == (nj, d, 2 * FFN_COLS) and w_down_p.shape == (nj * FFN_COLS, d)
    tm = min(FFN_ROWS, s)
    ni = s // tm
    x_index = lambda i, j: (jnp.minimum(i + jnp.minimum(j, 1), ni - 1), 0)
    return pl.pallas_call(
        functools.partial(_ffn_kernel, nj=nj, final=final),
        grid=(ni, nj),
        in_specs=[
            pl.BlockSpec((tm, d), x_index),
            pl.BlockSpec((1, d), lambda i, j: (0, 0)),
            pl.BlockSpec((None, d, 2 * FFN_COLS), lambda i, j: (j, 0, 0)),
            pl.BlockSpec((FFN_COLS, d), lambda i, j: (j, 0)),
            pl.BlockSpec((1, d), lambda i, j: (0, 0)),
        ],
        out_specs=pl.BlockSpec((tm, d), lambda i, j: (i, 0)),
        out_shape=jax.ShapeDtypeStruct((s, d), F32),
        scratch_shapes=[pltpu.VMEM((tm, d), BF16)],
        compiler_params=pltpu.CompilerParams(
            dimension_semantics=("arbitrary", "arbitrary"), vmem_limit_bytes=VMEM_LIMIT),
        name="ffn",
    )(x, g, w_gu_p, w_down_p, fg)


def _inproj_kernel(x_ref, g_ref, w_ref, sn_ref, sw_ref, sb_ref, band_ref, pw_ref, ps_ref,
                   qkv_ref, o_ref, ph_ref, plo_ref, *, tm, attn_width, scale):
    i = pl.program_id(0)
    ch = SGU_CHUNK
    width = SGU_GROUPS * GROUP_DIM
    xn = _rmsnorm(x_ref[...], g_ref[...]).astype(BF16)
    proj = _dot(xn, w_ref[:, 3 * attn_width:])
    u = jax.nn.gelu(proj[:, :width])
    v = jax.nn.gelu(proj[:, width:2 * width])
    p = proj[:, 2 * width:]

    @pl.when(i == 0)
    def _():
        ph_ref[0:ch, :] = jnp.zeros((ch, width), BF16)
        plo_ref[0:ch, :] = jnp.zeros((ch, width), BF16)

    p_hi = p.astype(BF16)
    ph_ref[ch:ch + tm, :] = p_hi
    plo_ref[ch:ch + tm, :] = (p - p_hi.astype(F32)).astype(BF16)

    row = lax.broadcasted_iota(jnp.int32, (ch, ch), 0)
    col = lax.broadcasted_iota(jnp.int32, (ch, ch), 1)
    tril = col <= row
    units = [(grp, c) for grp in range(SGU_GROUPS) for c in range(tm // ch)]
    lanes = lambda grp: slice(grp * GROUP_DIM, (grp + 1) * GROUP_DIM)
    rows = lambda c: slice(c * ch, (c + 1) * ch)
    vg = [_rmsnorm(v[:, lanes(grp)], sn_ref[:, lanes(grp)]).astype(BF16) for grp in range(SGU_GROUPS)]
    w_causal = [jnp.where(tril, sw_ref[grp], 0.0).astype(BF16) for grp in range(SGU_GROUPS)]
    mixed = {(grp, c): _dot(w_causal[grp], vg[grp][rows(c)]) for grp, c in units}
    wsum = {(grp, c): _dot(band_ref[grp], ph_ref[c * ch:(c + 2) * ch, lanes(grp)])
            + _dot(band_ref[grp], plo_ref[c * ch:(c + 2) * ch, lanes(grp)]) for grp, c in units}
    for grp, c in units:
        o_ref[rows(c), lanes(grp)] = (u[rows(c), lanes(grp)] * (mixed[grp, c] + sb_ref[:, lanes(grp)])).astype(BF16)
    dlt = {}
    for grp, c in units:
        pos = i * tm + c * ch + row
        count = jnp.minimum(pos + 1, POOL_WINDOWS[grp]).astype(F32)
        dlt[grp, c] = (wsum[grp, c] / count - p[rows(c), lanes(grp)]).astype(BF16)
    for grp, c in units:
        y = _dot(dlt[grp, c], pw_ref[grp]) * ps_ref[:, lanes(grp)]
        o_ref[rows(c), width + grp * GROUP_DIM:width + (grp + 1) * GROUP_DIM] = y.astype(BF16)

    ph_ref[0:ch, :] = ph_ref[tm:tm + ch, :]
    plo_ref[0:ch, :] = plo_ref[tm:tm + ch, :]

    for c in range(3 * attn_width // QKV_COLS):
        cols = slice(c * QKV_COLS, (c + 1) * QKV_COLS)
        acc = _dot(xn, w_ref[:, cols])
        if (c + 1) * QKV_COLS <= attn_width:
            acc = acc * scale
        qkv_ref[:, cols] = acc.astype(BF16)


def _pool_bands():
    t = np.arange(SGU_CHUNK)[:, None] + SGU_CHUNK
    j = np.arange(2 * SGU_CHUNK)[None, :]
    bands = [((t - j >= 0) & (t - j < w)).astype(np.float32) for w in POOL_WINDOWS]
    return jnp.asarray(np.stack(bands), dtype=BF16)


def _inproj(x, g, w_in, attn_width, sgu_norm, sgu_w, sgu_bias, pool_w, pool_scale):
    s, d = x.shape
    width = SGU_GROUPS * GROUP_DIM
    n_in = w_in.shape[1]
    assert n_in == 3 * attn_width + 3 * width and attn_width % QKV_COLS == 0
    tm = min(PROJ_ROWS, s)
    full = lambda shape: pl.BlockSpec(shape, lambda i: (0,) * len(shape))
    return pl.pallas_call(
        functools.partial(_inproj_kernel, tm=tm, attn_width=attn_width,
                          scale=LOG2_E / math.sqrt(HEAD_DIM)),
        grid=(s // tm,),
        in_specs=[
            pl.BlockSpec((tm, d), lambda i: (i, 0)),
            full((1, d)),
            pl.BlockSpec((d, n_in), lambda i: (0, 0), pipeline_mode=pl.Buffered(1)),
            full((1, width)),
            full((SGU_GROUPS, SGU_CHUNK, SGU_CHUNK)),
            full((SGU_CHUNK, width)),
            full((SGU_GROUPS, SGU_CHUNK, 2 * SGU_CHUNK)),
            full((SGU_GROUPS, GROUP_DIM, GROUP_DIM)),
            full((1, width)),
        ],
        out_specs=[pl.BlockSpec((tm, 3 * attn_width), lambda i: (i, 0)),
                   pl.BlockSpec((tm, 2 * width), lambda i: (i, 0))],
        out_shape=[jax.ShapeDtypeStruct((s, 3 * attn_width), BF16),
                   jax.ShapeDtypeStruct((s, 2 * width), BF16)],
        scratch_shapes=[pltpu.VMEM((tm + SGU_CHUNK, width), BF16),
                        pltpu.VMEM((tm + SGU_CHUNK, width), BF16)],
        compiler_params=pltpu.CompilerParams(
            dimension_semantics=("arbitrary",), vmem_limit_bytes=VMEM_LIMIT),
        name="in_proj",
    )(x, g, w_in, sgu_norm, sgu_w, sgu_bias, _pool_bands(), pool_w, pool_scale)


def _attn_block(q, ks, vs, m_neg, r, causal):
    z2 = _attn_scores(q, ks)
    hilo, total = _attn_softplus(z2, causal)
    a = _attn_weights(z2, _dot(hilo, m_neg), r, causal)
    return _dot(a, vs), (total if r is None else r + total)


def _attn_scores(q, ks):
    return lax.dot_general(q, ks, (((1,), (1,)), ((), ())), preferred_element_type=F32)


def _attn_softplus(z2, causal):
    neg_abs = pltpu.bitcast(pltpu.bitcast(z2, jnp.uint32) | jnp.uint32(0x80000000), F32)
    sp = jnp.maximum(z2, 0.0) + jnp.log(1.0 + jnp.exp2(neg_abs)) * LOG2_E
    if causal is not None:
        sp = jnp.where(causal, sp, 0.0)
    hi = pltpu.bitcast(pltpu.bitcast(sp, jnp.uint32) & jnp.uint32(0xFFFF0000), F32)
    lo = sp - hi
    total = jnp.broadcast_to(-jnp.sum(sp, axis=1, keepdims=True), (z2.shape[0], HEAD_DIM))
    return jnp.concatenate([hi.astype(BF16), lo.astype(BF16)], axis=1), total


def _attn_weights(z2, from_here, r, causal):
    log_a = z2 + from_here
    if r is not None:
        log_a = log_a + jnp.concatenate([r] * (z2.shape[1] // HEAD_DIM), axis=1)
    a = jnp.exp2(log_a)
    if causal is not None:
        a = jnp.where(causal, a, 0.0)
    return a.astype(BF16)


def _attn_kernel(q_ref, k_ref, v_ref, m_ref, o_ref, r_ref, acc_ref, *, tq):
    i = pl.program_id(1)
    m_neg = m_ref[...]
    heads = r_ref.shape[0]
    row = lax.broadcasted_iota(jnp.int32, (tq, tq), 0)
    col = lax.broadcasted_iota(jnp.int32, (tq, tq), 1)
    causal = col < row
    diag = pl.multiple_of(i * tq, tq)
    prev = pl.multiple_of(jnp.maximum(i - 1, 0) * tq, tq)
    no_prev = jnp.where(i == 0, jnp.float32(-1e30), jnp.float32(0.0))

    rmax0 = None
    for h in range(heads):
        hs = slice(h * HEAD_DIM, (h + 1) * HEAD_DIM)
        q = q_ref[:, hs]
        pv0, r = _attn_block(q, k_ref[pl.ds(diag, tq), hs], v_ref[pl.ds(diag, tq), hs], m_neg, None, causal)
        pv1, r = _attn_block(q, k_ref[pl.ds(prev, tq), hs], v_ref[pl.ds(prev, tq), hs], m_neg,
                             r + no_prev, None)
        acc_ref[h] = pv0 + pv1
        r_ref[h] = r
        hmax = jnp.max(r)
        rmax0 = hmax if rmax0 is None else jnp.maximum(rmax0, hmax)

    def cond(carry):
        j, rmax = carry
        return jnp.logical_and(j >= 0, rmax > LOG2_ZERO_BOUND)

    def body(carry):
        j, _ = carry
        start = pl.multiple_of(j * tq, tq)
        rmax = None
        for h in range(heads):
            hs = slice(h * HEAD_DIM, (h + 1) * HEAD_DIM)
            pv, r = _attn_block(q_ref[:, hs], k_ref[pl.ds(start, tq), hs], v_ref[pl.ds(start, tq), hs],
                                m_neg, r_ref[h], None)
            acc_ref[h] += pv
            r_ref[h] = r
            hmax = jnp.max(r)
            rmax = hmax if rmax is None else jnp.maximum(rmax, hmax)
        return j - 1, rmax

    lax.while_loop(cond, body, (i - 2, rmax0))
    for h in range(heads):
        o_ref[:, h * HEAD_DIM:(h + 1) * HEAD_DIM] = acc_ref[h].astype(BF16)


def _cumsum_matrix(ch):
    j = np.arange(ch)[:, None]
    s = np.arange(ch)[None, :]
    m = -(j >= s).astype(np.float32)
    return jnp.asarray(np.concatenate([m, m], axis=0), dtype=BF16)


def _attn_near_kernel(q_ref, kd_ref, kp_ref, vd_ref, vp_ref, m_ref, o_ref, rmax_ref, *, tq, n_heads):
    i = pl.program_id(0)
    m_neg = m_ref[...]
    row = lax.broadcasted_iota(jnp.int32, (tq, tq), 0)
    col = lax.broadcasted_iota(jnp.int32, (tq, tq), 1)
    causal = col < row
    no_prev = jnp.where(i == 0, jnp.float32(-1e30), jnp.float32(0.0))
    heads = [slice(h * HEAD_DIM, (h + 1) * HEAD_DIM) for h in range(n_heads)]
    z_d, z_p, sp_d, sp_p, from_d, from_p, a_d, a_p = ({} for _ in range(8))
    rmax = []

    def stage(s, h):
        hs = heads[h]
        if s == 0:
            z_d[h] = _attn_scores(q_ref[:, hs], kd_ref[:, hs])
            z_p[h] = _attn_scores(q_ref[:, hs], kp_ref[:, hs])
        elif s == 1:
            sp_d[h] = _attn_softplus(z_d[h], causal)
            sp_p[h] = _attn_softplus(z_p[h], None)
        elif s == 2:
            from_d[h] = _dot(sp_d[h][0], m_neg)
            from_p[h] = _dot(sp_p[h][0], m_neg)
        elif s == 3:
            a_d[h] = _attn_weights(z_d[h], from_d[h], None, causal)
            a_p[h] = _attn_weights(z_p[h], from_p[h], sp_d[h][1] + no_prev, None)
        else:
            o_ref[:, hs] = (_dot(a_d[h], vd_ref[:, hs]) + _dot(a_p[h], vp_ref[:, hs])).astype(BF16)
            rmax.append(sp_d[h][1] + no_prev + sp_p[h][1])

    for t in range(n_heads + 4):
        for h in range(n_heads):
            if 0 <= t - h < 5:
                stage(t - h, h)
    rm = functools.reduce(jnp.maximum, rmax)
    rm = jnp.max(rm.reshape(tq // 8, 8, HEAD_DIM), axis=0)
    rmax_ref[0] = jnp.where(i <= 1, jnp.float32(-1e30), rm)


def _attention_near(qkv, n_heads):
    s = qkv.shape[0]
    tq = ATTN_Q
    width = n_heads * HEAD_DIM
    assert s % tq == 0 and qkv.shape[1] == 3 * width
    nq = s // tq
    prev = lambda i: jnp.maximum(i - 1, 0)
    return pl.pallas_call(
        functools.partial(_attn_near_kernel, tq=tq, n_heads=n_heads),
        grid=(nq,),
        in_specs=[
            pl.BlockSpec((tq, width), lambda i: (i, 0)),
            pl.BlockSpec((tq, width), lambda i: (i, 1)),
            pl.BlockSpec((tq, width), lambda i: (prev(i), 1)),
            pl.BlockSpec((tq, width), lambda i: (i, 2)),
            pl.BlockSpec((tq, width), lambda i: (prev(i), 2)),
            pl.BlockSpec((2 * tq, tq), lambda i: (0, 0)),
        ],
        out_specs=[pl.BlockSpec((tq, width), lambda i: (i, 0)),
                   pl.BlockSpec((1, 8, HEAD_DIM), lambda i: (i, 0, 0))],
        out_shape=[jax.ShapeDtypeStruct((s, width), BF16),
                   jax.ShapeDtypeStruct((nq, 8, HEAD_DIM), F32)],
        compiler_params=pltpu.CompilerParams(
            dimension_semantics=("parallel",), vmem_limit_bytes=VMEM_LIMIT),
        name="stickbreak_attn_near",
    )(qkv, qkv, qkv, qkv, qkv, _cumsum_matrix(tq))


def _attention(qkv, n_heads):
    y_near, rmax = _attention_near(qkv, n_heads)
    need_far = jnp.max(rmax) > LOG2_ZERO_BOUND
    return lax.cond(need_far, lambda qkv_, y_: _attention_walk(qkv_, n_heads),
                    lambda qkv_, y_: y_, qkv, y_near)


def _attention_walk(qkv, n_heads):
    s = qkv.shape[0]
    tq = ATTN_Q
    assert s % tq == 0 and s >= 2 * tq and n_heads % ATTN_HEADS == 0
    width = ATTN_HEADS * HEAD_DIM
    groups = n_heads // ATTN_HEADS
    return pl.pallas_call(
        functools.partial(_attn_kernel, tq=tq),
        grid=(groups, s // tq),
        in_specs=[
            pl.BlockSpec((tq, width), lambda g, i: (i, g)),
            pl.BlockSpec((s, width), lambda g, i: (0, groups + g)),
            pl.BlockSpec((s, width), lambda g, i: (0, 2 * groups + g)),
            pl.BlockSpec((2 * tq, tq), lambda g, i: (0, 0)),
        ],
        out_specs=pl.BlockSpec((tq, width), lambda g, i: (i, g)),
        out_shape=jax.ShapeDtypeStruct((s, n_heads * HEAD_DIM), BF16),
        scratch_shapes=[pltpu.VMEM((ATTN_HEADS, tq, HEAD_DIM), F32),
                        pltpu.VMEM((ATTN_HEADS, tq, HEAD_DIM), F32)],
        compiler_params=pltpu.CompilerParams(
            dimension_semantics=("parallel", "arbitrary"), vmem_limit_bytes=VMEM_LIMIT),
        name="stickbreak_attn",
    )(qkv, qkv, qkv, _cumsum_matrix(tq))


def _outproj_kernel(x_ref, ya_ref, ybc_ref, wa_ref, wb_ref, o_ref):
    o_ref[...] = x_ref[...] + _dot(ya_ref[...], wa_ref[...]) + _dot(ybc_ref[...], wb_ref[...])


def _outproj(x, y_a, y_bc, w_out):
    s, d = x.shape
    ka = y_a.shape[1]
    kb = y_bc.shape[1]
    assert ka == kb
    tm = min(PROJ_ROWS, s)
    return pl.pallas_call(
        _outproj_kernel,
        grid=(s // tm,),
        in_specs=[
            pl.BlockSpec((tm, d), lambda i: (i, 0)),
            pl.BlockSpec((tm, ka), lambda i: (i, 0)),
            pl.BlockSpec((tm, kb), lambda i: (i, 0)),
            pl.BlockSpec((ka, d), lambda i: (0, 0)),
            pl.BlockSpec((kb, d), lambda i: (1, 0)),
        ],
        out_specs=pl.BlockSpec((tm, d), lambda i: (i, 0)),
        out_shape=jax.ShapeDtypeStruct((s, d), F32),
        compiler_params=pltpu.CompilerParams(
            dimension_semantics=("parallel",), vmem_limit_bytes=VMEM_LIMIT),
        name="out_proj",
    )(x, y_a, y_bc, w_out, w_out)


def _cast_kernel(x_ref, o_ref):
    o_ref[...] = x_ref[...].astype(BF16)


def _cast_gate_up_kernel(x_ref, o_ref, *, d_ff):
    nj, rows, two_tf = o_ref.shape
    tf = two_tf // 2
    for j in range(nj):
        n = min(tf, d_ff - j * tf)
        o_ref[j, :, :n] = x_ref[:, j * tf:j * tf + n].astype(BF16)
        o_ref[j, :, tf:tf + n] = x_ref[:, d_ff + j * tf:d_ff + j * tf + n].astype(BF16)
        if n < tf:
            zeros = jnp.zeros((rows, tf - n), BF16)
            o_ref[j, :, n:tf] = zeros
            o_ref[j, :, tf + n:] = zeros


def _cast_down_kernel(x_ref, o_ref, *, d_ff):
    o_ref[:d_ff, :] = (x_ref[...] * FFN_RES_WEIGHT).astype(BF16)
    if o_ref.shape[0] > d_ff:
        o_ref[d_ff:, :] = jnp.zeros((o_ref.shape[0] - d_ff, o_ref.shape[1]), BF16)


def _cast_call(body, w, layer, in_block, out_block, out_shape, grid, index, name, out_index=None):
    return pl.pallas_call(
        body,
        grid=(grid,),
        in_specs=[pl.BlockSpec((None,) + in_block, lambda i: (layer,) + index(i))],
        out_specs=pl.BlockSpec(out_block, out_index or index),
        out_shape=jax.ShapeDtypeStruct(out_shape, BF16),
        compiler_params=pltpu.CompilerParams(
            dimension_semantics=("parallel",), vmem_limit_bytes=VMEM_LIMIT),
        name=name,
    )(w)


def _cast_plain(w, layer):
    _, rows, cols = w.shape
    rb = CAST_ROWS
    assert rows % rb == 0
    return _cast_call(_cast_kernel, w, layer, (rb, cols), (rb, cols), (rows, cols), rows // rb,
                      lambda i: (i, 0), "cast_bf16")


def _cast_ffn_weights(w_gu, w_down, layer):
    _, d, two_ff = w_gu.shape
    d_ff = two_ff // 2
    d_ffp = d_ff + (-d_ff) % FFN_COLS
    rb = CAST_ROWS // 2
    nj = d_ffp // FFN_COLS
    assert d % rb == 0 and d % CAST_COLS == 0
    w_gu_p = _cast_call(functools.partial(_cast_gate_up_kernel, d_ff=d_ff), w_gu, layer,
                        (rb, two_ff), (nj, rb, 2 * FFN_COLS), (nj, d, 2 * FFN_COLS), d // rb,
                        lambda i: (i, 0), "cast_gate_up", out_index=lambda i: (0, i, 0))
    w_down_p = _cast_call(functools.partial(_cast_down_kernel, d_ff=d_ff), w_down, layer,
                          (d_ff, CAST_COLS), (d_ffp, CAST_COLS), (d_ffp, d), d // CAST_COLS,
                          lambda i: (0, i), "cast_down")
    return w_gu_p, w_down_p


def kernel(x, ffn1_norm, ffn1_w_gu, ffn1_w_down, mix_norm, w_in, sgu_norm, sgu_w, sgu_b, pool_w,
           pool_scale, w_out, ffn2_norm, ffn2_w_gu, ffn2_w_down, final_norm):
    b, s, d = x.shape
    assert b == 1
    depth = w_in.shape[0]
    attn_width = w_out.shape[1] // 2
    n_heads = attn_width // HEAD_DIM
    width = SGU_GROUPS * GROUP_DIM
    h = x.reshape(s, d)
    fg = final_norm.reshape(1, d)
    for l in range(depth):
        wgu1, wd1 = _cast_ffn_weights(ffn1_w_gu, ffn1_w_down, l)
        wgu2, wd2 = _cast_ffn_weights(ffn2_w_gu, ffn2_w_down, l)
        w_in_l = _cast_plain(w_in, l)
        mix_g = mix_norm[l].reshape(1, d)
        sgu_bias = jnp.repeat(sgu_b[l].T, GROUP_DIM, axis=1)

        h = _ffn(h, ffn1_norm[l].reshape(1, d), wgu1, wd1, fg, final=False)
        qkv, y_bc = _inproj(h, mix_g, w_in_l, attn_width, sgu_norm[l].reshape(1, width),
                            sgu_w[l], sgu_bias, pool_w[l].astype(BF16), pool_scale[l].reshape(1, width))
        y_a = _attention(qkv, n_heads)
        h = _outproj(h, y_a, y_bc, _cast_plain(w_out, l))
        h = _ffn(h, ffn2_norm[l].reshape(1, d), wgu2, wd2, fg, final=(l == depth - 1))
    return h.reshape(b, s, d)
```

```python
import functools
import math

import numpy as np
import jax
import jax.numpy as jnp
from jax import lax
from jax.experimental import pallas as pl
from jax.experimental.pallas import tpu as pltpu

F32 = jnp.float32
BF16 = jnp.bfloat16

HEAD_DIM = 128
SGU_GROUPS = 4
SGU_CHUNK = 128
POOL_WINDOWS = (2, 4, 8, 16)
GROUP_DIM = 128
EPS = 1e-6
FFN_RES_WEIGHT = 0.5
assert math.frexp(FFN_RES_WEIGHT)[0] == 0.5

V7X_VMEM_BYTES = 64 * 1024 * 1024
VMEM_LIMIT = V7X_VMEM_BYTES - 6 * 1024 * 1024

FFN_ROWS = 1024
FFN_COLS = 512
PROJ_ROWS = 512
QKV_COLS = 1024
CAST_ROWS = 512
CAST_COLS = 512
ATTN_Q = 256
ATTN_HEADS = 2
LOG2_E = 1.4426950408889634
LOG2_ZERO_BOUND = -151.0


def _rmsnorm(x, g):
    inv = lax.rsqrt(jnp.mean(x * x, axis=-1, keepdims=True) + EPS)
    return x * inv * g


def _dot(a, b):
    return jnp.dot(a, b, preferred_element_type=F32)


def _ffn_kernel(x_ref, g_ref, wgu_ref, wd_ref, fg_ref, o_ref, xn_ref, *, nj, final):
    j = pl.program_id(1)
    tf = wd_ref.shape[0]

    @pl.when(j == 0)
    def _():
        xn_ref[...] = _rmsnorm(x_ref[...], g_ref[...]).astype(BF16)

    def down_proj():
        gu = _dot(xn_ref[...], wgu_ref[...])
        gate = gu[:, :tf]
        h = (gate * jax.nn.sigmoid(gate) * gu[:, tf:]).astype(BF16)
        return _dot(h, wd_ref[...])

    @pl.when(j == 0)
    def _():
        o_ref[...] = x_ref[...] + down_proj()

    @pl.when(j > 0)
    def _():
        o_ref[...] += down_proj()

    if final:
        @pl.when(j == nj - 1)
        def _():
            o_ref[...] = _rmsnorm(o_ref[...], fg_ref[...])


def _ffn(x, g, w_gu_p, w_down_p, fg, *, final):
    s, d = x.shape
    nj = w_gu_p.shape[0]
    assert w_gu_p.shape == (nj, d, 2 * FFN_COLS) and w_down_p.shape == (nj * FFN_COLS, d)
    tm = min(FFN_ROWS, s)
    ni = s // tm
    x_index = lambda i, j: (jnp.minimum(i + jnp.minimum(j, 1), ni - 1), 0)
    return pl.pallas_call(
        functools.partial(_ffn_kernel, nj=nj, final=final),
        grid=(ni, nj),
        in_specs=[
            pl.BlockSpec((tm, d), x_index),
            pl.BlockSpec((1, d), lambda i, j: (0, 0)),
            pl.BlockSpec((None, d, 2 * FFN_COLS), lambda i, j: (j, 0, 0)),
            pl.BlockSpec((FFN_COLS, d), lambda i, j: (j, 0)),
            pl.BlockSpec((1, d), lambda i, j: (0, 0)),
        ],
        out_specs=pl.BlockSpec((tm, d), lambda i, j: (i, 0)),
        out_shape=jax.ShapeDtypeStruct((s, d), F32),
        scratch_shapes=[pltpu.VMEM((tm, d), BF16)],
        compiler_params=pltpu.CompilerParams(
            dimension_semantics=("arbitrary", "arbitrary"), vmem_limit_bytes=VMEM_LIMIT),
        name="ffn",
    )(x, g, w_gu_p, w_down_p, fg)


def _inproj_kernel(x_ref, g_ref, w_ref, sn_ref, sw_ref, sb_ref, band_ref, pw_ref, ps_ref,
                   qkv_ref, o_ref, ph_ref, plo_ref, *, tm, attn_width, scale):
    i = pl.program_id(0)
    ch = SGU_CHUNK
    width = SGU_GROUPS * GROUP_DIM
    xn = _rmsnorm(x_ref[...], g_ref[...]).astype(BF16)
    proj = _dot(xn, w_ref[:, 3 * attn_width:])
    u = jax.nn.gelu(proj[:, :width])
    v = jax.nn.gelu(proj[:, width:2 * width])
    p = proj[:, 2 * width:]

    @pl.when(i == 0)
    def _():
        ph_ref[0:ch, :] = jnp.zeros((ch, width), BF16)
        plo_ref[0:ch, :] = jnp.zeros((ch, width), BF16)

    p_hi = p.astype(BF16)
    ph_ref[ch:ch + tm, :] = p_hi
    plo_ref[ch:ch + tm, :] = (p - p_hi.astype(F32)).astype(BF16)

    row = lax.broadcasted_iota(jnp.int32, (ch, ch), 0)
    col = lax.broadcasted_iota(jnp.int32, (ch, ch), 1)
    tril = col <= row
    units = [(grp, c) for grp in range(SGU_GROUPS) for c in range(tm // ch)]
    lanes = lambda grp: slice(grp * GROUP_DIM, (grp + 1) * GROUP_DIM)
    rows = lambda c: slice(c * ch, (c + 1) * ch)
    vg = [_rmsnorm(v[:, lanes(grp)], sn_ref[:, lanes(grp)]).astype(BF16) for grp in range(SGU_GROUPS)]
    w_causal = [jnp.where(tril, sw_ref[grp], 0.0).astype(BF16) for grp in range(SGU_GROUPS)]
    mixed = {(grp, c): _dot(w_causal[grp], vg[grp][rows(c)]) for grp, c in units}
    wsum = {(grp, c): _dot(band_ref[grp], ph_ref[c * ch:(c + 2) * ch, lanes(grp)])
            + _dot(band_ref[grp], plo_ref[c * ch:(c + 2) * ch, lanes(grp)]) for grp, c in units}
    for grp, c in units:
        o_ref[rows(c), lanes(grp)] = (u[rows(c), lanes(grp)] * (mixed[grp, c] + sb_ref[:, lanes(grp)])).astype(BF16)
    dlt = {}
    for grp, c in units:
        pos = i * tm + c * ch + row
        count = jnp.minimum(pos + 1, POOL_WINDOWS[grp]).astype(F32)
        dlt[grp, c] = (wsum[grp, c] / count - p[rows(c), lanes(grp)]).astype(BF16)
    for grp, c in units:
        y = _dot(dlt[grp, c], pw_ref[grp]) * ps_ref[:, lanes(grp)]
        o_ref[rows(c), width + grp * GROUP_DIM:width + (grp + 1) * GROUP_DIM] = y.astype(BF16)

    ph_ref[0:ch, :] = ph_ref[tm:tm + ch, :]
    plo_ref[0:ch, :] = plo_ref[tm:tm + ch, :]

    for c in range(3 * attn_width // QKV_COLS):
        cols = slice(c * QKV_COLS, (c + 1) * QKV_COLS)
        acc = _dot(xn, w_ref[:, cols])
        if (c + 1) * QKV_COLS <= attn_width:
            acc = acc * scale
        qkv_ref[:, cols] = acc.astype(BF16)


def _pool_bands():
    t = np.arange(SGU_CHUNK)[:, None] + SGU_CHUNK
    j = np.arange(2 * SGU_CHUNK)[None, :]
    bands = [((t - j >= 0) & (t - j < w)).astype(np.float32) for w in POOL_WINDOWS]
    return jnp.asarray(np.stack(bands), dtype=BF16)


def _inproj(x, g, w_in, attn_width, sgu_norm, sgu_w, sgu_bias, pool_w, pool_scale):
    s, d = x.shape
    width = SGU_GROUPS * GROUP_DIM
    n_in = w_in.shape[1]
    assert n_in == 3 * attn_width + 3 * width and attn_width % QKV_COLS == 0
    tm = min(PROJ_ROWS, s)
    full = lambda shape: pl.BlockSpec(shape, lambda i: (0,) * len(shape))
    return pl.pallas_call(
        functools.partial(_inproj_kernel, tm=tm, attn_width=attn_width,
                          scale=LOG2_E / math.sqrt(HEAD_DIM)),
        grid=(s // tm,),
        in_specs=[
            pl.BlockSpec((tm, d), lambda i: (i, 0)),
            full((1, d)),
            pl.BlockSpec((d, n_in), lambda i: (0, 0), pipeline_mode=pl.Buffered(1)),
            full((1, width)),
            full((SGU_GROUPS, SGU_CHUNK, SGU_CHUNK)),
            full((SGU_CHUNK, width)),
            full((SGU_GROUPS, SGU_CHUNK, 2 * SGU_CHUNK)),
            full((SGU_GROUPS, GROUP_DIM, GROUP_DIM)),
            full((1, width)),
        ],
        out_specs=[pl.BlockSpec((tm, 3 * attn_width), lambda i: (i, 0)),
                   pl.BlockSpec((tm, 2 * width), lambda i: (i, 0))],
        out_shape=[jax.ShapeDtypeStruct((s, 3 * attn_width), BF16),
                   jax.ShapeDtypeStruct((s, 2 * width), BF16)],
        scratch_shapes=[pltpu.VMEM((tm + SGU_CHUNK, width), BF16),
                        pltpu.VMEM((tm + SGU_CHUNK, width), BF16)],
        compiler_params=pltpu.CompilerParams(
            dimension_semantics=("arbitrary",), vmem_limit_bytes=VMEM_LIMIT),
        name="in_proj",
    )(x, g, w_in, sgu_norm, sgu_w, sgu_bias, _pool_bands(), pool_w, pool_scale)


def _attn_block(q, ks, vs, m_neg, r, causal):
    z2 = _attn_scores(q, ks)
    hilo, total = _attn_softplus(z2, causal)
    a = _attn_weights(z2, _dot(hilo, m_neg), r, causal)
    return _dot(a, vs), (total if r is None else r + total)


def _attn_scores(q, ks):
    return lax.dot_general(q, ks, (((1,), (1,)), ((), ())), preferred_element_type=F32)


def _attn_softplus(z2, causal):
    neg_abs = pltpu.bitcast(pltpu.bitcast(z2, jnp.uint32) | jnp.uint32(0x80000000), F32)
    sp = jnp.maximum(z2, 0.0) + jnp.log(1.0 + jnp.exp2(neg_abs)) * LOG2_E
    if causal is not None:
        sp = jnp.where(causal, sp, 0.0)
    hi = pltpu.bitcast(pltpu.bitcast(sp, jnp.uint32) & jnp.uint32(0xFFFF0000), F32)
    lo = sp - hi
    total = jnp.broadcast_to(-jnp.sum(sp, axis=1, keepdims=True), (z2.shape[0], HEAD_DIM))
    return jnp.concatenate([hi.astype(BF16), lo.astype(BF16)], axis=1), total


def _attn_weights(z2, from_here, r, causal):
    log_a = z2 + from_here
    if r is not None:
        log_a = log_a + jnp.concatenate([r] * (z2.shape[1] // HEAD_DIM), axis=1)
    a = jnp.exp2(log_a)
    if causal is not None:
        a = jnp.where(causal, a, 0.0)
    return a.astype(BF16)


def _attn_kernel(q_ref, k_ref, v_ref, m_ref, o_ref, r_ref, acc_ref, *, tq):
    i = pl.program_id(1)
    m_neg = m_ref[...]
    heads = r_ref.shape[0]
    row = lax.broadcasted_iota(jnp.int32, (tq, tq), 0)
    col = lax.broadcasted_iota(jnp.int32, (tq, tq), 1)
    causal = col < row
    diag = pl.multiple_of(i * tq, tq)
    prev = pl.multiple_of(jnp.maximum(i - 1, 0) * tq, tq)
    no_prev = jnp.where(i == 0, jnp.float32(-1e30), jnp.float32(0.0))

    rmax0 = None
    for h in range(heads):
        hs = slice(h * HEAD_DIM, (h + 1) * HEAD_DIM)
        q = q_ref[:, hs]
        pv0, r = _attn_block(q, k_ref[pl.ds(diag, tq), hs], v_ref[pl.ds(diag, tq), hs], m_neg, None, causal)
        pv1, r = _attn_block(q, k_ref[pl.ds(prev, tq), hs], v_ref[pl.ds(prev, tq), hs], m_neg,
                             r + no_prev, None)
        acc_ref[h] = pv0 + pv1
        r_ref[h] = r
        hmax = jnp.max(r)
        rmax0 = hmax if rmax0 is None else jnp.maximum(rmax0, hmax)

    def cond(carry):
        j, rmax = carry
        return jnp.logical_and(j >= 0, rmax > LOG2_ZERO_BOUND)

    def body(carry):
        j, _ = carry
        start = pl.multiple_of(j * tq, tq)
        rmax = None
        for h in range(heads):
            hs = slice(h * HEAD_DIM, (h + 1) * HEAD_DIM)
            pv, r = _attn_block(q_ref[:, hs], k_ref[pl.ds(start, tq), hs], v_ref[pl.ds(start, tq), hs],
                                m_neg, r_ref[h], None)
            acc_ref[h] += pv
            r_ref[h] = r
            hmax = jnp.max(r)
            rmax = hmax if rmax is None else jnp.maximum(rmax, hmax)
        return j - 1, rmax

    lax.while_loop(cond, body, (i - 2, rmax0))
    for h in range(heads):
        o_ref[:, h * HEAD_DIM:(h + 1) * HEAD_DIM] = acc_ref[h].astype(BF16)


def _cumsum_matrix(ch):
    j = np.arange(ch)[:, None]
    s = np.arange(ch)[None, :]
    m = -(j >= s).astype(np.float32)
    return jnp.asarray(np.concatenate([m, m], axis=0), dtype=BF16)


def _attn_near_kernel(q_ref, kd_ref, kp_ref, vd_ref, vp_ref, m_ref, *rest, tq, n_heads, n_cast, d_ff):
    cast_in, (o_ref, rmax_ref), cast_out = rest[:n_cast], rest[n_cast:n_cast + 2], rest[n_cast + 2:]
    for w_ref, wo_ref in zip(cast_in, cast_out, strict=True):
        _cast_gate_up_kernel(w_ref, wo_ref, d_ff=d_ff)
    i = pl.program_id(0)
    m_neg = m_ref[...]
    row = lax.broadcasted_iota(jnp.int32, (tq, tq), 0)
    col = lax.broadcasted_iota(jnp.int32, (tq, tq), 1)
    causal = col < row
    no_prev = jnp.where(i == 0, jnp.float32(-1e30), jnp.float32(0.0))
    heads = [slice(h * HEAD_DIM, (h + 1) * HEAD_DIM) for h in range(n_heads)]
    z_d, z_p, sp_d, sp_p, from_d, from_p, a_d, a_p = ({} for _ in range(8))
    rmax = []

    def stage(s, h):
        hs = heads[h]
        if s == 0:
            z_d[h] = _attn_scores(q_ref[:, hs], kd_ref[:, hs])
            z_p[h] = _attn_scores(q_ref[:, hs], kp_ref[:, hs])
        elif s == 1:
            sp_d[h] = _attn_softplus(z_d[h], causal)
            sp_p[h] = _attn_softplus(z_p[h], None)
        elif s == 2:
            from_d[h] = _dot(sp_d[h][0], m_neg)
            from_p[h] = _dot(sp_p[h][0], m_neg)
        elif s == 3:
            a_d[h] = _attn_weights(z_d[h], from_d[h], None, causal)
            a_p[h] = _attn_weights(z_p[h], from_p[h], sp_d[h][1] + no_prev, None)
        else:
            o_ref[:, hs] = (_dot(a_d[h], vd_ref[:, hs]) + _dot(a_p[h], vp_ref[:, hs])).astype(BF16)
            rmax.append(sp_d[h][1] + no_prev + sp_p[h][1])

    for t in range(n_heads + 4):
        for h in range(n_heads):
            if 0 <= t - h < 5:
                stage(t - h, h)
    rm = functools.reduce(jnp.maximum, rmax)
    rm = jnp.max(rm.reshape(tq // 8, 8, HEAD_DIM), axis=0)
    rmax_ref[0] = jnp.where(i <= 1, jnp.float32(-1e30), rm)


def _attention_near(qkv, n_heads, gate_up_casts):
    s = qkv.shape[0]
    tq = ATTN_Q
    width = n_heads * HEAD_DIM
    assert s % tq == 0 and qkv.shape[1] == 3 * width
    nq = s // tq
    prev = lambda i: jnp.maximum(i - 1, 0)
    n_cast = len(gate_up_casts)
    _, d, two_ff = gate_up_casts[0][0].shape
    d_ff = two_ff // 2
    nj = (d_ff + FFN_COLS - 1) // FFN_COLS
    rb = d // nq
    assert d % nq == 0 and rb % 16 == 0
    cast_in_specs = [pl.BlockSpec((None, rb, two_ff), functools.partial(lambda layer, i: (layer, i, 0), layer))
                     for _, layer in gate_up_casts]
    cast_out_spec = pl.BlockSpec((nj, rb, 2 * FFN_COLS), lambda i: (0, i, 0))
    cast_out_shape = jax.ShapeDtypeStruct((nj, d, 2 * FFN_COLS), BF16)
    outs = pl.pallas_call(
        functools.partial(_attn_near_kernel, tq=tq, n_heads=n_heads, n_cast=n_cast, d_ff=d_ff),
        grid=(nq,),
        in_specs=[
            pl.BlockSpec((tq, width), lambda i: (i, 0)),
            pl.BlockSpec((tq, width), lambda i: (i, 1)),
            pl.BlockSpec((tq, width), lambda i: (prev(i), 1)),
            pl.BlockSpec((tq, width), lambda i: (i, 2)),
            pl.BlockSpec((tq, width), lambda i: (prev(i), 2)),
            pl.BlockSpec((2 * tq, tq), lambda i: (0, 0)),
        ] + cast_in_specs,
        out_specs=[pl.BlockSpec((tq, width), lambda i: (i, 0)),
                   pl.BlockSpec((1, 8, HEAD_DIM), lambda i: (i, 0, 0))] + [cast_out_spec] * n_cast,
        out_shape=[jax.ShapeDtypeStruct((s, width), BF16),
                   jax.ShapeDtypeStruct((nq, 8, HEAD_DIM), F32)] + [cast_out_shape] * n_cast,
        compiler_params=pltpu.CompilerParams(
            dimension_semantics=("parallel",), vmem_limit_bytes=VMEM_LIMIT),
        name="stickbreak_attn_near",
    )(qkv, qkv, qkv, qkv, qkv, _cumsum_matrix(tq), *[w for w, _ in gate_up_casts])
    return outs[0], outs[1], outs[2:]


def _attention(qkv, n_heads, gate_up_casts):
    y_near, rmax, cast = _attention_near(qkv, n_heads, gate_up_casts)
    need_far = jnp.max(rmax) > LOG2_ZERO_BOUND
    y = lax.cond(need_far, lambda qkv_, y_: _attention_walk(qkv_, n_heads),
                 lambda qkv_, y_: y_, qkv, y_near)
    return y, cast


def _attention_walk(qkv, n_heads):
    s = qkv.shape[0]
    tq = ATTN_Q
    assert s % tq == 0 and s >= 2 * tq and n_heads % ATTN_HEADS == 0
    width = ATTN_HEADS * HEAD_DIM
    groups = n_heads // ATTN_HEADS
    return pl.pallas_call(
        functools.partial(_attn_kernel, tq=tq),
        grid=(groups, s // tq),
        in_specs=[
            pl.BlockSpec((tq, width), lambda g, i: (i, g)),
            pl.BlockSpec((s, width), lambda g, i: (0, groups + g)),
            pl.BlockSpec((s, width), lambda g, i: (0, 2 * groups + g)),
            pl.BlockSpec((2 * tq, tq), lambda g, i: (0, 0)),
        ],
        out_specs=pl.BlockSpec((tq, width), lambda g, i: (i, g)),
        out_shape=jax.ShapeDtypeStruct((s, n_heads * HEAD_DIM), BF16),
        scratch_shapes=[pltpu.VMEM((ATTN_HEADS, tq, HEAD_DIM), F32),
                        pltpu.VMEM((ATTN_HEADS, tq, HEAD_DIM), F32)],
        compiler_params=pltpu.CompilerParams(
            dimension_semantics=("parallel", "arbitrary"), vmem_limit_bytes=VMEM_LIMIT),
        name="stickbreak_attn",
    )(qkv, qkv, qkv, _cumsum_matrix(tq))


def _outproj_kernel(x_ref, ya_ref, ybc_ref, wa_ref, wb_ref, o_ref):
    o_ref[...] = x_ref[...] + _dot(ya_ref[...], wa_ref[...]) + _dot(ybc_ref[...], wb_ref[...])


def _outproj(x, y_a, y_bc, w_out):
    s, d = x.shape
    ka = y_a.shape[1]
    kb = y_bc.shape[1]
    assert ka == kb
    tm = min(PROJ_ROWS, s)
    return pl.pallas_call(
        _outproj_kernel,
        grid=(s // tm,),
        in_specs=[
            pl.BlockSpec((tm, d), lambda i: (i, 0)),
            pl.BlockSpec((tm, ka), lambda i: (i, 0)),
            pl.BlockSpec((tm, kb), lambda i: (i, 0)),
            pl.BlockSpec((ka, d), lambda i: (0, 0)),
            pl.BlockSpec((kb, d), lambda i: (1, 0)),
        ],
        out_specs=pl.BlockSpec((tm, d), lambda i: (i, 0)),
        out_shape=jax.ShapeDtypeStruct((s, d), F32),
        compiler_params=pltpu.CompilerParams(
            dimension_semantics=("parallel",), vmem_limit_bytes=VMEM_LIMIT),
        name="out_proj",
    )(x, y_a, y_bc, w_out, w_out)


def _cast_kernel(x_ref, o_ref):
    o_ref[...] = x_ref[...].astype(BF16)


def _cast_gate_up_kernel(x_ref, o_ref, *, d_ff):
    nj, rows, two_tf = o_ref.shape
    tf = two_tf // 2
    for j in range(nj):
        n = min(tf, d_ff - j * tf)
        o_ref[j, :, :n] = x_ref[:, j * tf:j * tf + n].astype(BF16)
        o_ref[j, :, tf:tf + n] = x_ref[:, d_ff + j * tf:d_ff + j * tf + n].astype(BF16)
        if n < tf:
            zeros = jnp.zeros((rows, tf - n), BF16)
            o_ref[j, :, n:tf] = zeros
            o_ref[j, :, tf + n:] = zeros


def _cast_down_kernel(x_ref, o_ref, *, d_ff):
    o_ref[:d_ff, :] = (x_ref[...] * FFN_RES_WEIGHT).astype(BF16)
    if o_ref.shape[0] > d_ff:
        o_ref[d_ff:, :] = jnp.zeros((o_ref.shape[0] - d_ff, o_ref.shape[1]), BF16)


def _cast_call(body, w, layer, in_block, out_block, out_shape, grid, index, name, out_index=None):
    return pl.pallas_call(
        body,
        grid=(grid,),
        in_specs=[pl.BlockSpec((None,) + in_block, lambda i: (layer,) + index(i))],
        out_specs=pl.BlockSpec(out_block, out_index or index),
        out_shape=jax.ShapeDtypeStruct(out_shape, BF16),
        compiler_params=pltpu.CompilerParams(
            dimension_semantics=("parallel",), vmem_limit_bytes=VMEM_LIMIT),
        name=name,
    )(w)


def _cast_plain(w, layer):
    _, rows, cols = w.shape
    rb = CAST_ROWS
    assert rows % rb == 0
    return _cast_call(_cast_kernel, w, layer, (rb, cols), (rb, cols), (rows, cols), rows // rb,
                      lambda i: (i, 0), "cast_bf16")


def _cast_gate_up(w_gu, layer):
    _, d, two_ff = w_gu.shape
    d_ff = two_ff // 2
    nj = (d_ff + FFN_COLS - 1) // FFN_COLS
    rb = CAST_ROWS // 2
    assert d % rb == 0
    return _cast_call(functools.partial(_cast_gate_up_kernel, d_ff=d_ff), w_gu, layer,
                      (rb, two_ff), (nj, rb, 2 * FFN_COLS), (nj, d, 2 * FFN_COLS), d // rb,
                      lambda i: (i, 0), "cast_gate_up", out_index=lambda i: (0, i, 0))


def _cast_down(w_down, layer):
    _, d_ff, d = w_down.shape
    d_ffp = d_ff + (-d_ff) % FFN_COLS
    assert d % CAST_COLS == 0
    return _cast_call(functools.partial(_cast_down_kernel, d_ff=d_ff), w_down, layer,
                      (d_ff, CAST_COLS), (d_ffp, CAST_COLS), (d_ffp, d), d // CAST_COLS,
                      lambda i: (0, i), "cast_down")


def kernel(x, ffn1_norm, ffn1_w_gu, ffn1_w_down, mix_norm, w_in, sgu_norm, sgu_w, sgu_b, pool_w,
           pool_scale, w_out, ffn2_norm, ffn2_w_gu, ffn2_w_down, final_norm):
    b, s, d = x.shape
    assert b == 1
    depth = w_in.shape[0]
    attn_width = w_out.shape[1] // 2
    n_heads = attn_width // HEAD_DIM
    width = SGU_GROUPS * GROUP_DIM
    h = x.reshape(s, d)
    fg = final_norm.reshape(1, d)
    wgu1 = _cast_gate_up(ffn1_w_gu, 0)
    for l in range(depth):
        wd1 = _cast_down(ffn1_w_down, l)
        wd2 = _cast_down(ffn2_w_down, l)
        w_in_l = _cast_plain(w_in, l)
        mix_g = mix_norm[l].reshape(1, d)
        sgu_bias = jnp.repeat(sgu_b[l].T, GROUP_DIM, axis=1)

        h = _ffn(h, ffn1_norm[l].reshape(1, d), wgu1, wd1, fg, final=False)
        qkv, y_bc = _inproj(h, mix_g, w_in_l, attn_width, sgu_norm[l].reshape(1, width),
                            sgu_w[l], sgu_bias, pool_w[l].astype(BF16), pool_scale[l].reshape(1, width))
        hosted = [(ffn2_w_gu, l)] + ([(ffn1_w_gu, l + 1)] if l + 1 < depth else [])
        y_a, hosted_bf16 = _attention(qkv, n_heads, hosted)
        h = _outproj(h, y_a, y_bc, _cast_plain(w_out, l))
        h = _ffn(h, ffn2_norm[l].reshape(1, d), hosted_bf16[0], wd2, fg, final=(l == depth - 1))
        if l + 1 < depth:
            wgu1 = hosted_bf16[1]
    return h.reshape(b, s, d)
```

```python
import functools
import math
from typing import Callable, NamedTuple

import numpy as np
import jax
import jax.numpy as jnp
from jax import lax
from jax.experimental import pallas as pl
from jax.experimental.pallas import tpu as pltpu

F32 = jnp.float32
BF16 = jnp.bfloat16

HEAD_DIM = 128
LANES = 128
BF16_ROWS = 16
SGU_GROUPS = 4
SGU_CHUNK = 128
POOL_WINDOWS = (2, 4, 8, 16)
GROUP_DIM = 128
EPS = 1e-6
FFN_RES_WEIGHT = 0.5
assert math.frexp(FFN_RES_WEIGHT)[0] == 0.5

V7X_VMEM_BYTES = 64 * 1024 * 1024
VMEM_LIMIT = V7X_VMEM_BYTES - 6 * 1024 * 1024

FFN_ROWS = 1024
FFN_COLS = 512
PROJ_ROWS = 512
QKV_COLS = 1024
CAST_ROWS = 512
CAST_COLS = 512
ATTN_Q = 256
ATTN_HEADS = 2
LOG2_E = 1.4426950408889634
LOG2_ZERO_BOUND = -151.0


def _rmsnorm(x, g):
    inv = lax.rsqrt(jnp.mean(x * x, axis=-1, keepdims=True) + EPS)
    return x * inv * g


def _dot(a, b):
    return jnp.dot(a, b, preferred_element_type=F32)


def _ffn_kernel(x_ref, g_ref, wgu_ref, wd_ref, fg_ref, o_ref, xn_ref, *, nj, final):
    j = pl.program_id(1)
    tf = wd_ref.shape[0]

    @pl.when(j == 0)
    def _():
        xn_ref[...] = _rmsnorm(x_ref[...], g_ref[...]).astype(BF16)

    def down_proj():
        gu = _dot(xn_ref[...], wgu_ref[...])
        gate = gu[:, :tf]
        h = (gate * jax.nn.sigmoid(gate) * gu[:, tf:]).astype(BF16)
        return _dot(h, wd_ref[...])

    @pl.when(j == 0)
    def _():
        o_ref[...] = x_ref[...] + down_proj()

    @pl.when(j > 0)
    def _():
        o_ref[...] += down_proj()

    if final:
        @pl.when(j == nj - 1)
        def _():
            o_ref[...] = _rmsnorm(o_ref[...], fg_ref[...])


def _ffn(x, g, w_gu_p, w_down_p, fg, *, final):
    s, d = x.shape
    nj = w_gu_p.shape[0]
    assert w_gu_p.shape == (nj, d, 2 * FFN_COLS) and w_down_p.shape == (nj * FFN_COLS, d)
    tm = min(FFN_ROWS, s)
    ni = s // tm
    x_index = lambda i, j: (jnp.minimum(i + jnp.minimum(j, 1), ni - 1), 0)
    return pl.pallas_call(
        functools.partial(_ffn_kernel, nj=nj, final=final),
        grid=(ni, nj),
        in_specs=[
            pl.BlockSpec((tm, d), x_index),
            pl.BlockSpec((1, d), lambda i, j: (0, 0)),
            pl.BlockSpec((None, d, 2 * FFN_COLS), lambda i, j: (j, 0, 0)),
            pl.BlockSpec((FFN_COLS, d), lambda i, j: (j, 0)),
            pl.BlockSpec((1, d), lambda i, j: (0, 0)),
        ],
        out_specs=pl.BlockSpec((tm, d), lambda i, j: (i, 0)),
        out_shape=jax.ShapeDtypeStruct((s, d), F32),
        scratch_shapes=[pltpu.VMEM((tm, d), BF16)],
        compiler_params=pltpu.CompilerParams(
            dimension_semantics=("arbitrary", "arbitrary"), vmem_limit_bytes=VMEM_LIMIT),
        name="ffn",
    )(x, g, w_gu_p, w_down_p, fg)


def _inproj_kernel(x_ref, g_ref, w_ref, sn_ref, sw_ref, sb_ref, band_ref, pw_ref, ps_ref,
                   qkv_ref, o_ref, ph_ref, plo_ref, *, tm, attn_width, scale):
    i = pl.program_id(0)
    ch = SGU_CHUNK
    width = SGU_GROUPS * GROUP_DIM
    xn = _rmsnorm(x_ref[...], g_ref[...]).astype(BF16)
    proj = _dot(xn, w_ref[:, 3 * attn_width:])
    u = jax.nn.gelu(proj[:, :width])
    v = jax.nn.gelu(proj[:, width:2 * width])
    p = proj[:, 2 * width:]

    @pl.when(i == 0)
    def _():
        ph_ref[0:ch, :] = jnp.zeros((ch, width), BF16)
        plo_ref[0:ch, :] = jnp.zeros((ch, width), BF16)

    p_hi = p.astype(BF16)
    ph_ref[ch:ch + tm, :] = p_hi
    plo_ref[ch:ch + tm, :] = (p - p_hi.astype(F32)).astype(BF16)

    row = lax.broadcasted_iota(jnp.int32, (ch, ch), 0)
    col = lax.broadcasted_iota(jnp.int32, (ch, ch), 1)
    tril = col <= row
    units = [(grp, c) for grp in range(SGU_GROUPS) for c in range(tm // ch)]
    lanes = lambda grp: slice(grp * GROUP_DIM, (grp + 1) * GROUP_DIM)
    rows = lambda c: slice(c * ch, (c + 1) * ch)
    vg = [_rmsnorm(v[:, lanes(grp)], sn_ref[:, lanes(grp)]).astype(BF16) for grp in range(SGU_GROUPS)]
    w_causal = [jnp.where(tril, sw_ref[grp], 0.0).astype(BF16) for grp in range(SGU_GROUPS)]
    mixed = {(grp, c): _dot(w_causal[grp], vg[grp][rows(c)]) for grp, c in units}
    wsum = {(grp, c): _dot(band_ref[grp], ph_ref[c * ch:(c + 2) * ch, lanes(grp)])
            + _dot(band_ref[grp], plo_ref[c * ch:(c + 2) * ch, lanes(grp)]) for grp, c in units}
    for grp, c in units:
        o_ref[rows(c), lanes(grp)] = (u[rows(c), lanes(grp)] * (mixed[grp, c] + sb_ref[:, lanes(grp)])).astype(BF16)
    dlt = {}
    for grp, c in units:
        pos = i * tm + c * ch + row
        count = jnp.minimum(pos + 1, POOL_WINDOWS[grp]).astype(F32)
        dlt[grp, c] = (wsum[grp, c] / count - p[rows(c), lanes(grp)]).astype(BF16)
    for grp, c in units:
        y = _dot(dlt[grp, c], pw_ref[grp]) * ps_ref[:, lanes(grp)]
        o_ref[rows(c), width + grp * GROUP_DIM:width + (grp + 1) * GROUP_DIM] = y.astype(BF16)

    ph_ref[0:ch, :] = ph_ref[tm:tm + ch, :]
    plo_ref[0:ch, :] = plo_ref[tm:tm + ch, :]

    for c in range(3 * attn_width // QKV_COLS):
        cols = slice(c * QKV_COLS, (c + 1) * QKV_COLS)
        acc = _dot(xn, w_ref[:, cols])
        if (c + 1) * QKV_COLS <= attn_width:
            acc = acc * scale
        qkv_ref[:, cols] = acc.astype(BF16)


def _pool_bands():
    t = np.arange(SGU_CHUNK)[:, None] + SGU_CHUNK
    j = np.arange(2 * SGU_CHUNK)[None, :]
    bands = [((t - j >= 0) & (t - j < w)).astype(np.float32) for w in POOL_WINDOWS]
    return jnp.asarray(np.stack(bands), dtype=BF16)


def _inproj(x, g, w_in, attn_width, sgu_norm, sgu_w, sgu_bias, pool_w, pool_scale):
    s, d = x.shape
    width = SGU_GROUPS * GROUP_DIM
    n_in = w_in.shape[1]
    assert n_in == 3 * attn_width + 3 * width and attn_width % QKV_COLS == 0
    tm = min(PROJ_ROWS, s)
    full = lambda shape: pl.BlockSpec(shape, lambda i: (0,) * len(shape))
    return pl.pallas_call(
        functools.partial(_inproj_kernel, tm=tm, attn_width=attn_width,
                          scale=LOG2_E / math.sqrt(HEAD_DIM)),
        grid=(s // tm,),
        in_specs=[
            pl.BlockSpec((tm, d), lambda i: (i, 0)),
            full((1, d)),
            pl.BlockSpec((d, n_in), lambda i: (0, 0), pipeline_mode=pl.Buffered(1)),
            full((1, width)),
            full((SGU_GROUPS, SGU_CHUNK, SGU_CHUNK)),
            full((SGU_CHUNK, width)),
            full((SGU_GROUPS, SGU_CHUNK, 2 * SGU_CHUNK)),
            full((SGU_GROUPS, GROUP_DIM, GROUP_DIM)),
            full((1, width)),
        ],
        out_specs=[pl.BlockSpec((tm, 3 * attn_width), lambda i: (i, 0)),
                   pl.BlockSpec((tm, 2 * width), lambda i: (i, 0))],
        out_shape=[jax.ShapeDtypeStruct((s, 3 * attn_width), BF16),
                   jax.ShapeDtypeStruct((s, 2 * width), BF16)],
        scratch_shapes=[pltpu.VMEM((tm + SGU_CHUNK, width), BF16),
                        pltpu.VMEM((tm + SGU_CHUNK, width), BF16)],
        compiler_params=pltpu.CompilerParams(
            dimension_semantics=("arbitrary",), vmem_limit_bytes=VMEM_LIMIT),
        name="in_proj",
    )(x, g, w_in, sgu_norm, sgu_w, sgu_bias, _pool_bands(), pool_w, pool_scale)


def _attn_block(q, ks, vs, m_neg, r, causal):
    z2 = _attn_scores(q, ks)
    hilo, total = _attn_softplus(z2, causal)
    a = _attn_weights(z2, _dot(hilo, m_neg), r, causal)
    return _dot(a, vs), (total if r is None else r + total)


def _attn_scores(q, ks):
    return lax.dot_general(q, ks, (((1,), (1,)), ((), ())), preferred_element_type=F32)


def _attn_softplus(z2, causal):
    neg_abs = pltpu.bitcast(pltpu.bitcast(z2, jnp.uint32) | jnp.uint32(0x80000000), F32)
    sp = jnp.maximum(z2, 0.0) + jnp.log(1.0 + jnp.exp2(neg_abs)) * LOG2_E
    if causal is not None:
        sp = jnp.where(causal, sp, 0.0)
    hi = pltpu.bitcast(pltpu.bitcast(sp, jnp.uint32) & jnp.uint32(0xFFFF0000), F32)
    lo = sp - hi
    total = jnp.broadcast_to(-jnp.sum(sp, axis=1, keepdims=True), (z2.shape[0], HEAD_DIM))
    return jnp.concatenate([hi.astype(BF16), lo.astype(BF16)], axis=1), total


def _attn_weights(z2, from_here, r, causal):
    log_a = z2 + from_here
    if r is not None:
        log_a = log_a + jnp.concatenate([r] * (z2.shape[1] // HEAD_DIM), axis=1)
    a = jnp.exp2(log_a)
    if causal is not None:
        a = jnp.where(causal, a, 0.0)
    return a.astype(BF16)


def _attn_kernel(q_ref, k_ref, v_ref, m_ref, o_ref, r_ref, acc_ref, *, tq):
    i = pl.program_id(1)
    m_neg = m_ref[...]
    heads = r_ref.shape[0]
    row = lax.broadcasted_iota(jnp.int32, (tq, tq), 0)
    col = lax.broadcasted_iota(jnp.int32, (tq, tq), 1)
    causal = col < row
    diag = pl.multiple_of(i * tq, tq)
    prev = pl.multiple_of(jnp.maximum(i - 1, 0) * tq, tq)
    no_prev = jnp.where(i == 0, jnp.float32(-1e30), jnp.float32(0.0))

    rmax0 = None
    for h in range(heads):
        hs = slice(h * HEAD_DIM, (h + 1) * HEAD_DIM)
        q = q_ref[:, hs]
        pv0, r = _attn_block(q, k_ref[pl.ds(diag, tq), hs], v_ref[pl.ds(diag, tq), hs], m_neg, None, causal)
        pv1, r = _attn_block(q, k_ref[pl.ds(prev, tq), hs], v_ref[pl.ds(prev, tq), hs], m_neg,
                             r + no_prev, None)
        acc_ref[h] = pv0 + pv1
        r_ref[h] = r
        hmax = jnp.max(r)
        rmax0 = hmax if rmax0 is None else jnp.maximum(rmax0, hmax)

    def cond(carry):
        j, rmax = carry
        return jnp.logical_and(j >= 0, rmax > LOG2_ZERO_BOUND)

    def body(carry):
        j, _ = carry
        start = pl.multiple_of(j * tq, tq)
        rmax = None
        for h in range(heads):
            hs = slice(h * HEAD_DIM, (h + 1) * HEAD_DIM)
            pv, r = _attn_block(q_ref[:, hs], k_ref[pl.ds(start, tq), hs], v_ref[pl.ds(start, tq), hs],
                                m_neg, r_ref[h], None)
            acc_ref[h] += pv
            r_ref[h] = r
            hmax = jnp.max(r)
            rmax = hmax if rmax is None else jnp.maximum(rmax, hmax)
        return j - 1, rmax

    lax.while_loop(cond, body, (i - 2, rmax0))
    for h in range(heads):
        o_ref[:, h * HEAD_DIM:(h + 1) * HEAD_DIM] = acc_ref[h].astype(BF16)


def _cumsum_matrix(ch):
    j = np.arange(ch)[:, None]
    s = np.arange(ch)[None, :]
    m = -(j >= s).astype(np.float32)
    return jnp.asarray(np.concatenate([m, m], axis=0), dtype=BF16)


def _attn_near_kernel(q_ref, kd_ref, kp_ref, vd_ref, vp_ref, m_ref, *rest, tq, n_heads, cast_bodies):
    n_cast = len(cast_bodies)
    cast_in, (o_ref, rmax_ref), cast_out = rest[:n_cast], rest[n_cast:n_cast + 2], rest[n_cast + 2:]
    i = pl.program_id(0)
    for body, w_ref, wo_ref in zip(cast_bodies, cast_in, cast_out, strict=True):
        body(i, w_ref, wo_ref)
    m_neg = m_ref[...]
    row = lax.broadcasted_iota(jnp.int32, (tq, tq), 0)
    col = lax.broadcasted_iota(jnp.int32, (tq, tq), 1)
    causal = col < row
    no_prev = jnp.where(i == 0, jnp.float32(-1e30), jnp.float32(0.0))
    heads = [slice(h * HEAD_DIM, (h + 1) * HEAD_DIM) for h in range(n_heads)]
    z_d, z_p, sp_d, sp_p, from_d, from_p, a_d, a_p = ({} for _ in range(8))
    rmax = []

    def stage(s, h):
        hs = heads[h]
        if s == 0:
            z_d[h] = _attn_scores(q_ref[:, hs], kd_ref[:, hs])
            z_p[h] = _attn_scores(q_ref[:, hs], kp_ref[:, hs])
        elif s == 1:
            sp_d[h] = _attn_softplus(z_d[h], causal)
            sp_p[h] = _attn_softplus(z_p[h], None)
        elif s == 2:
            from_d[h] = _dot(sp_d[h][0], m_neg)
            from_p[h] = _dot(sp_p[h][0], m_neg)
        elif s == 3:
            a_d[h] = _attn_weights(z_d[h], from_d[h], None, causal)
            a_p[h] = _attn_weights(z_p[h], from_p[h], sp_d[h][1] + no_prev, None)
        else:
            o_ref[:, hs] = (_dot(a_d[h], vd_ref[:, hs]) + _dot(a_p[h], vp_ref[:, hs])).astype(BF16)
            rmax.append(sp_d[h][1] + no_prev + sp_p[h][1])

    for t in range(n_heads + 4):
        for h in range(n_heads):
            if 0 <= t - h < 5:
                stage(t - h, h)
    rm = functools.reduce(jnp.maximum, rmax)
    rm = jnp.max(rm.reshape(tq // 8, 8, HEAD_DIM), axis=0)
    rmax_ref[0] = jnp.where(i <= 1, jnp.float32(-1e30), rm)


def _attention_near(qkv, n_heads, cast_jobs):
    s = qkv.shape[0]
    tq = ATTN_Q
    width = n_heads * HEAD_DIM
    assert s % tq == 0 and qkv.shape[1] == 3 * width
    nq = s // tq
    prev = lambda i: jnp.maximum(i - 1, 0)
    jobs = [make(nq) for make in cast_jobs]
    outs = pl.pallas_call(
        functools.partial(_attn_near_kernel, tq=tq, n_heads=n_heads,
                          cast_bodies=tuple(job.body for job in jobs)),
        grid=(nq,),
        in_specs=[
            pl.BlockSpec((tq, width), lambda i: (i, 0)),
            pl.BlockSpec((tq, width), lambda i: (i, 1)),
            pl.BlockSpec((tq, width), lambda i: (prev(i), 1)),
            pl.BlockSpec((tq, width), lambda i: (i, 2)),
            pl.BlockSpec((tq, width), lambda i: (prev(i), 2)),
            pl.BlockSpec((2 * tq, tq), lambda i: (0, 0)),
        ] + [job.in_spec for job in jobs],
        out_specs=[pl.BlockSpec((tq, width), lambda i: (i, 0)),
                   pl.BlockSpec((1, 8, HEAD_DIM), lambda i: (i, 0, 0))] + [job.out_spec for job in jobs],
        out_shape=[jax.ShapeDtypeStruct((s, width), BF16),
                   jax.ShapeDtypeStruct((nq, 8, HEAD_DIM), F32)] + [job.out_shape for job in jobs],
        compiler_params=pltpu.CompilerParams(
            dimension_semantics=("arbitrary",), vmem_limit_bytes=VMEM_LIMIT),
        name="stickbreak_attn_near",
    )(qkv, qkv, qkv, qkv, qkv, _cumsum_matrix(tq), *[job.w for job in jobs])
    return outs[0], outs[1], outs[2:]


def _attention(qkv, n_heads, cast_jobs):
    y_near, rmax, cast = _attention_near(qkv, n_heads, cast_jobs)
    need_far = jnp.max(rmax) > LOG2_ZERO_BOUND
    y = lax.cond(need_far, lambda qkv_, y_: _attention_walk(qkv_, n_heads),
                 lambda qkv_, y_: y_, qkv, y_near)
    return y, cast


def _attention_walk(qkv, n_heads):
    s = qkv.shape[0]
    tq = ATTN_Q
    assert s % tq == 0 and s >= 2 * tq and n_heads % ATTN_HEADS == 0
    width = ATTN_HEADS * HEAD_DIM
    groups = n_heads // ATTN_HEADS
    return pl.pallas_call(
        functools.partial(_attn_kernel, tq=tq),
        grid=(groups, s // tq),
        in_specs=[
            pl.BlockSpec((tq, width), lambda g, i: (i, g)),
            pl.BlockSpec((s, width), lambda g, i: (0, groups + g)),
            pl.BlockSpec((s, width), lambda g, i: (0, 2 * groups + g)),
            pl.BlockSpec((2 * tq, tq), lambda g, i: (0, 0)),
        ],
        out_specs=pl.BlockSpec((tq, width), lambda g, i: (i, g)),
        out_shape=jax.ShapeDtypeStruct((s, n_heads * HEAD_DIM), BF16),
        scratch_shapes=[pltpu.VMEM((ATTN_HEADS, tq, HEAD_DIM), F32),
                        pltpu.VMEM((ATTN_HEADS, tq, HEAD_DIM), F32)],
        compiler_params=pltpu.CompilerParams(
            dimension_semantics=("parallel", "arbitrary"), vmem_limit_bytes=VMEM_LIMIT),
        name="stickbreak_attn",
    )(qkv, qkv, qkv, _cumsum_matrix(tq))


def _outproj_kernel(x_ref, ya_ref, ybc_ref, wa_ref, wb_ref, o_ref):
    o_ref[...] = x_ref[...] + _dot(ya_ref[...], wa_ref[...]) + _dot(ybc_ref[...], wb_ref[...])


def _outproj(x, y_a, y_bc, w_out):
    s, d = x.shape
    ka = y_a.shape[1]
    kb = y_bc.shape[1]
    assert ka == kb
    tm = min(PROJ_ROWS, s)
    return pl.pallas_call(
        _outproj_kernel,
        grid=(s // tm,),
        in_specs=[
            pl.BlockSpec((tm, d), lambda i: (i, 0)),
            pl.BlockSpec((tm, ka), lambda i: (i, 0)),
            pl.BlockSpec((tm, kb), lambda i: (i, 0)),
            pl.BlockSpec((ka, d), lambda i: (0, 0)),
            pl.BlockSpec((kb, d), lambda i: (1, 0)),
        ],
        out_specs=pl.BlockSpec((tm, d), lambda i: (i, 0)),
        out_shape=jax.ShapeDtypeStruct((s, d), F32),
        compiler_params=pltpu.CompilerParams(
            dimension_semantics=("parallel",), vmem_limit_bytes=VMEM_LIMIT),
        name="out_proj",
    )(x, y_a, y_bc, w_out, w_out)


def _cast_kernel(x_ref, o_ref):
    o_ref[...] = x_ref[...].astype(BF16)


def _cast_gate_up_kernel(x_ref, o_ref, *, d_ff):
    nj, rows, two_tf = o_ref.shape
    tf = two_tf // 2
    for j in range(nj):
        n = min(tf, d_ff - j * tf)
        o_ref[j, :, :n] = x_ref[:, j * tf:j * tf + n].astype(BF16)
        o_ref[j, :, tf:tf + n] = x_ref[:, d_ff + j * tf:d_ff + j * tf + n].astype(BF16)
        if n < tf:
            zeros = jnp.zeros((rows, tf - n), BF16)
            o_ref[j, :, n:tf] = zeros
            o_ref[j, :, tf + n:] = zeros


def _cast_down_kernel(x_ref, o_ref, *, d_ff):
    o_ref[:d_ff, :] = (x_ref[...] * FFN_RES_WEIGHT).astype(BF16)
    if o_ref.shape[0] > d_ff:
        o_ref[d_ff:, :] = jnp.zeros((o_ref.shape[0] - d_ff, o_ref.shape[1]), BF16)


def _cast_call(body, w, layer, in_block, out_block, out_shape, grid, index, name, out_index=None):
    return pl.pallas_call(
        body,
        grid=(grid,),
        in_specs=[pl.BlockSpec((None,) + in_block, lambda i: (layer,) + index(i))],
        out_specs=pl.BlockSpec(out_block, out_index or index),
        out_shape=jax.ShapeDtypeStruct(out_shape, BF16),
        compiler_params=pltpu.CompilerParams(
            dimension_semantics=("parallel",), vmem_limit_bytes=VMEM_LIMIT),
        name=name,
    )(w)


def _cast_plain(w, layer):
    _, rows, cols = w.shape
    rb = CAST_ROWS
    assert rows % rb == 0
    return _cast_call(_cast_kernel, w, layer, (rb, cols), (rb, cols), (rows, cols), rows // rb,
                      lambda i: (i, 0), "cast_bf16")


def _cast_gate_up(w_gu, layer):
    _, d, two_ff = w_gu.shape
    d_ff = two_ff // 2
    nj = (d_ff + FFN_COLS - 1) // FFN_COLS
    rb = CAST_ROWS // 2
    assert d % rb == 0
    return _cast_call(functools.partial(_cast_gate_up_kernel, d_ff=d_ff), w_gu, layer,
                      (rb, two_ff), (nj, rb, 2 * FFN_COLS), (nj, d, 2 * FFN_COLS), d // rb,
                      lambda i: (i, 0), "cast_gate_up", out_index=lambda i: (0, i, 0))


def _cast_down(w_down, layer):
    _, d_ff, d = w_down.shape
    d_ffp = d_ff + (-d_ff) % FFN_COLS
    assert d % CAST_COLS == 0
    return _cast_call(functools.partial(_cast_down_kernel, d_ff=d_ff), w_down, layer,
                      (d_ff, CAST_COLS), (d_ffp, CAST_COLS), (d_ffp, d), d // CAST_COLS,
                      lambda i: (0, i), "cast_down")


class _CastJob(NamedTuple):
    w: jax.Array
    in_spec: pl.BlockSpec
    out_spec: pl.BlockSpec
    out_shape: jax.ShapeDtypeStruct
    body: Callable


def _gate_up_job(w_gu, layer, n):
    _, d, two_ff = w_gu.shape
    d_ff = two_ff // 2
    nj = (d_ff + FFN_COLS - 1) // FFN_COLS
    rb = d // n
    assert d % n == 0 and rb % BF16_ROWS == 0
    return _CastJob(w_gu, pl.BlockSpec((None, rb, two_ff), lambda i: (layer, i, 0)),
                    pl.BlockSpec((nj, rb, 2 * FFN_COLS), lambda i: (0, i, 0)),
                    jax.ShapeDtypeStruct((nj, d, 2 * FFN_COLS), BF16),
                    lambda i, w_ref, o_ref: _cast_gate_up_kernel(w_ref, o_ref, d_ff=d_ff))


def _plain_job(w, layer, n):
    _, rows, cols = w.shape
    rb = rows // n
    assert rows % n == 0 and rb % BF16_ROWS == 0
    return _CastJob(w, pl.BlockSpec((None, rb, cols), lambda i: (layer, i, 0)),
                    pl.BlockSpec((rb, cols), lambda i: (i, 0)),
                    jax.ShapeDtypeStruct((rows, cols), BF16),
                    lambda i, w_ref, o_ref: _cast_kernel(w_ref, o_ref))


def _cast_down_part_kernel(i, x_ref, o_ref, *, parts):
    d_ff = x_ref.shape[0]
    rows = d_ff // parts
    start = pl.multiple_of((i % parts) * rows, BF16_ROWS)
    o_ref[pl.ds(start, rows), :] = (x_ref[pl.ds(start, rows), :] * FFN_RES_WEIGHT).astype(BF16)
    if o_ref.shape[0] > d_ff:
        o_ref[d_ff:, :] = jnp.zeros((o_ref.shape[0] - d_ff, o_ref.shape[1]), BF16)


def _down_job(w_down, layer, n):
    _, d_ff, d = w_down.shape
    d_ffp = d_ff + (-d_ff) % FFN_COLS
    blocks = d // LANES
    parts = n // blocks
    assert n % blocks == 0 and d_ff % parts == 0 and (d_ff // parts) % BF16_ROWS == 0
    return _CastJob(w_down, pl.BlockSpec((None, d_ff, LANES), lambda i: (layer, 0, i // parts)),
                    pl.BlockSpec((d_ffp, LANES), lambda i: (0, i // parts)),
                    jax.ShapeDtypeStruct((d_ffp, d), BF16),
                    functools.partial(_cast_down_part_kernel, parts=parts))


def kernel(x, ffn1_norm, ffn1_w_gu, ffn1_w_down, mix_norm, w_in, sgu_norm, sgu_w, sgu_b, pool_w,
           pool_scale, w_out, ffn2_norm, ffn2_w_gu, ffn2_w_down, final_norm):
    b, s, d = x.shape
    assert b == 1
    depth = w_in.shape[0]
    attn_width = w_out.shape[1] // 2
    n_heads = attn_width // HEAD_DIM
    width = SGU_GROUPS * GROUP_DIM
    h = x.reshape(s, d)
    fg = final_norm.reshape(1, d)
    wgu1 = _cast_gate_up(ffn1_w_gu, 0)
    wd1 = _cast_down(ffn1_w_down, 0)
    w_in_l = _cast_plain(w_in, 0)
    for l in range(depth):
        mix_g = mix_norm[l].reshape(1, d)
        sgu_bias = jnp.repeat(sgu_b[l].T, GROUP_DIM, axis=1)

        h = _ffn(h, ffn1_norm[l].reshape(1, d), wgu1, wd1, fg, final=False)
        qkv, y_bc = _inproj(h, mix_g, w_in_l, attn_width, sgu_norm[l].reshape(1, width),
                            sgu_w[l], sgu_bias, pool_w[l].astype(BF16), pool_scale[l].reshape(1, width))
        jobs = [functools.partial(_plain_job, w_out, l),
                functools.partial(_gate_up_job, ffn2_w_gu, l),
                functools.partial(_down_job, ffn2_w_down, l)]
        if l + 1 < depth:
            jobs += [functools.partial(_plain_job, w_in, l + 1),
                     functools.partial(_gate_up_job, ffn1_w_gu, l + 1),
                     functools.partial(_down_job, ffn1_w_down, l + 1)]
        y_a, cast = _attention(qkv, n_heads, jobs)
        w_out_l, wgu2, wd2 = cast[:3]
        h = _outproj(h, y_a, y_bc, w_out_l)
        h = _ffn(h, ffn2_norm[l].reshape(1, d), wgu2, wd2, fg, final=(l == depth - 1))
        if l + 1 < depth:
            w_in_l, wgu1, wd1 = cast[3:]
    return h.reshape(b, s, d)
```

```python
import functools
import math
from typing import Callable, NamedTuple

import numpy as np
import jax
import jax.numpy as jnp
from jax import lax
from jax.experimental import pallas as pl
from jax.experimental.pallas import tpu as pltpu

F32 = jnp.float32
BF16 = jnp.bfloat16

HEAD_DIM = 128
LANES = 128
BF16_ROWS = 16
SGU_GROUPS = 4
SGU_CHUNK = 128
POOL_WINDOWS = (2, 4, 8, 16)
GROUP_DIM = 128
EPS = 1e-6
FFN_RES_WEIGHT = 0.5
assert math.frexp(FFN_RES_WEIGHT)[0] == 0.5

V7X_VMEM_BYTES = 64 * 1024 * 1024
VMEM_LIMIT = V7X_VMEM_BYTES - 6 * 1024 * 1024

FFN_ROWS = 1024
FFN_COLS = 512
PROJ_ROWS = 512
QKV_COLS = 1024
CAST_ROWS = 512
CAST_COLS = 512
ATTN_Q = 256
ATTN_HEADS = 2
LOG2_E = 1.4426950408889634
LOG2_ZERO_BOUND = -151.0


def _rmsnorm(x, g):
    inv = lax.rsqrt(jnp.mean(x * x, axis=-1, keepdims=True) + EPS)
    return x * inv * g


def _dot(a, b):
    return jnp.dot(a, b, preferred_element_type=F32)


def _ffn_kernel(x_ref, g_ref, wgu_ref, wd_ref, fg_ref, o_ref, xn_ref, *, nj, final):
    j = pl.program_id(1)
    tf = wd_ref.shape[0]

    @pl.when(j == 0)
    def _():
        xn_ref[...] = _rmsnorm(x_ref[...], g_ref[...]).astype(BF16)

    def down_proj():
        gu = _dot(xn_ref[...], wgu_ref[...])
        gate = gu[:, :tf]
        h = (gate * jax.nn.sigmoid(gate) * gu[:, tf:]).astype(BF16)
        return _dot(h, wd_ref[...])

    @pl.when(j == 0)
    def _():
        o_ref[...] = x_ref[...] + down_proj()

    @pl.when(j > 0)
    def _():
        o_ref[...] += down_proj()

    if final:
        @pl.when(j == nj - 1)
        def _():
            o_ref[...] = _rmsnorm(o_ref[...], fg_ref[...])


def _ffn(x, g, w_gu_p, w_down_p, fg, *, final):
    s, d = x.shape
    nj = w_gu_p.shape[0]
    assert w_gu_p.shape == (nj, d, 2 * FFN_COLS) and w_down_p.shape == (nj * FFN_COLS, d)
    tm = min(FFN_ROWS, s)
    ni = s // tm
    x_index = lambda i, j: (jnp.minimum(i + jnp.minimum(j, 1), ni - 1), 0)
    return pl.pallas_call(
        functools.partial(_ffn_kernel, nj=nj, final=final),
        grid=(ni, nj),
        in_specs=[
            pl.BlockSpec((tm, d), x_index),
            pl.BlockSpec((1, d), lambda i, j: (0, 0)),
            pl.BlockSpec((None, d, 2 * FFN_COLS), lambda i, j: (j, 0, 0)),
            pl.BlockSpec((FFN_COLS, d), lambda i, j: (j, 0)),
            pl.BlockSpec((1, d), lambda i, j: (0, 0)),
        ],
        out_specs=pl.BlockSpec((tm, d), lambda i, j: (i, 0)),
        out_shape=jax.ShapeDtypeStruct((s, d), F32),
        scratch_shapes=[pltpu.VMEM((tm, d), BF16)],
        compiler_params=pltpu.CompilerParams(
            dimension_semantics=("arbitrary", "arbitrary"), vmem_limit_bytes=VMEM_LIMIT),
        name="ffn",
    )(x, g, w_gu_p, w_down_p, fg)


def _inproj_kernel(x_ref, g_ref, w_ref, sn_ref, sw_ref, sb_ref, band_ref, pw_ref, ps_ref,
                   qkv_ref, o_ref, ph_ref, plo_ref, *, tm, attn_width, scale):
    i = pl.program_id(0)
    ch = SGU_CHUNK
    width = SGU_GROUPS * GROUP_DIM
    xn = _rmsnorm(x_ref[...], g_ref[...]).astype(BF16)
    proj = _dot(xn, w_ref[:, 3 * attn_width:])
    u = jax.nn.gelu(proj[:, :width])
    v = jax.nn.gelu(proj[:, width:2 * width])
    p = proj[:, 2 * width:]

    @pl.when(i == 0)
    def _():
        ph_ref[0:ch, :] = jnp.zeros((ch, width), BF16)
        plo_ref[0:ch, :] = jnp.zeros((ch, width), BF16)

    p_hi = p.astype(BF16)
    ph_ref[ch:ch + tm, :] = p_hi
    plo_ref[ch:ch + tm, :] = (p - p_hi.astype(F32)).astype(BF16)

    row = lax.broadcasted_iota(jnp.int32, (ch, ch), 0)
    col = lax.broadcasted_iota(jnp.int32, (ch, ch), 1)
    tril = col <= row
    units = [(grp, c) for grp in range(SGU_GROUPS) for c in range(tm // ch)]
    lanes = lambda grp: slice(grp * GROUP_DIM, (grp + 1) * GROUP_DIM)
    rows = lambda c: slice(c * ch, (c + 1) * ch)
    vg = [_rmsnorm(v[:, lanes(grp)], sn_ref[:, lanes(grp)]).astype(BF16) for grp in range(SGU_GROUPS)]
    w_causal = [jnp.where(tril, sw_ref[grp], 0.0).astype(BF16) for grp in range(SGU_GROUPS)]
    mixed = {}
    for grp in range(SGU_GROUPS):
        for c in range(0, tm // ch, 2):
            pair = _dot(w_causal[grp], jnp.concatenate([vg[grp][rows(c)], vg[grp][rows(c + 1)]], axis=1))
            mixed[grp, c], mixed[grp, c + 1] = pair[:, :GROUP_DIM], pair[:, GROUP_DIM:]
    wsum = {}
    for grp, c in units:
        ext = slice(c * ch, (c + 2) * ch)
        pair = _dot(band_ref[grp], jnp.concatenate([ph_ref[ext, lanes(grp)], plo_ref[ext, lanes(grp)]], axis=1))
        wsum[grp, c] = pair[:, :GROUP_DIM] + pair[:, GROUP_DIM:]
    for grp, c in units:
        o_ref[rows(c), lanes(grp)] = (u[rows(c), lanes(grp)] * (mixed[grp, c] + sb_ref[:, lanes(grp)])).astype(BF16)
    dlt = {}
    for grp, c in units:
        pos = i * tm + c * ch + row
        count = jnp.minimum(pos + 1, POOL_WINDOWS[grp]).astype(F32)
        dlt[grp, c] = (wsum[grp, c] / count - p[rows(c), lanes(grp)]).astype(BF16)
    for grp, c in units:
        y = _dot(dlt[grp, c], pw_ref[grp]) * ps_ref[:, lanes(grp)]
        o_ref[rows(c), width + grp * GROUP_DIM:width + (grp + 1) * GROUP_DIM] = y.astype(BF16)

    ph_ref[0:ch, :] = ph_ref[tm:tm + ch, :]
    plo_ref[0:ch, :] = plo_ref[tm:tm + ch, :]

    for c in range(3 * attn_width // QKV_COLS):
        cols = slice(c * QKV_COLS, (c + 1) * QKV_COLS)
        acc = _dot(xn, w_ref[:, cols])
        if (c + 1) * QKV_COLS <= attn_width:
            acc = acc * scale
        qkv_ref[:, cols] = acc.astype(BF16)


def _pool_bands():
    t = np.arange(SGU_CHUNK)[:, None] + SGU_CHUNK
    j = np.arange(2 * SGU_CHUNK)[None, :]
    bands = [((t - j >= 0) & (t - j < w)).astype(np.float32) for w in POOL_WINDOWS]
    return jnp.asarray(np.stack(bands), dtype=BF16)


def _inproj(x, g, w_in, attn_width, sgu_norm, sgu_w, sgu_bias, pool_w, pool_scale):
    s, d = x.shape
    width = SGU_GROUPS * GROUP_DIM
    n_in = w_in.shape[1]
    assert n_in == 3 * attn_width + 3 * width and attn_width % QKV_COLS == 0
    tm = min(PROJ_ROWS, s)
    full = lambda shape: pl.BlockSpec(shape, lambda i: (0,) * len(shape))
    return pl.pallas_call(
        functools.partial(_inproj_kernel, tm=tm, attn_width=attn_width,
                          scale=LOG2_E / math.sqrt(HEAD_DIM)),
        grid=(s // tm,),
        in_specs=[
            pl.BlockSpec((tm, d), lambda i: (i, 0)),
            full((1, d)),
            pl.BlockSpec((d, n_in), lambda i: (0, 0), pipeline_mode=pl.Buffered(1)),
            full((1, width)),
            full((SGU_GROUPS, SGU_CHUNK, SGU_CHUNK)),
            full((SGU_CHUNK, width)),
            full((SGU_GROUPS, SGU_CHUNK, 2 * SGU_CHUNK)),
            full((SGU_GROUPS, GROUP_DIM, GROUP_DIM)),
            full((1, width)),
        ],
        out_specs=[pl.BlockSpec((tm, 3 * attn_width), lambda i: (i, 0)),
                   pl.BlockSpec((tm, 2 * width), lambda i: (i, 0))],
        out_shape=[jax.ShapeDtypeStruct((s, 3 * attn_width), BF16),
                   jax.ShapeDtypeStruct((s, 2 * width), BF16)],
        scratch_shapes=[pltpu.VMEM((tm + SGU_CHUNK, width), BF16),
                        pltpu.VMEM((tm + SGU_CHUNK, width), BF16)],
        compiler_params=pltpu.CompilerParams(
            dimension_semantics=("arbitrary",), vmem_limit_bytes=VMEM_LIMIT),
        name="in_proj",
    )(x, g, w_in, sgu_norm, sgu_w, sgu_bias, _pool_bands(), pool_w, pool_scale)


def _attn_block(q, ks, vs, m_neg, r, causal):
    log_own, sp16, total = _attn_softplus(_attn_scores(q, ks), causal)
    a = _attn_weights(log_own, _dot(sp16, m_neg), r, causal)
    return _dot(a, vs), (total if r is None else r + total)


def _attn_scores(q, ks):
    return lax.dot_general(q, ks, (((1,), (1,)), ((), ())), preferred_element_type=F32)


def _attn_softplus(z2, causal):
    neg_abs = pltpu.bitcast(pltpu.bitcast(z2, jnp.uint32) | jnp.uint32(0x80000000), F32)
    sp = jnp.maximum(z2, 0.0) + jnp.log(1.0 + jnp.exp2(neg_abs)) * LOG2_E
    if causal is not None:
        sp = jnp.where(causal, sp, 0.0)
    total = jnp.broadcast_to(-jnp.sum(sp, axis=1, keepdims=True), (z2.shape[0], HEAD_DIM))
    return z2 - sp, sp.astype(BF16), total


def _attn_weights(log_own, later, r, causal):
    log_a = log_own + later
    if r is not None:
        log_a = log_a + jnp.concatenate([r] * (log_own.shape[1] // HEAD_DIM), axis=1)
    a = jnp.exp2(log_a)
    if causal is not None:
        a = jnp.where(causal, a, 0.0)
    return a.astype(BF16)


def _attn_kernel(q_ref, k_ref, v_ref, m_ref, o_ref, r_ref, acc_ref, *, tq):
    i = pl.program_id(1)
    m_neg = m_ref[...]
    heads = r_ref.shape[0]
    row = lax.broadcasted_iota(jnp.int32, (tq, tq), 0)
    col = lax.broadcasted_iota(jnp.int32, (tq, tq), 1)
    causal = col < row
    diag = pl.multiple_of(i * tq, tq)
    prev = pl.multiple_of(jnp.maximum(i - 1, 0) * tq, tq)
    no_prev = jnp.where(i == 0, jnp.float32(-1e30), jnp.float32(0.0))

    rmax0 = None
    for h in range(heads):
        hs = slice(h * HEAD_DIM, (h + 1) * HEAD_DIM)
        q = q_ref[:, hs]
        pv0, r = _attn_block(q, k_ref[pl.ds(diag, tq), hs], v_ref[pl.ds(diag, tq), hs], m_neg, None, causal)
        pv1, r = _attn_block(q, k_ref[pl.ds(prev, tq), hs], v_ref[pl.ds(prev, tq), hs], m_neg,
                             r + no_prev, None)
        acc_ref[h] = pv0 + pv1
        r_ref[h] = r
        hmax = jnp.max(r)
        rmax0 = hmax if rmax0 is None else jnp.maximum(rmax0, hmax)

    def cond(carry):
        j, rmax = carry
        return jnp.logical_and(j >= 0, rmax > LOG2_ZERO_BOUND)

    def body(carry):
        j, _ = carry
        start = pl.multiple_of(j * tq, tq)
        rmax = None
        for h in range(heads):
            hs = slice(h * HEAD_DIM, (h + 1) * HEAD_DIM)
            pv, r = _attn_block(q_ref[:, hs], k_ref[pl.ds(start, tq), hs], v_ref[pl.ds(start, tq), hs],
                                m_neg, r_ref[h], None)
            acc_ref[h] += pv
            r_ref[h] = r
            hmax = jnp.max(r)
            rmax = hmax if rmax is None else jnp.maximum(rmax, hmax)
        return j - 1, rmax

    lax.while_loop(cond, body, (i - 2, rmax0))
    for h in range(heads):
        o_ref[:, h * HEAD_DIM:(h + 1) * HEAD_DIM] = acc_ref[h].astype(BF16)


def _cumsum_matrix(ch):
    j = np.arange(ch)[:, None]
    s = np.arange(ch)[None, :]
    return jnp.asarray(-(j > s).astype(np.float32), dtype=BF16)


def _attn_near_kernel(q_ref, kd_ref, kp_ref, vd_ref, vp_ref, m_ref, *rest, tq, n_heads, cast_bodies):
    n_cast = len(cast_bodies)
    cast_in, (o_ref, rmax_ref), cast_out = rest[:n_cast], rest[n_cast:n_cast + 2], rest[n_cast + 2:]
    i = pl.program_id(0)
    for body, w_ref, wo_ref in zip(cast_bodies, cast_in, cast_out, strict=True):
        body(i, w_ref, wo_ref)
    m_neg = m_ref[...]
    row = lax.broadcasted_iota(jnp.int32, (tq, tq), 0)
    col = lax.broadcasted_iota(jnp.int32, (tq, tq), 1)
    causal = col < row
    no_prev = jnp.where(i == 0, jnp.float32(-1e30), jnp.float32(0.0))
    heads = [slice(h * HEAD_DIM, (h + 1) * HEAD_DIM) for h in range(n_heads)]
    z_d, z_p, sp_d, sp_p, from_d, from_p, a_d, a_p = ({} for _ in range(8))
    rmax = []

    def stage(s, h):
        hs = heads[h]
        if s == 0:
            z_d[h] = _attn_scores(q_ref[:, hs], kd_ref[:, hs])
            z_p[h] = _attn_scores(q_ref[:, hs], kp_ref[:, hs])
        elif s == 1:
            sp_d[h] = _attn_softplus(z_d[h], causal)
            sp_p[h] = _attn_softplus(z_p[h], None)
        elif s == 2:
            from_d[h] = _dot(sp_d[h][1], m_neg)
            from_p[h] = _dot(sp_p[h][1], m_neg)
        elif s == 3:
            a_d[h] = _attn_weights(sp_d[h][0], from_d[h], None, causal)
            a_p[h] = _attn_weights(sp_p[h][0], from_p[h], sp_d[h][2] + no_prev, None)
        else:
            o_ref[:, hs] = (_dot(a_d[h], vd_ref[:, hs]) + _dot(a_p[h], vp_ref[:, hs])).astype(BF16)
            rmax.append(sp_d[h][2] + no_prev + sp_p[h][2])

    for t in range(n_heads + 4):
        for h in range(n_heads):
            if 0 <= t - h < 5:
                stage(t - h, h)
    rm = functools.reduce(jnp.maximum, rmax)
    rm = jnp.max(rm.reshape(tq // 8, 8, HEAD_DIM), axis=0)
    rmax_ref[0] = jnp.where(i <= 1, jnp.float32(-1e30), rm)


def _attention_near(qkv, n_heads, cast_jobs):
    s = qkv.shape[0]
    tq = ATTN_Q
    width = n_heads * HEAD_DIM
    assert s % tq == 0 and qkv.shape[1] == 3 * width
    nq = s // tq
    prev = lambda i: jnp.maximum(i - 1, 0)
    jobs = [make(nq) for make in cast_jobs]
    outs = pl.pallas_call(
        functools.partial(_attn_near_kernel, tq=tq, n_heads=n_heads,
                          cast_bodies=tuple(job.body for job in jobs)),
        grid=(nq,),
        in_specs=[
            pl.BlockSpec((tq, width), lambda i: (i, 0)),
            pl.BlockSpec((tq, width), lambda i: (i, 1)),
            pl.BlockSpec((tq, width), lambda i: (prev(i), 1)),
            pl.BlockSpec((tq, width), lambda i: (i, 2)),
            pl.BlockSpec((tq, width), lambda i: (prev(i), 2)),
            pl.BlockSpec((tq, tq), lambda i: (0, 0)),
        ] + [job.in_spec for job in jobs],
        out_specs=[pl.BlockSpec((tq, width), lambda i: (i, 0)),
                   pl.BlockSpec((1, 8, HEAD_DIM), lambda i: (i, 0, 0))] + [job.out_spec for job in jobs],
        out_shape=[jax.ShapeDtypeStruct((s, width), BF16),
                   jax.ShapeDtypeStruct((nq, 8, HEAD_DIM), F32)] + [job.out_shape for job in jobs],
        compiler_params=pltpu.CompilerParams(
            dimension_semantics=("arbitrary",), vmem_limit_bytes=VMEM_LIMIT),
        name="stickbreak_attn_near",
    )(qkv, qkv, qkv, qkv, qkv, _cumsum_matrix(tq), *[job.w for job in jobs])
    return outs[0], outs[1], outs[2:]


def _attention(qkv, n_heads, cast_jobs):
    y_near, rmax, cast = _attention_near(qkv, n_heads, cast_jobs)
    need_far = jnp.max(rmax) > LOG2_ZERO_BOUND
    y = lax.cond(need_far, lambda qkv_, y_: _attention_walk(qkv_, n_heads),
                 lambda qkv_, y_: y_, qkv, y_near)
    return y, cast


def _attention_walk(qkv, n_heads):
    s = qkv.shape[0]
    tq = ATTN_Q
    assert s % tq == 0 and s >= 2 * tq and n_heads % ATTN_HEADS == 0
    width = ATTN_HEADS * HEAD_DIM
    groups = n_heads // ATTN_HEADS
    return pl.pallas_call(
        functools.partial(_attn_kernel, tq=tq),
        grid=(groups, s // tq),
        in_specs=[
            pl.BlockSpec((tq, width), lambda g, i: (i, g)),
            pl.BlockSpec((s, width), lambda g, i: (0, groups + g)),
            pl.BlockSpec((s, width), lambda g, i: (0, 2 * groups + g)),
            pl.BlockSpec((tq, tq), lambda g, i: (0, 0)),
        ],
        out_specs=pl.BlockSpec((tq, width), lambda g, i: (i, g)),
        out_shape=jax.ShapeDtypeStruct((s, n_heads * HEAD_DIM), BF16),
        scratch_shapes=[pltpu.VMEM((ATTN_HEADS, tq, HEAD_DIM), F32),
                        pltpu.VMEM((ATTN_HEADS, tq, HEAD_DIM), F32)],
        compiler_params=pltpu.CompilerParams(
            dimension_semantics=("parallel", "arbitrary"), vmem_limit_bytes=VMEM_LIMIT),
        name="stickbreak_attn",
    )(qkv, qkv, qkv, _cumsum_matrix(tq))


def _outproj_kernel(x_ref, ya_ref, ybc_ref, wa_ref, wb_ref, o_ref):
    o_ref[...] = x_ref[...] + _dot(ya_ref[...], wa_ref[...]) + _dot(ybc_ref[...], wb_ref[...])


def _outproj(x, y_a, y_bc, w_out):
    s, d = x.shape
    ka = y_a.shape[1]
    kb = y_bc.shape[1]
    assert ka == kb
    tm = min(PROJ_ROWS, s)
    return pl.pallas_call(
        _outproj_kernel,
        grid=(s // tm,),
        in_specs=[
            pl.BlockSpec((tm, d), lambda i: (i, 0)),
            pl.BlockSpec((tm, ka), lambda i: (i, 0)),
            pl.BlockSpec((tm, kb), lambda i: (i, 0)),
            pl.BlockSpec((ka, d), lambda i: (0, 0)),
            pl.BlockSpec((kb, d), lambda i: (1, 0)),
        ],
        out_specs=pl.BlockSpec((tm, d), lambda i: (i, 0)),
        out_shape=jax.ShapeDtypeStruct((s, d), F32),
        compiler_params=pltpu.CompilerParams(
            dimension_semantics=("parallel",), vmem_limit_bytes=VMEM_LIMIT),
        name="out_proj",
    )(x, y_a, y_bc, w_out, w_out)


def _cast_kernel(x_ref, o_ref):
    o_ref[...] = x_ref[...].astype(BF16)


def _cast_gate_up_kernel(x_ref, o_ref, *, d_ff):
    nj, rows, two_tf = o_ref.shape
    tf = two_tf // 2
    for j in range(nj):
        n = min(tf, d_ff - j * tf)
        o_ref[j, :, :n] = x_ref[:, j * tf:j * tf + n].astype(BF16)
        o_ref[j, :, tf:tf + n] = x_ref[:, d_ff + j * tf:d_ff + j * tf + n].astype(BF16)
        if n < tf:
            zeros = jnp.zeros((rows, tf - n), BF16)
            o_ref[j, :, n:tf] = zeros
            o_ref[j, :, tf + n:] = zeros


def _cast_down_kernel(x_ref, o_ref, *, d_ff):
    o_ref[:d_ff, :] = (x_ref[...] * FFN_RES_WEIGHT).astype(BF16)
    if o_ref.shape[0] > d_ff:
        o_ref[d_ff:, :] = jnp.zeros((o_ref.shape[0] - d_ff, o_ref.shape[1]), BF16)


def _cast_call(body, w, layer, in_block, out_block, out_shape, grid, index, name, out_index=None):
    return pl.pallas_call(
        body,
        grid=(grid,),
        in_specs=[pl.BlockSpec((None,) + in_block, lambda i: (layer,) + index(i))],
        out_specs=pl.BlockSpec(out_block, out_index or index),
        out_shape=jax.ShapeDtypeStruct(out_shape, BF16),
        compiler_params=pltpu.CompilerParams(
            dimension_semantics=("parallel",), vmem_limit_bytes=VMEM_LIMIT),
        name=name,
    )(w)


def _cast_plain(w, layer):
    _, rows, cols = w.shape
    rb = CAST_ROWS
    assert rows % rb == 0
    return _cast_call(_cast_kernel, w, layer, (rb, cols), (rb, cols), (rows, cols), rows // rb,
                      lambda i: (i, 0), "cast_bf16")


def _cast_gate_up(w_gu, layer):
    _, d, two_ff = w_gu.shape
    d_ff = two_ff // 2
    nj = (d_ff + FFN_COLS - 1) // FFN_COLS
    rb = CAST_ROWS // 2
    assert d % rb == 0
    return _cast_call(functools.partial(_cast_gate_up_kernel, d_ff=d_ff), w_gu, layer,
                      (rb, two_ff), (nj, rb, 2 * FFN_COLS), (nj, d, 2 * FFN_COLS), d // rb,
                      lambda i: (i, 0), "cast_gate_up", out_index=lambda i: (0, i, 0))


def _cast_down(w_down, layer):
    _, d_ff, d = w_down.shape
    d_ffp = d_ff + (-d_ff) % FFN_COLS
    assert d % CAST_COLS == 0
    return _cast_call(functools.partial(_cast_down_kernel, d_ff=d_ff), w_down, layer,
                      (d_ff, CAST_COLS), (d_ffp, CAST_COLS), (d_ffp, d), d // CAST_COLS,
                      lambda i: (0, i), "cast_down")


class _CastJob(NamedTuple):
    w: jax.Array
    in_spec: pl.BlockSpec
    out_spec: pl.BlockSpec
    out_shape: jax.ShapeDtypeStruct
    body: Callable


def _gate_up_job(w_gu, layer, n):
    _, d, two_ff = w_gu.shape
    d_ff = two_ff // 2
    nj = (d_ff + FFN_COLS - 1) // FFN_COLS
    rb = d // n
    assert d % n == 0 and rb % BF16_ROWS == 0
    return _CastJob(w_gu, pl.BlockSpec((None, rb, two_ff), lambda i: (layer, i, 0)),
                    pl.BlockSpec((nj, rb, 2 * FFN_COLS), lambda i: (0, i, 0)),
                    jax.ShapeDtypeStruct((nj, d, 2 * FFN_COLS), BF16),
                    lambda i, w_ref, o_ref: _cast_gate_up_kernel(w_ref, o_ref, d_ff=d_ff))


def _plain_job(w, layer, n):
    _, rows, cols = w.shape
    rb = rows // n
    assert rows % n == 0 and rb % BF16_ROWS == 0
    return _CastJob(w, pl.BlockSpec((None, rb, cols), lambda i: (layer, i, 0)),
                    pl.BlockSpec((rb, cols), lambda i: (i, 0)),
                    jax.ShapeDtypeStruct((rows, cols), BF16),
                    lambda i, w_ref, o_ref: _cast_kernel(w_ref, o_ref))


def _cast_down_part_kernel(i, x_ref, o_ref, *, parts):
    d_ff = x_ref.shape[0]
    rows = d_ff // parts
    start = pl.multiple_of((i % parts) * rows, BF16_ROWS)
    o_ref[pl.ds(start, rows), :] = (x_ref[pl.ds(start, rows), :] * FFN_RES_WEIGHT).astype(BF16)
    if o_ref.shape[0] > d_ff:
        o_ref[d_ff:, :] = jnp.zeros((o_ref.shape[0] - d_ff, o_ref.shape[1]), BF16)


def _down_job(w_down, layer, n):
    _, d_ff, d = w_down.shape
    d_ffp = d_ff + (-d_ff) % FFN_COLS
    blocks = d // LANES
    parts = n // blocks
    assert n % blocks == 0 and d_ff % parts == 0 and (d_ff // parts) % BF16_ROWS == 0
    return _CastJob(w_down, pl.BlockSpec((None, d_ff, LANES), lambda i: (layer, 0, i // parts)),
                    pl.BlockSpec((d_ffp, LANES), lambda i: (0, i // parts)),
                    jax.ShapeDtypeStruct((d_ffp, d), BF16),
                    functools.partial(_cast_down_part_kernel, parts=parts))


def kernel(x, ffn1_norm, ffn1_w_gu, ffn1_w_down, mix_norm, w_in, sgu_norm, sgu_w, sgu_b, pool_w,
           pool_scale, w_out, ffn2_norm, ffn2_w_gu, ffn2_w_down, final_norm):
    b, s, d = x.shape
    assert b == 1
    depth = w_in.shape[0]
    attn_width = w_out.shape[1] // 2
    n_heads = attn_width // HEAD_DIM
    width = SGU_GROUPS * GROUP_DIM
    h = x.reshape(s, d)
    fg = final_norm.reshape(1, d)
    wgu1 = _cast_gate_up(ffn1_w_gu, 0)
    wd1 = _cast_down(ffn1_w_down, 0)
    w_in_l = _cast_plain(w_in, 0)
    for l in range(depth):
        mix_g = mix_norm[l].reshape(1, d)
        sgu_bias = jnp.repeat(sgu_b[l].T, GROUP_DIM, axis=1)

        h = _ffn(h, ffn1_norm[l].reshape(1, d), wgu1, wd1, fg, final=False)
        qkv, y_bc = _inproj(h, mix_g, w_in_l, attn_width, sgu_norm[l].reshape(1, width),
                            sgu_w[l], sgu_bias, pool_w[l].astype(BF16), pool_scale[l].reshape(1, width))
        jobs = [functools.partial(_plain_job, w_out, l),
                functools.partial(_gate_up_job, ffn2_w_gu, l),
                functools.partial(_down_job, ffn2_w_down, l)]
        if l + 1 < depth:
            jobs += [functools.partial(_plain_job, w_in, l + 1),
                     functools.partial(_gate_up_job, ffn1_w_gu, l + 1),
                     functools.partial(_down_job, ffn1_w_down, l + 1)]
        y_a, cast = _attention(qkv, n_heads, jobs)
        w_out_l, wgu2, wd2 = cast[:3]
        h = _outproj(h, y_a, y_bc, w_out_l)
        h = _ffn(h, ffn2_norm[l].reshape(1, d), wgu2, wd2, fg, final=(l == depth - 1))
        if l + 1 < depth:
            w_in_l, wgu1, wd1 = cast[3:]
    return h.reshape(b, s, d)
```

```python
import functools
import math
from typing import Callable, NamedTuple

import numpy as np
import jax
import jax.numpy as jnp
from jax import lax
from jax.experimental import pallas as pl
from jax.experimental.pallas import tpu as pltpu

F32 = jnp.float32
BF16 = jnp.bfloat16

HEAD_DIM = 128
LANES = 128
BF16_ROWS = 16
SGU_GROUPS = 4
SGU_CHUNK = 128
POOL_WINDOWS = (2, 4, 8, 16)
GROUP_DIM = 128
EPS = 1e-6
FFN_RES_WEIGHT = 0.5
assert math.frexp(FFN_RES_WEIGHT)[0] == 0.5

V7X_VMEM_BYTES = 64 * 1024 * 1024
VMEM_LIMIT = V7X_VMEM_BYTES - 6 * 1024 * 1024

FFN_ROWS = 1024
FFN_COLS = 512
PROJ_ROWS = 512
QKV_COLS = 1024
CAST_ROWS = 512
CAST_COLS = 512
ATTN_Q = 256
ATTN_HEADS = 2
LOG2_E = 1.4426950408889634
LOG2_ZERO_BOUND = -151.0


def _rmsnorm(x, g):
    inv = lax.rsqrt(jnp.mean(x * x, axis=-1, keepdims=True) + EPS)
    return x * inv * g


def _dot(a, b):
    return jnp.dot(a, b, preferred_element_type=F32)


def _ffn_kernel(x_ref, g_ref, wgu_ref, wd_ref, fg_ref, o_ref, xn_ref, *, nj, final):
    j = pl.program_id(1)
    tf = wd_ref.shape[0]

    @pl.when(j == 0)
    def _():
        xn_ref[...] = _rmsnorm(x_ref[...], g_ref[...]).astype(BF16)

    def down_proj():
        gu = _dot(xn_ref[...], wgu_ref[...])
        gate = gu[:, :tf]
        h = (gate * jax.nn.sigmoid(gate) * gu[:, tf:]).astype(BF16)
        return _dot(h, wd_ref[...])

    @pl.when(j == 0)
    def _():
        o_ref[...] = x_ref[...] + down_proj()

    @pl.when(j > 0)
    def _():
        o_ref[...] += down_proj()

    if final:
        @pl.when(j == nj - 1)
        def _():
            o_ref[...] = _rmsnorm(o_ref[...], fg_ref[...])


def _ffn(x, g, w_gu_p, w_down_p, fg, *, final):
    s, d = x.shape
    nj = w_gu_p.shape[0]
    assert w_gu_p.shape == (nj, d, 2 * FFN_COLS) and w_down_p.shape == (nj * FFN_COLS, d)
    tm = min(FFN_ROWS, s)
    ni = s // tm
    x_index = lambda i, j: (jnp.minimum(i + jnp.minimum(j, 1), ni - 1), 0)
    return pl.pallas_call(
        functools.partial(_ffn_kernel, nj=nj, final=final),
        grid=(ni, nj),
        in_specs=[
            pl.BlockSpec((tm, d), x_index),
            pl.BlockSpec((1, d), lambda i, j: (0, 0)),
            pl.BlockSpec((None, d, 2 * FFN_COLS), lambda i, j: (j, 0, 0)),
            pl.BlockSpec((FFN_COLS, d), lambda i, j: (j, 0)),
            pl.BlockSpec((1, d), lambda i, j: (0, 0)),
        ],
        out_specs=pl.BlockSpec((tm, d), lambda i, j: (i, 0)),
        out_shape=jax.ShapeDtypeStruct((s, d), F32),
        scratch_shapes=[pltpu.VMEM((tm, d), BF16)],
        compiler_params=pltpu.CompilerParams(
            dimension_semantics=("arbitrary", "arbitrary"), vmem_limit_bytes=VMEM_LIMIT),
        name="ffn",
    )(x, g, w_gu_p, w_down_p, fg)


def _inproj_kernel(x_ref, g_ref, w_ref, sn_ref, sw_ref, sb_ref, band_ref, pw_ref, ps_ref,
                   qkv_ref, o_ref, ph_ref, plo_ref, *, tm, attn_width, scale):
    i = pl.program_id(0)
    ch = SGU_CHUNK
    width = SGU_GROUPS * GROUP_DIM
    xn = _rmsnorm(x_ref[...], g_ref[...]).astype(BF16)
    proj = _dot(xn, w_ref[:, 3 * attn_width:])
    u = jax.nn.gelu(proj[:, :width])
    v = jax.nn.gelu(proj[:, width:2 * width])
    p = proj[:, 2 * width:]

    @pl.when(i == 0)
    def _():
        ph_ref[0:ch, :] = jnp.zeros((ch, width), BF16)
        plo_ref[0:ch, :] = jnp.zeros((ch, width), BF16)

    p_hi = p.astype(BF16)
    ph_ref[ch:ch + tm, :] = p_hi
    plo_ref[ch:ch + tm, :] = (p - p_hi.astype(F32)).astype(BF16)

    row = lax.broadcasted_iota(jnp.int32, (ch, ch), 0)
    col = lax.broadcasted_iota(jnp.int32, (ch, ch), 1)
    tril = col <= row
    units = [(grp, c) for grp in range(SGU_GROUPS) for c in range(tm // ch)]
    lanes = lambda grp: slice(grp * GROUP_DIM, (grp + 1) * GROUP_DIM)
    rows = lambda c: slice(c * ch, (c + 1) * ch)
    vg = [_rmsnorm(v[:, lanes(grp)], sn_ref[:, lanes(grp)]).astype(BF16) for grp in range(SGU_GROUPS)]
    w_causal = [jnp.where(tril, sw_ref[grp], 0.0).astype(BF16) for grp in range(SGU_GROUPS)]
    mixed = {}
    for grp in range(SGU_GROUPS):
        for c in range(0, tm // ch, 2):
            pair = _dot(w_causal[grp], jnp.concatenate([vg[grp][rows(c)], vg[grp][rows(c + 1)]], axis=1))
            mixed[grp, c], mixed[grp, c + 1] = pair[:, :GROUP_DIM], pair[:, GROUP_DIM:]
    wsum = {}
    for grp, c in units:
        ext = slice(c * ch, (c + 2) * ch)
        pair = _dot(band_ref[grp], jnp.concatenate([ph_ref[ext, lanes(grp)], plo_ref[ext, lanes(grp)]], axis=1))
        wsum[grp, c] = pair[:, :GROUP_DIM] + pair[:, GROUP_DIM:]
    for grp, c in units:
        o_ref[rows(c), lanes(grp)] = (u[rows(c), lanes(grp)] * (mixed[grp, c] + sb_ref[:, lanes(grp)])).astype(BF16)
    dlt = {}
    for grp, c in units:
        pos = i * tm + c * ch + row
        count = jnp.minimum(pos + 1, POOL_WINDOWS[grp]).astype(F32)
        dlt[grp, c] = (wsum[grp, c] / count - p[rows(c), lanes(grp)]).astype(BF16)
    for grp, c in units:
        y = _dot(dlt[grp, c], pw_ref[grp]) * ps_ref[:, lanes(grp)]
        o_ref[rows(c), width + grp * GROUP_DIM:width + (grp + 1) * GROUP_DIM] = y.astype(BF16)

    ph_ref[0:ch, :] = ph_ref[tm:tm + ch, :]
    plo_ref[0:ch, :] = plo_ref[tm:tm + ch, :]

    for c in range(3 * attn_width // QKV_COLS):
        cols = slice(c * QKV_COLS, (c + 1) * QKV_COLS)
        acc = _dot(xn, w_ref[:, cols])
        if (c + 1) * QKV_COLS <= attn_width:
            acc = acc * scale
        qkv_ref[:, cols] = acc.astype(BF16)


def _pool_bands():
    t = np.arange(SGU_CHUNK)[:, None] + SGU_CHUNK
    j = np.arange(2 * SGU_CHUNK)[None, :]
    bands = [((t - j >= 0) & (t - j < w)).astype(np.float32) for w in POOL_WINDOWS]
    return jnp.asarray(np.stack(bands), dtype=BF16)


def _inproj(x, g, w_in, attn_width, sgu_norm, sgu_w, sgu_bias, pool_w, pool_scale):
    s, d = x.shape
    width = SGU_GROUPS * GROUP_DIM
    n_in = w_in.shape[1]
    assert n_in == 3 * attn_width + 3 * width and attn_width % QKV_COLS == 0
    tm = min(PROJ_ROWS, s)
    full = lambda shape: pl.BlockSpec(shape, lambda i: (0,) * len(shape))
    return pl.pallas_call(
        functools.partial(_inproj_kernel, tm=tm, attn_width=attn_width,
                          scale=LOG2_E / math.sqrt(HEAD_DIM)),
        grid=(s // tm,),
        in_specs=[
            pl.BlockSpec((tm, d), lambda i: (i, 0)),
            full((1, d)),
            pl.BlockSpec((d, n_in), lambda i: (0, 0), pipeline_mode=pl.Buffered(1)),
            full((1, width)),
            full((SGU_GROUPS, SGU_CHUNK, SGU_CHUNK)),
            full((SGU_CHUNK, width)),
            full((SGU_GROUPS, SGU_CHUNK, 2 * SGU_CHUNK)),
            full((SGU_GROUPS, GROUP_DIM, GROUP_DIM)),
            full((1, width)),
        ],
        out_specs=[pl.BlockSpec((tm, 3 * attn_width), lambda i: (i, 0)),
                   pl.BlockSpec((tm, 2 * width), lambda i: (i, 0))],
        out_shape=[jax.ShapeDtypeStruct((s, 3 * attn_width), BF16),
                   jax.ShapeDtypeStruct((s, 2 * width), BF16)],
        scratch_shapes=[pltpu.VMEM((tm + SGU_CHUNK, width), BF16),
                        pltpu.VMEM((tm + SGU_CHUNK, width), BF16)],
        compiler_params=pltpu.CompilerParams(
            dimension_semantics=("arbitrary",), vmem_limit_bytes=VMEM_LIMIT),
        name="in_proj",
    )(x, g, w_in, sgu_norm, sgu_w, sgu_bias, _pool_bands(), pool_w, pool_scale)


def _attn_block(q, ks, vs, m_neg, r, causal):
    log_own, sp16, total = _attn_softplus(_attn_scores(q, ks), causal)
    a = _attn_weights(log_own, _dot(sp16, m_neg), r, causal)
    return _dot(a, vs), (total if r is None else r + total)


def _attn_scores(q, ks):
    return lax.dot_general(q, ks, (((1,), (1,)), ((), ())), preferred_element_type=F32)


def _attn_softplus(z2, causal):
    neg_abs = pltpu.bitcast(pltpu.bitcast(z2, jnp.uint32) | jnp.uint32(0x80000000), F32)
    sp = jnp.maximum(z2, 0.0) + jnp.log(1.0 + jnp.exp2(neg_abs)) * LOG2_E
    if causal is not None:
        sp = jnp.where(causal, sp, 0.0)
    total = jnp.broadcast_to(-jnp.sum(sp, axis=1, keepdims=True), (z2.shape[0], HEAD_DIM))
    return z2 - sp, sp.astype(BF16), total


def _attn_weights(log_own, later, r, causal):
    log_a = log_own + later
    if r is not None:
        log_a = log_a + jnp.concatenate([r] * (log_own.shape[1] // HEAD_DIM), axis=1)
    a = jnp.exp2(log_a)
    if causal is not None:
        a = jnp.where(causal, a, 0.0)
    return a.astype(BF16)


def _attn_kernel(q_ref, k_ref, v_ref, m_ref, o_ref, r_ref, acc_ref, *, tq):
    i = pl.program_id(1)
    m_neg = m_ref[...]
    heads = r_ref.shape[0]
    row = lax.broadcasted_iota(jnp.int32, (tq, tq), 0)
    col = lax.broadcasted_iota(jnp.int32, (tq, tq), 1)
    causal = col < row
    diag = pl.multiple_of(i * tq, tq)
    prev = pl.multiple_of(jnp.maximum(i - 1, 0) * tq, tq)
    no_prev = jnp.where(i == 0, jnp.float32(-1e30), jnp.float32(0.0))

    rmax0 = None
    for h in range(heads):
        hs = slice(h * HEAD_DIM, (h + 1) * HEAD_DIM)
        q = q_ref[:, hs]
        pv0, r = _attn_block(q, k_ref[pl.ds(diag, tq), hs], v_ref[pl.ds(diag, tq), hs], m_neg, None, causal)
        pv1, r = _attn_block(q, k_ref[pl.ds(prev, tq), hs], v_ref[pl.ds(prev, tq), hs], m_neg,
                             r + no_prev, None)
        acc_ref[h] = pv0 + pv1
        r_ref[h] = r
        hmax = jnp.max(r)
        rmax0 = hmax if rmax0 is None else jnp.maximum(rmax0, hmax)

    def cond(carry):
        j, rmax = carry
        return jnp.logical_and(j >= 0, rmax > LOG2_ZERO_BOUND)

    def body(carry):
        j, _ = carry
        start = pl.multiple_of(j * tq, tq)
        rmax = None
        for h in range(heads):
            hs = slice(h * HEAD_DIM, (h + 1) * HEAD_DIM)
            pv, r = _attn_block(q_ref[:, hs], k_ref[pl.ds(start, tq), hs], v_ref[pl.ds(start, tq), hs],
                                m_neg, r_ref[h], None)
            acc_ref[h] += pv
            r_ref[h] = r
            hmax = jnp.max(r)
            rmax = hmax if rmax is None else jnp.maximum(rmax, hmax)
        return j - 1, rmax

    lax.while_loop(cond, body, (i - 2, rmax0))
    for h in range(heads):
        o_ref[:, h * HEAD_DIM:(h + 1) * HEAD_DIM] = acc_ref[h].astype(BF16)


def _cumsum_matrix(ch):
    j = np.arange(ch)[:, None]
    s = np.arange(ch)[None, :]
    return jnp.asarray(-(j > s).astype(np.float32), dtype=BF16)


def _attn_near_kernel(q_ref, kd_ref, kp_ref, vd_ref, vp_ref, m_ref, *rest, tq, n_heads, cast_bodies):
    n_cast = len(cast_bodies)
    cast_in, (o_ref, rmax_ref), cast_out = rest[:n_cast], rest[n_cast:n_cast + 2], rest[n_cast + 2:]
    i = pl.program_id(0)
    for body, w_ref, wo_ref in zip(cast_bodies, cast_in, cast_out, strict=True):
        body(i, w_ref, wo_ref)
    m_neg = m_ref[...]
    row = lax.broadcasted_iota(jnp.int32, (tq, tq), 0)
    col = lax.broadcasted_iota(jnp.int32, (tq, tq), 1)
    causal = col < row
    no_prev = jnp.where(i == 0, jnp.float32(-1e30), jnp.float32(0.0))
    heads = [slice(h * HEAD_DIM, (h + 1) * HEAD_DIM) for h in range(n_heads)]
    z_d, z_p, sp_d, sp_p, from_d, from_p, a_d, a_p = ({} for _ in range(8))
    rmax = []

    def stage(s, h):
        hs = heads[h]
        if s == 0:
            z_d[h] = _attn_scores(q_ref[:, hs], kd_ref[:, hs])
            z_p[h] = _attn_scores(q_ref[:, hs], kp_ref[:, hs])
        elif s == 1:
            sp_d[h] = _attn_softplus(z_d[h], causal)
            sp_p[h] = _attn_softplus(z_p[h], None)
        elif s == 2:
            from_d[h] = _dot(sp_d[h][1], m_neg)
            from_p[h] = _dot(sp_p[h][1], m_neg)
        elif s == 3:
            a_d[h] = _attn_weights(sp_d[h][0], from_d[h], None, causal)
            a_p[h] = _attn_weights(sp_p[h][0], from_p[h], sp_d[h][2] + no_prev, None)
        else:
            o_ref[:, hs] = (_dot(a_d[h], vd_ref[:, hs]) + _dot(a_p[h], vp_ref[:, hs])).astype(BF16)
            rmax.append(sp_d[h][2] + no_prev + sp_p[h][2])

    for t in range(n_heads + 4):
        for h in range(n_heads):
            if 0 <= t - h < 5:
                stage(t - h, h)
    rm = functools.reduce(jnp.maximum, rmax)
    rm = jnp.max(rm.reshape(tq // 8, 8, HEAD_DIM), axis=0)
    rmax_ref[0] = jnp.where(i <= 1, jnp.float32(-1e30), rm)


def _attention_near(qkv, n_heads, cast_jobs):
    s = qkv.shape[0]
    tq = ATTN_Q
    width = n_heads * HEAD_DIM
    assert s % tq == 0 and qkv.shape[1] == 3 * width
    nq = s // tq
    prev = lambda i: jnp.maximum(i - 1, 0)
    jobs = [make(nq) for make in cast_jobs]
    outs = pl.pallas_call(
        functools.partial(_attn_near_kernel, tq=tq, n_heads=n_heads,
                          cast_bodies=tuple(job.body for job in jobs)),
        grid=(nq,),
        in_specs=[
            pl.BlockSpec((tq, width), lambda i: (i, 0)),
            pl.BlockSpec((tq, width), lambda i: (i, 1)),
            pl.BlockSpec((tq, width), lambda i: (prev(i), 1)),
            pl.BlockSpec((tq, width), lambda i: (i, 2)),
            pl.BlockSpec((tq, width), lambda i: (prev(i), 2)),
            pl.BlockSpec((tq, tq), lambda i: (0, 0)),
        ] + [job.in_spec for job in jobs],
        out_specs=[pl.BlockSpec((tq, width), lambda i: (i, 0)),
                   pl.BlockSpec((1, 8, HEAD_DIM), lambda i: (i, 0, 0))] + [job.out_spec for job in jobs],
        out_shape=[jax.ShapeDtypeStruct((s, width), BF16),
                   jax.ShapeDtypeStruct((nq, 8, HEAD_DIM), F32)] + [job.out_shape for job in jobs],
        compiler_params=pltpu.CompilerParams(
            dimension_semantics=("arbitrary",), vmem_limit_bytes=VMEM_LIMIT),
        name="stickbreak_attn_near",
    )(qkv, qkv, qkv, qkv, qkv, _cumsum_matrix(tq), *[job.w for job in jobs])
    return outs[0], outs[1], outs[2:]


def _attention(qkv, n_heads, cast_jobs):
    y_near, rmax, cast = _attention_near(qkv, n_heads, cast_jobs)
    need_far = jnp.max(rmax) > LOG2_ZERO_BOUND
    y = lax.cond(need_far, lambda qkv_, y_: _attention_walk(qkv_, n_heads),
                 lambda qkv_, y_: y_, qkv, y_near)
    return y, cast


def _attention_walk(qkv, n_heads):
    s = qkv.shape[0]
    tq = ATTN_Q
    assert s % tq == 0 and s >= 2 * tq and n_heads % ATTN_HEADS == 0
    width = ATTN_HEADS * HEAD_DIM
    groups = n_heads // ATTN_HEADS
    return pl.pallas_call(
        functools.partial(_attn_kernel, tq=tq),
        grid=(groups, s // tq),
        in_specs=[
            pl.BlockSpec((tq, width), lambda g, i: (i, g)),
            pl.BlockSpec((s, width), lambda g, i: (0, groups + g)),
            pl.BlockSpec((s, width), lambda g, i: (0, 2 * groups + g)),
            pl.BlockSpec((tq, tq), lambda g, i: (0, 0)),
        ],
        out_specs=pl.BlockSpec((tq, width), lambda g, i: (i, g)),
        out_shape=jax.ShapeDtypeStruct((s, n_heads * HEAD_DIM), BF16),
        scratch_shapes=[pltpu.VMEM((ATTN_HEADS, tq, HEAD_DIM), F32),
                        pltpu.VMEM((ATTN_HEADS, tq, HEAD_DIM), F32)],
        compiler_params=pltpu.CompilerParams(
            dimension_semantics=("parallel", "arbitrary"), vmem_limit_bytes=VMEM_LIMIT),
        name="stickbreak_attn",
    )(qkv, qkv, qkv, _cumsum_matrix(tq))


def _outproj_kernel(x_ref, ya_ref, ybc_ref, wa_ref, wb_ref, o_ref):
    o_ref[...] = x_ref[...] + _dot(ya_ref[...], wa_ref[...]) + _dot(ybc_ref[...], wb_ref[...])


def _outproj(x, y_a, y_bc, w_out):
    s, d = x.shape
    ka = y_a.shape[1]
    kb = y_bc.shape[1]
    assert ka == kb
    tm = min(PROJ_ROWS, s)
    return pl.pallas_call(
        _outproj_kernel,
        grid=(s // tm,),
        in_specs=[
            pl.BlockSpec((tm, d), lambda i: (i, 0)),
            pl.BlockSpec((tm, ka), lambda i: (i, 0)),
            pl.BlockSpec((tm, kb), lambda i: (i, 0)),
            pl.BlockSpec((ka, d), lambda i: (0, 0)),
            pl.BlockSpec((kb, d), lambda i: (1, 0)),
        ],
        out_specs=pl.BlockSpec((tm, d), lambda i: (i, 0)),
        out_shape=jax.ShapeDtypeStruct((s, d), F32),
        compiler_params=pltpu.CompilerParams(
            dimension_semantics=("parallel",), vmem_limit_bytes=VMEM_LIMIT),
        name="out_proj",
    )(x, y_a, y_bc, w_out, w_out)


def _cast_kernel(x_ref, o_ref):
    o_ref[...] = x_ref[...].astype(BF16)


def _cast_gate_up_kernel(x_ref, o_ref, *, d_ff):
    nj, rows, two_tf = o_ref.shape
    tf = two_tf // 2
    for j in range(nj):
        n = min(tf, d_ff - j * tf)
        o_ref[j, :, :n] = x_ref[:, j * tf:j * tf + n].astype(BF16)
        o_ref[j, :, tf:tf + n] = x_ref[:, d_ff + j * tf:d_ff + j * tf + n].astype(BF16)
        if n < tf:
            zeros = jnp.zeros((rows, tf - n), BF16)
            o_ref[j, :, n:tf] = zeros
            o_ref[j, :, tf + n:] = zeros


def _cast_down_kernel(x_ref, o_ref, *, d_ff):
    o_ref[:d_ff, :] = (x_ref[...] * FFN_RES_WEIGHT).astype(BF16)
    if o_ref.shape[0] > d_ff:
        o_ref[d_ff:, :] = jnp.zeros((o_ref.shape[0] - d_ff, o_ref.shape[1]), BF16)


def _cast_call(body, w, layer, in_block, out_block, out_shape, grid, index, name, out_index=None):
    return pl.pallas_call(
        body,
        grid=(grid,),
        in_specs=[pl.BlockSpec((None,) + in_block, lambda i: (layer,) + index(i))],
        out_specs=pl.BlockSpec(out_block, out_index or index),
        out_shape=jax.ShapeDtypeStruct(out_shape, BF16),
        compiler_params=pltpu.CompilerParams(
            dimension_semantics=("parallel",), vmem_limit_bytes=VMEM_LIMIT),
        name=name,
    )(w)


def _cast_plain(w, layer):
    _, rows, cols = w.shape
    rb = CAST_ROWS
    assert rows % rb == 0
    return _cast_call(_cast_kernel, w, layer, (rb, cols), (rb, cols), (rows, cols), rows // rb,
                      lambda i: (i, 0), "cast_bf16")


def _cast_gate_up(w_gu, layer):
    _, d, two_ff = w_gu.shape
    d_ff = two_ff // 2
    nj = (d_ff + FFN_COLS - 1) // FFN_COLS
    rb = CAST_ROWS // 2
    assert d % rb == 0
    return _cast_call(functools.partial(_cast_gate_up_kernel, d_ff=d_ff), w_gu, layer,
                      (rb, two_ff), (nj, rb, 2 * FFN_COLS), (nj, d, 2 * FFN_COLS), d // rb,
                      lambda i: (i, 0), "cast_gate_up", out_index=lambda i: (0, i, 0))


def _cast_down(w_down, layer):
    _, d_ff, d = w_down.shape
    d_ffp = d_ff + (-d_ff) % FFN_COLS
    assert d % CAST_COLS == 0
    return _cast_call(functools.partial(_cast_down_kernel, d_ff=d_ff), w_down, layer,
                      (d_ff, CAST_COLS), (d_ffp, CAST_COLS), (d_ffp, d), d // CAST_COLS,
                      lambda i: (0, i), "cast_down")


class _CastJob(NamedTuple):
    w: jax.Array
    in_spec: pl.BlockSpec
    out_spec: pl.BlockSpec
    out_shape: jax.ShapeDtypeStruct
    body: Callable


def _gate_up_job(w_gu, layer, n):
    _, d, two_ff = w_gu.shape
    d_ff = two_ff // 2
    nj = (d_ff + FFN_COLS - 1) // FFN_COLS
    rb = d // n
    assert d % n == 0 and rb % BF16_ROWS == 0
    return _CastJob(w_gu, pl.BlockSpec((None, rb, two_ff), lambda i: (layer, i, 0)),
                    pl.BlockSpec((nj, rb, 2 * FFN_COLS), lambda i: (0, i, 0)),
                    jax.ShapeDtypeStruct((nj, d, 2 * FFN_COLS), BF16),
                    lambda i, w_ref, o_ref: _cast_gate_up_kernel(w_ref, o_ref, d_ff=d_ff))


def _plain_job(w, layer, n):
    _, rows, cols = w.shape
    rb = rows // n
    assert rows % n == 0 and rb % BF16_ROWS == 0
    return _CastJob(w, pl.BlockSpec((None, rb, cols), lambda i: (layer, i, 0)),
                    pl.BlockSpec((rb, cols), lambda i: (i, 0)),
                    jax.ShapeDtypeStruct((rows, cols), BF16),
                    lambda i, w_ref, o_ref: _cast_kernel(w_ref, o_ref))


def _cast_down_rows_kernel(i, x_ref, o_ref, *, n_valid, n_blocks):
    if n_blocks == n_valid:
        o_ref[...] = (x_ref[...] * FFN_RES_WEIGHT).astype(BF16)
        return

    @pl.when(i < n_valid)
    def _():
        o_ref[...] = (x_ref[...] * FFN_RES_WEIGHT).astype(BF16)

    @pl.when(i >= n_valid)
    def _():
        o_ref[...] = jnp.zeros_like(o_ref)


def _down_job(w_down, layer, n):
    _, d_ff, d = w_down.shape
    d_ffp = d_ff + (-d_ff) % FFN_COLS
    rb = math.gcd(d_ff, d_ffp)
    n_valid, n_blocks = d_ff // rb, d_ffp // rb
    assert rb % BF16_ROWS == 0 and n_blocks <= n
    return _CastJob(w_down, pl.BlockSpec((None, rb, d), lambda i: (layer, jnp.minimum(i, n_valid - 1), 0)),
                    pl.BlockSpec((rb, d), lambda i: (jnp.minimum(i, n_blocks - 1), 0)),
                    jax.ShapeDtypeStruct((d_ffp, d), BF16),
                    functools.partial(_cast_down_rows_kernel, n_valid=n_valid, n_blocks=n_blocks))


def kernel(x, ffn1_norm, ffn1_w_gu, ffn1_w_down, mix_norm, w_in, sgu_norm, sgu_w, sgu_b, pool_w,
           pool_scale, w_out, ffn2_norm, ffn2_w_gu, ffn2_w_down, final_norm):
    b, s, d = x.shape
    assert b == 1
    depth = w_in.shape[0]
    attn_width = w_out.shape[1] // 2
    n_heads = attn_width // HEAD_DIM
    width = SGU_GROUPS * GROUP_DIM
    h = x.reshape(s, d)
    fg = final_norm.reshape(1, d)
    wgu1 = _cast_gate_up(ffn1_w_gu, 0)
    wd1 = _cast_down(ffn1_w_down, 0)
    w_in_l = _cast_plain(w_in, 0)
    for l in range(depth):
        mix_g = mix_norm[l].reshape(1, d)
        sgu_bias = jnp.repeat(sgu_b[l].T, GROUP_DIM, axis=1)

        h = _ffn(h, ffn1_norm[l].reshape(1, d), wgu1, wd1, fg, final=False)
        qkv, y_bc = _inproj(h, mix_g, w_in_l, attn_width, sgu_norm[l].reshape(1, width),
                            sgu_w[l], sgu_bias, pool_w[l].astype(BF16), pool_scale[l].reshape(1, width))
        jobs = [functools.partial(_plain_job, w_out, l),
                functools.partial(_gate_up_job, ffn2_w_gu, l),
                functools.partial(_down_job, ffn2_w_down, l)]
        if l + 1 < depth:
            jobs += [functools.partial(_plain_job, w_in, l + 1),
                     functools.partial(_gate_up_job, ffn1_w_gu, l + 1),
                     functools.partial(_down_job, ffn1_w_down, l + 1)]
        y_a, cast = _attention(qkv, n_heads, jobs)
        w_out_l, wgu2, wd2 = cast[:3]
        h = _outproj(h, y_a, y_bc, w_out_l)
        h = _ffn(h, ffn2_norm[l].reshape(1, d), wgu2, wd2, fg, final=(l == depth - 1))
        if l + 1 < depth:
            w_in_l, wgu1, wd1 = cast[3:]
    return h.reshape(b, s, d)
```

```python
import functools
import math
from typing import Callable, NamedTuple

import numpy as np
import jax
import jax.numpy as jnp
from jax import lax
from jax.experimental import pallas as pl
from jax.experimental.pallas import tpu as pltpu

F32 = jnp.float32
BF16 = jnp.bfloat16

HEAD_DIM = 128
LANES = 128
BF16_ROWS = 16
SGU_GROUPS = 4
SGU_CHUNK = 128
POOL_WINDOWS = (2, 4, 8, 16)
GROUP_DIM = 128
EPS = 1e-6
FFN_RES_WEIGHT = 0.5
assert math.frexp(FFN_RES_WEIGHT)[0] == 0.5

V7X_VMEM_BYTES = 64 * 1024 * 1024
VMEM_LIMIT = V7X_VMEM_BYTES - 6 * 1024 * 1024

FFN_ROWS = 1024
FFN_COLS = 512
PROJ_ROWS = 512
QKV_COLS = 1024
CAST_ROWS = 512
CAST_COLS = 512
ATTN_Q = 256
ATTN_HEADS = 2
LOG2_E = 1.4426950408889634
LOG2_ZERO_BOUND = -151.0


def _rmsnorm(x, g):
    inv = lax.rsqrt(jnp.mean(x * x, axis=-1, keepdims=True) + EPS)
    return x * inv * g


def _dot(a, b):
    return jnp.dot(a, b, preferred_element_type=F32)


def _ffn_kernel(x_ref, g_ref, wgu_ref, wd_ref, fg_ref, o_ref, xn_ref, *, nj, final):
    j = pl.program_id(1)
    tf = wd_ref.shape[0]

    @pl.when(j == 0)
    def _():
        xn_ref[...] = _rmsnorm(x_ref[...], g_ref[...]).astype(BF16)

    def down_proj():
        gu = _dot(xn_ref[...], wgu_ref[...])
        gate = gu[:, :tf]
        h = (gate * jax.nn.sigmoid(gate) * gu[:, tf:]).astype(BF16)
        return _dot(h, wd_ref[...])

    @pl.when(j == 0)
    def _():
        o_ref[...] = x_ref[...] + down_proj()

    @pl.when(j > 0)
    def _():
        o_ref[...] += down_proj()

    if final:
        @pl.when(j == nj - 1)
        def _():
            o_ref[...] = _rmsnorm(o_ref[...], fg_ref[...])


def _ffn(x, g, w_gu_p, w_down_p, fg, *, final):
    s, d = x.shape
    nj = w_gu_p.shape[0]
    assert w_gu_p.shape == (nj, d, 2 * FFN_COLS) and w_down_p.shape == (nj * FFN_COLS, d)
    tm = min(FFN_ROWS, s)
    ni = s // tm
    x_index = lambda i, j: (jnp.minimum(i + jnp.minimum(j, 1), ni - 1), 0)
    return pl.pallas_call(
        functools.partial(_ffn_kernel, nj=nj, final=final),
        grid=(ni, nj),
        in_specs=[
            pl.BlockSpec((tm, d), x_index),
            pl.BlockSpec((1, d), lambda i, j: (0, 0)),
            pl.BlockSpec((None, d, 2 * FFN_COLS), lambda i, j: (j, 0, 0)),
            pl.BlockSpec((FFN_COLS, d), lambda i, j: (j, 0)),
            pl.BlockSpec((1, d), lambda i, j: (0, 0)),
        ],
        out_specs=pl.BlockSpec((tm, d), lambda i, j: (i, 0)),
        out_shape=jax.ShapeDtypeStruct((s, d), F32),
        scratch_shapes=[pltpu.VMEM((tm, d), BF16)],
        compiler_params=pltpu.CompilerParams(
            dimension_semantics=("arbitrary", "arbitrary"), vmem_limit_bytes=VMEM_LIMIT),
        name="ffn",
    )(x, g, w_gu_p, w_down_p, fg)


def _inproj_kernel(x_ref, g_ref, w_ref, sn_ref, sw_ref, sb_ref, band_ref, pw_ref, ps_ref,
                   qkv_ref, o_ref, ph_ref, plo_ref, *, tm, attn_width, scale):
    i = pl.program_id(0)
    ch = SGU_CHUNK
    width = SGU_GROUPS * GROUP_DIM
    xn = _rmsnorm(x_ref[...], g_ref[...]).astype(BF16)
    proj = _dot(xn, w_ref[:, 3 * attn_width:])
    u = jax.nn.gelu(proj[:, :width])
    v = jax.nn.gelu(proj[:, width:2 * width])
    p = proj[:, 2 * width:]

    @pl.when(i == 0)
    def _():
        ph_ref[0:ch, :] = jnp.zeros((ch, width), BF16)
        plo_ref[0:ch, :] = jnp.zeros((ch, width), BF16)

    p_hi = p.astype(BF16)
    ph_ref[ch:ch + tm, :] = p_hi
    plo_ref[ch:ch + tm, :] = (p - p_hi.astype(F32)).astype(BF16)

    row = lax.broadcasted_iota(jnp.int32, (ch, ch), 0)
    col = lax.broadcasted_iota(jnp.int32, (ch, ch), 1)
    tril = col <= row
    units = [(grp, c) for grp in range(SGU_GROUPS) for c in range(tm // ch)]
    lanes = lambda grp: slice(grp * GROUP_DIM, (grp + 1) * GROUP_DIM)
    rows = lambda c: slice(c * ch, (c + 1) * ch)
    vg = [_rmsnorm(v[:, lanes(grp)], sn_ref[:, lanes(grp)]).astype(BF16) for grp in range(SGU_GROUPS)]
    w_causal = [jnp.where(tril, sw_ref[grp], 0.0).astype(BF16) for grp in range(SGU_GROUPS)]
    mixed = {}
    for grp in range(SGU_GROUPS):
        for c in range(0, tm // ch, 2):
            pair = _dot(w_causal[grp], jnp.concatenate([vg[grp][rows(c)], vg[grp][rows(c + 1)]], axis=1))
            mixed[grp, c], mixed[grp, c + 1] = pair[:, :GROUP_DIM], pair[:, GROUP_DIM:]
    wsum = {}
    for grp, c in units:
        ext = slice(c * ch, (c + 2) * ch)
        pair = _dot(band_ref[grp], jnp.concatenate([ph_ref[ext, lanes(grp)], plo_ref[ext, lanes(grp)]], axis=1))
        wsum[grp, c] = pair[:, :GROUP_DIM] + pair[:, GROUP_DIM:]
    for grp, c in units:
        o_ref[rows(c), lanes(grp)] = (u[rows(c), lanes(grp)] * (mixed[grp, c] + sb_ref[:, lanes(grp)])).astype(BF16)
    dlt = {}
    for grp, c in units:
        pos = i * tm + c * ch + row
        count = jnp.minimum(pos + 1, POOL_WINDOWS[grp]).astype(F32)
        dlt[grp, c] = (wsum[grp, c] / count - p[rows(c), lanes(grp)]).astype(BF16)
    for grp, c in units:
        y = _dot(dlt[grp, c], pw_ref[grp]) * ps_ref[:, lanes(grp)]
        o_ref[rows(c), width + grp * GROUP_DIM:width + (grp + 1) * GROUP_DIM] = y.astype(BF16)

    ph_ref[0:ch, :] = ph_ref[tm:tm + ch, :]
    plo_ref[0:ch, :] = plo_ref[tm:tm + ch, :]

    for c in range(3 * attn_width // QKV_COLS):
        cols = slice(c * QKV_COLS, (c + 1) * QKV_COLS)
        acc = _dot(xn, w_ref[:, cols])
        if (c + 1) * QKV_COLS <= attn_width:
            acc = acc * scale
        qkv_ref[:, cols] = acc.astype(BF16)


def _pool_bands():
    t = np.arange(SGU_CHUNK)[:, None] + SGU_CHUNK
    j = np.arange(2 * SGU_CHUNK)[None, :]
    bands = [((t - j >= 0) & (t - j < w)).astype(np.float32) for w in POOL_WINDOWS]
    return jnp.asarray(np.stack(bands), dtype=BF16)


def _inproj(x, g, w_in, attn_width, sgu_norm, sgu_w, sgu_bias, pool_w, pool_scale):
    s, d = x.shape
    width = SGU_GROUPS * GROUP_DIM
    n_in = w_in.shape[1]
    assert n_in == 3 * attn_width + 3 * width and attn_width % QKV_COLS == 0
    tm = min(PROJ_ROWS, s)
    full = lambda shape: pl.BlockSpec(shape, lambda i: (0,) * len(shape))
    return pl.pallas_call(
        functools.partial(_inproj_kernel, tm=tm, attn_width=attn_width,
                          scale=LOG2_E / math.sqrt(HEAD_DIM)),
        grid=(s // tm,),
        in_specs=[
            pl.BlockSpec((tm, d), lambda i: (i, 0)),
            full((1, d)),
            pl.BlockSpec((d, n_in), lambda i: (0, 0), pipeline_mode=pl.Buffered(1)),
            full((1, width)),
            full((SGU_GROUPS, SGU_CHUNK, SGU_CHUNK)),
            full((SGU_CHUNK, width)),
            full((SGU_GROUPS, SGU_CHUNK, 2 * SGU_CHUNK)),
            full((SGU_GROUPS, GROUP_DIM, GROUP_DIM)),
            full((1, width)),
        ],
        out_specs=[pl.BlockSpec((tm, 3 * attn_width), lambda i: (i, 0)),
                   pl.BlockSpec((tm, 2 * width), lambda i: (i, 0))],
        out_shape=[jax.ShapeDtypeStruct((s, 3 * attn_width), BF16),
                   jax.ShapeDtypeStruct((s, 2 * width), BF16)],
        scratch_shapes=[pltpu.VMEM((tm + SGU_CHUNK, width), BF16),
                        pltpu.VMEM((tm + SGU_CHUNK, width), BF16)],
        compiler_params=pltpu.CompilerParams(
            dimension_semantics=("arbitrary",), vmem_limit_bytes=VMEM_LIMIT),
        name="in_proj",
    )(x, g, w_in, sgu_norm, sgu_w, sgu_bias, _pool_bands(), pool_w, pool_scale)


def _attn_block(q, ks, vs, m_neg, r, causal):
    log_own, sp16, total = _attn_softplus(_attn_scores(q, ks), causal)
    a = _attn_weights(log_own, _dot(sp16, m_neg), r, causal)
    return _dot(a, vs), (total if r is None else r + total)


def _attn_scores(q, ks):
    return lax.dot_general(q, ks, (((1,), (1,)), ((), ())), preferred_element_type=F32)


def _attn_softplus(z2, causal):
    neg_abs = pltpu.bitcast(pltpu.bitcast(z2, jnp.uint32) | jnp.uint32(0x80000000), F32)
    sp = jnp.maximum(z2, 0.0) + jnp.log(1.0 + jnp.exp2(neg_abs)) * LOG2_E
    if causal is not None:
        sp = jnp.where(causal, sp, 0.0)
    total = jnp.broadcast_to(-jnp.sum(sp, axis=1, keepdims=True), (z2.shape[0], HEAD_DIM))
    return z2 - sp, sp.astype(BF16), total


def _attn_weights(log_own, later, r, causal):
    log_a = log_own + later
    if r is not None:
        log_a = log_a + jnp.concatenate([r] * (log_own.shape[1] // HEAD_DIM), axis=1)
    a = jnp.exp2(log_a)
    if causal is not None:
        a = jnp.where(causal, a, 0.0)
    return a.astype(BF16)


def _attn_kernel(q_ref, k_ref, v_ref, m_ref, o_ref, r_ref, acc_ref, *, tq):
    i = pl.program_id(1)
    m_neg = m_ref[...]
    heads = r_ref.shape[0]
    row = lax.broadcasted_iota(jnp.int32, (tq, tq), 0)
    col = lax.broadcasted_iota(jnp.int32, (tq, tq), 1)
    causal = col < row
    diag = pl.multiple_of(i * tq, tq)
    prev = pl.multiple_of(jnp.maximum(i - 1, 0) * tq, tq)
    no_prev = jnp.where(i == 0, jnp.float32(-1e30), jnp.float32(0.0))

    rmax0 = None
    for h in range(heads):
        hs = slice(h * HEAD_DIM, (h + 1) * HEAD_DIM)
        q = q_ref[:, hs]
        pv0, r = _attn_block(q, k_ref[pl.ds(diag, tq), hs], v_ref[pl.ds(diag, tq), hs], m_neg, None, causal)
        pv1, r = _attn_block(q, k_ref[pl.ds(prev, tq), hs], v_ref[pl.ds(prev, tq), hs], m_neg,
                             r + no_prev, None)
        acc_ref[h] = pv0 + pv1
        r_ref[h] = r
        hmax = jnp.max(r)
        rmax0 = hmax if rmax0 is None else jnp.maximum(rmax0, hmax)

    def cond(carry):
        j, rmax = carry
        return jnp.logical_and(j >= 0, rmax > LOG2_ZERO_BOUND)

    def body(carry):
        j, _ = carry
        start = pl.multiple_of(j * tq, tq)
        rmax = None
        for h in range(heads):
            hs = slice(h * HEAD_DIM, (h + 1) * HEAD_DIM)
            pv, r = _attn_block(q_ref[:, hs], k_ref[pl.ds(start, tq), hs], v_ref[pl.ds(start, tq), hs],
                                m_neg, r_ref[h], None)
            acc_ref[h] += pv
            r_ref[h] = r
            hmax = jnp.max(r)
            rmax = hmax if rmax is None else jnp.maximum(rmax, hmax)
        return j - 1, rmax

    lax.while_loop(cond, body, (i - 2, rmax0))
    for h in range(heads):
        o_ref[:, h * HEAD_DIM:(h + 1) * HEAD_DIM] = acc_ref[h].astype(BF16)


def _cumsum_matrix(ch):
    j = np.arange(ch)[:, None]
    s = np.arange(ch)[None, :]
    return jnp.asarray(-(j > s).astype(np.float32), dtype=BF16)


def _attn_near_kernel(q_ref, kd_ref, kp_ref, vd_ref, vp_ref, m_ref, x_ref, ybc_ref, wout_ref, *rest,
                      tq, n_heads, cast_bodies):
    n_cast = len(cast_bodies)
    cast_in, (o_ref, rmax_ref), cast_out = rest[:n_cast], rest[n_cast:n_cast + 2], rest[n_cast + 2:]
    attn_width = n_heads * HEAD_DIM
    i = pl.program_id(0)
    for body, w_ref, wo_ref in zip(cast_bodies, cast_in, cast_out, strict=True):
        body(i, w_ref, wo_ref)
    m_neg = m_ref[...]
    row = lax.broadcasted_iota(jnp.int32, (tq, tq), 0)
    col = lax.broadcasted_iota(jnp.int32, (tq, tq), 1)
    causal = col < row
    no_prev = jnp.where(i == 0, jnp.float32(-1e30), jnp.float32(0.0))
    heads = [slice(h * HEAD_DIM, (h + 1) * HEAD_DIM) for h in range(n_heads)]
    z_d, z_p, sp_d, sp_p, from_d, from_p, a_d, a_p, y_a = ({} for _ in range(9))
    rmax = []
    out = [x_ref[...]]
    pair_width = 2 * HEAD_DIM

    def stage(s, h):
        hs = heads[h]
        if s == 0:
            z_d[h] = _attn_scores(q_ref[:, hs], kd_ref[:, hs])
            z_p[h] = _attn_scores(q_ref[:, hs], kp_ref[:, hs])
        elif s == 1:
            sp_d[h] = _attn_softplus(z_d[h], causal)
            sp_p[h] = _attn_softplus(z_p[h], None)
        elif s == 2:
            from_d[h] = _dot(sp_d[h][1], m_neg)
            from_p[h] = _dot(sp_p[h][1], m_neg)
        elif s == 3:
            a_d[h] = _attn_weights(sp_d[h][0], from_d[h], None, causal)
            a_p[h] = _attn_weights(sp_p[h][0], from_p[h], sp_d[h][2] + no_prev, None)
        else:
            y_a[h] = (_dot(a_d[h], vd_ref[:, hs]) + _dot(a_p[h], vp_ref[:, hs])).astype(BF16)
            rmax.append(sp_d[h][2] + no_prev + sp_p[h][2])
            if h % 2 == 1:
                lhs, k0 = jnp.concatenate([y_a[h - 1], y_a[h]], axis=1), (h - 1) * HEAD_DIM
            else:
                lhs, k0 = ybc_ref[:, h * HEAD_DIM:h * HEAD_DIM + pair_width], attn_width + h * HEAD_DIM
            out[0] = _dot(lhs, wout_ref[k0:k0 + pair_width, :]) + out[0]

    for t in range(n_heads + 4):
        for h in range(n_heads):
            if 0 <= t - h < 5:
                stage(t - h, h)
    o_ref[...] = out[0]
    rm = functools.reduce(jnp.maximum, rmax)
    rm = jnp.max(rm.reshape(tq // 8, 8, HEAD_DIM), axis=0)
    rmax_ref[0] = jnp.where(i <= 1, jnp.float32(-1e30), rm)


def _attention_near(qkv, x, y_bc, w_out, n_heads, cast_jobs):
    s, d = x.shape
    tq = ATTN_Q
    width = n_heads * HEAD_DIM
    assert s % tq == 0 and qkv.shape == (s, 3 * width) and n_heads % 2 == 0
    assert y_bc.shape == (s, width) and w_out.shape == (2 * width, d)
    nq = s // tq
    prev = lambda i: jnp.maximum(i - 1, 0)
    jobs = [make(nq) for make in cast_jobs]
    outs = pl.pallas_call(
        functools.partial(_attn_near_kernel, tq=tq, n_heads=n_heads,
                          cast_bodies=tuple(job.body for job in jobs)),
        grid=(nq,),
        in_specs=[
            pl.BlockSpec((tq, width), lambda i: (i, 0)),
            pl.BlockSpec((tq, width), lambda i: (i, 1)),
            pl.BlockSpec((tq, width), lambda i: (prev(i), 1)),
            pl.BlockSpec((tq, width), lambda i: (i, 2)),
            pl.BlockSpec((tq, width), lambda i: (prev(i), 2)),
            pl.BlockSpec((tq, tq), lambda i: (0, 0)),
            pl.BlockSpec((tq, d), lambda i: (i, 0)),
            pl.BlockSpec((tq, width), lambda i: (i, 0)),
            pl.BlockSpec((2 * width, d), lambda i: (0, 0), pipeline_mode=pl.Buffered(1)),
        ] + [job.in_spec for job in jobs],
        out_specs=[pl.BlockSpec((tq, d), lambda i: (i, 0)),
                   pl.BlockSpec((1, 8, HEAD_DIM), lambda i: (i, 0, 0))] + [job.out_spec for job in jobs],
        out_shape=[jax.ShapeDtypeStruct((s, d), F32),
                   jax.ShapeDtypeStruct((nq, 8, HEAD_DIM), F32)] + [job.out_shape for job in jobs],
        compiler_params=pltpu.CompilerParams(
            dimension_semantics=("arbitrary",), vmem_limit_bytes=VMEM_LIMIT),
        name="stickbreak_attn_near",
    )(qkv, qkv, qkv, qkv, qkv, _cumsum_matrix(tq), x, y_bc, w_out, *[job.w for job in jobs])
    return outs[0], outs[1], outs[2:]


def _mix_and_project(qkv, x, y_bc, w_out, n_heads, cast_jobs):
    o_near, rmax, cast = _attention_near(qkv, x, y_bc, w_out, n_heads, cast_jobs)
    need_far = jnp.max(rmax) > LOG2_ZERO_BOUND
    o = lax.cond(need_far,
                 lambda qkv_, x_, y_bc_, w_out_, o_: _outproj(x_, _attention_walk(qkv_, n_heads), y_bc_, w_out_),
                 lambda qkv_, x_, y_bc_, w_out_, o_: o_,
                 qkv, x, y_bc, w_out, o_near)
    return o, cast


def _attention_walk(qkv, n_heads):
    s = qkv.shape[0]
    tq = ATTN_Q
    assert s % tq == 0 and s >= 2 * tq and n_heads % ATTN_HEADS == 0
    width = ATTN_HEADS * HEAD_DIM
    groups = n_heads // ATTN_HEADS
    return pl.pallas_call(
        functools.partial(_attn_kernel, tq=tq),
        grid=(groups, s // tq),
        in_specs=[
            pl.BlockSpec((tq, width), lambda g, i: (i, g)),
            pl.BlockSpec((s, width), lambda g, i: (0, groups + g)),
            pl.BlockSpec((s, width), lambda g, i: (0, 2 * groups + g)),
            pl.BlockSpec((tq, tq), lambda g, i: (0, 0)),
        ],
        out_specs=pl.BlockSpec((tq, width), lambda g, i: (i, g)),
        out_shape=jax.ShapeDtypeStruct((s, n_heads * HEAD_DIM), BF16),
        scratch_shapes=[pltpu.VMEM((ATTN_HEADS, tq, HEAD_DIM), F32),
                        pltpu.VMEM((ATTN_HEADS, tq, HEAD_DIM), F32)],
        compiler_params=pltpu.CompilerParams(
            dimension_semantics=("parallel", "arbitrary"), vmem_limit_bytes=VMEM_LIMIT),
        name="stickbreak_attn",
    )(qkv, qkv, qkv, _cumsum_matrix(tq))


def _outproj_kernel(x_ref, ya_ref, ybc_ref, wa_ref, wb_ref, o_ref):
    o_ref[...] = x_ref[...] + _dot(ya_ref[...], wa_ref[...]) + _dot(ybc_ref[...], wb_ref[...])


def _outproj(x, y_a, y_bc, w_out):
    s, d = x.shape
    ka = y_a.shape[1]
    kb = y_bc.shape[1]
    assert ka == kb
    tm = min(PROJ_ROWS, s)
    return pl.pallas_call(
        _outproj_kernel,
        grid=(s // tm,),
        in_specs=[
            pl.BlockSpec((tm, d), lambda i: (i, 0)),
            pl.BlockSpec((tm, ka), lambda i: (i, 0)),
            pl.BlockSpec((tm, kb), lambda i: (i, 0)),
            pl.BlockSpec((ka, d), lambda i: (0, 0)),
            pl.BlockSpec((kb, d), lambda i: (1, 0)),
        ],
        out_specs=pl.BlockSpec((tm, d), lambda i: (i, 0)),
        out_shape=jax.ShapeDtypeStruct((s, d), F32),
        compiler_params=pltpu.CompilerParams(
            dimension_semantics=("parallel",), vmem_limit_bytes=VMEM_LIMIT),
        name="out_proj",
    )(x, y_a, y_bc, w_out, w_out)


def _cast_kernel(x_ref, o_ref):
    o_ref[...] = x_ref[...].astype(BF16)


def _cast_gate_up_kernel(x_ref, o_ref, *, d_ff):
    nj, rows, two_tf = o_ref.shape
    tf = two_tf // 2
    for j in range(nj):
        n = min(tf, d_ff - j * tf)
        o_ref[j, :, :n] = x_ref[:, j * tf:j * tf + n].astype(BF16)
        o_ref[j, :, tf:tf + n] = x_ref[:, d_ff + j * tf:d_ff + j * tf + n].astype(BF16)
        if n < tf:
            zeros = jnp.zeros((rows, tf - n), BF16)
            o_ref[j, :, n:tf] = zeros
            o_ref[j, :, tf + n:] = zeros


def _cast_down_kernel(x_ref, o_ref, *, d_ff):
    o_ref[:d_ff, :] = (x_ref[...] * FFN_RES_WEIGHT).astype(BF16)
    if o_ref.shape[0] > d_ff:
        o_ref[d_ff:, :] = jnp.zeros((o_ref.shape[0] - d_ff, o_ref.shape[1]), BF16)


def _cast_call(body, w, layer, in_block, out_block, out_shape, grid, index, name, out_index=None):
    return pl.pallas_call(
        body,
        grid=(grid,),
        in_specs=[pl.BlockSpec((None,) + in_block, lambda i: (layer,) + index(i))],
        out_specs=pl.BlockSpec(out_block, out_index or index),
        out_shape=jax.ShapeDtypeStruct(out_shape, BF16),
        compiler_params=pltpu.CompilerParams(
            dimension_semantics=("parallel",), vmem_limit_bytes=VMEM_LIMIT),
        name=name,
    )(w)


def _cast_plain(w, layer):
    _, rows, cols = w.shape
    rb = CAST_ROWS
    assert rows % rb == 0
    return _cast_call(_cast_kernel, w, layer, (rb, cols), (rb, cols), (rows, cols), rows // rb,
                      lambda i: (i, 0), "cast_bf16")


def _cast_gate_up(w_gu, layer):
    _, d, two_ff = w_gu.shape
    d_ff = two_ff // 2
    nj = (d_ff + FFN_COLS - 1) // FFN_COLS
    rb = CAST_ROWS // 2
    assert d % rb == 0
    return _cast_call(functools.partial(_cast_gate_up_kernel, d_ff=d_ff), w_gu, layer,
                      (rb, two_ff), (nj, rb, 2 * FFN_COLS), (nj, d, 2 * FFN_COLS), d // rb,
                      lambda i: (i, 0), "cast_gate_up", out_index=lambda i: (0, i, 0))


def _cast_down(w_down, layer):
    _, d_ff, d = w_down.shape
    d_ffp = d_ff + (-d_ff) % FFN_COLS
    assert d % CAST_COLS == 0
    return _cast_call(functools.partial(_cast_down_kernel, d_ff=d_ff), w_down, layer,
                      (d_ff, CAST_COLS), (d_ffp, CAST_COLS), (d_ffp, d), d // CAST_COLS,
                      lambda i: (0, i), "cast_down")


class _CastJob(NamedTuple):
    w: jax.Array
    in_spec: pl.BlockSpec
    out_spec: pl.BlockSpec
    out_shape: jax.ShapeDtypeStruct
    body: Callable


def _gate_up_job(w_gu, layer, n):
    _, d, two_ff = w_gu.shape
    d_ff = two_ff // 2
    nj = (d_ff + FFN_COLS - 1) // FFN_COLS
    rb = d // n
    assert d % n == 0 and rb % BF16_ROWS == 0
    return _CastJob(w_gu, pl.BlockSpec((None, rb, two_ff), lambda i: (layer, i, 0)),
                    pl.BlockSpec((nj, rb, 2 * FFN_COLS), lambda i: (0, i, 0)),
                    jax.ShapeDtypeStruct((nj, d, 2 * FFN_COLS), BF16),
                    lambda i, w_ref, o_ref: _cast_gate_up_kernel(w_ref, o_ref, d_ff=d_ff))


def _plain_job(w, layer, n):
    _, rows, cols = w.shape
    rb = rows // n
    assert rows % n == 0 and rb % BF16_ROWS == 0
    return _CastJob(w, pl.BlockSpec((None, rb, cols), lambda i: (layer, i, 0)),
                    pl.BlockSpec((rb, cols), lambda i: (i, 0)),
                    jax.ShapeDtypeStruct((rows, cols), BF16),
                    lambda i, w_ref, o_ref: _cast_kernel(w_ref, o_ref))


def _cast_down_rows_kernel(i, x_ref, o_ref, *, n_valid, n_blocks):
    if n_blocks == n_valid:
        o_ref[...] = (x_ref[...] * FFN_RES_WEIGHT).astype(BF16)
        return

    @pl.when(i < n_valid)
    def _():
        o_ref[...] = (x_ref[...] * FFN_RES_WEIGHT).astype(BF16)

    @pl.when(i >= n_valid)
    def _():
        o_ref[...] = jnp.zeros_like(o_ref)


def _down_job(w_down, layer, n):
    _, d_ff, d = w_down.shape
    d_ffp = d_ff + (-d_ff) % FFN_COLS
    rb = math.gcd(d_ff, d_ffp)
    n_valid, n_blocks = d_ff // rb, d_ffp // rb
    assert rb % BF16_ROWS == 0 and n_blocks <= n
    return _CastJob(w_down, pl.BlockSpec((None, rb, d), lambda i: (layer, jnp.minimum(i, n_valid - 1), 0)),
                    pl.BlockSpec((rb, d), lambda i: (jnp.minimum(i, n_blocks - 1), 0)),
                    jax.ShapeDtypeStruct((d_ffp, d), BF16),
                    functools.partial(_cast_down_rows_kernel, n_valid=n_valid, n_blocks=n_blocks))


def kernel(x, ffn1_norm, ffn1_w_gu, ffn1_w_down, mix_norm, w_in, sgu_norm, sgu_w, sgu_b, pool_w,
           pool_scale, w_out, ffn2_norm, ffn2_w_gu, ffn2_w_down, final_norm):
    b, s, d = x.shape
    assert b == 1
    depth = w_in.shape[0]
    attn_width = w_out.shape[1] // 2
    n_heads = attn_width // HEAD_DIM
    width = SGU_GROUPS * GROUP_DIM
    h = x.reshape(s, d)
    fg = final_norm.reshape(1, d)
    wgu1 = _cast_gate_up(ffn1_w_gu, 0)
    wd1 = _cast_down(ffn1_w_down, 0)
    w_in_l = _cast_plain(w_in, 0)
    w_out_l = _cast_plain(w_out, 0)
    for l in range(depth):
        mix_g = mix_norm[l].reshape(1, d)
        sgu_bias = jnp.repeat(sgu_b[l].T, GROUP_DIM, axis=1)

        h = _ffn(h, ffn1_norm[l].reshape(1, d), wgu1, wd1, fg, final=False)
        qkv, y_bc = _inproj(h, mix_g, w_in_l, attn_width, sgu_norm[l].reshape(1, width),
                            sgu_w[l], sgu_bias, pool_w[l].astype(BF16), pool_scale[l].reshape(1, width))
        jobs = [functools.partial(_gate_up_job, ffn2_w_gu, l),
                functools.partial(_down_job, ffn2_w_down, l)]
        if l + 1 < depth:
            jobs += [functools.partial(_plain_job, w_in, l + 1),
                     functools.partial(_plain_job, w_out, l + 1),
                     functools.partial(_gate_up_job, ffn1_w_gu, l + 1),
                     functools.partial(_down_job, ffn1_w_down, l + 1)]
        h, cast = _mix_and_project(qkv, h, y_bc, w_out_l, n_heads, jobs)
        wgu2, wd2 = cast[:2]
        h = _ffn(h, ffn2_norm[l].reshape(1, d), wgu2, wd2, fg, final=(l == depth - 1))
        if l + 1 < depth:
            w_in_l, w_out_l, wgu1, wd1 = cast[2:]
    return h.reshape(b, s, d)
```

```python
import functools
import math
from typing import Callable, NamedTuple

import numpy as np
import jax
import jax.numpy as jnp
from jax import lax
from jax.experimental import pallas as pl
from jax.experimental.pallas import tpu as pltpu

F32 = jnp.float32
BF16 = jnp.bfloat16

HEAD_DIM = 128
LANES = 128
SUBLANES = 8
BF16_ROWS = 16
SGU_GROUPS = 4
SGU_CHUNK = 128
POOL_WINDOWS = (2, 4, 8, 16)
GROUP_DIM = 128
EPS = 1e-6
FFN_RES_WEIGHT = 0.5
assert math.frexp(FFN_RES_WEIGHT)[0] == 0.5

V7X_VMEM_BYTES = 64 * 1024 * 1024
VMEM_LIMIT = V7X_VMEM_BYTES - 6 * 1024 * 1024

FFN_ROWS = 1024
FFN_COLS = 512
PROJ_ROWS = 512
QKV_COLS = 1024
CAST_ROWS = 512
CAST_COLS = 512
ATTN_Q = 256
ATTN_HEADS = 2
LOG2_E = 1.4426950408889634
LOG2_ZERO_BOUND = -151.0


def _rmsnorm(x, g):
    inv = lax.rsqrt(jnp.mean(x * x, axis=-1, keepdims=True) + EPS)
    return x * inv * g


def _dot(a, b):
    return jnp.dot(a, b, preferred_element_type=F32)


def _ffn_kernel(x_ref, g_ref, wgu_ref, wd_ref, fg_ref, o_ref, xn_ref, *, nj, final):
    j = pl.program_id(1)
    tf = wd_ref.shape[0]

    @pl.when(j == 0)
    def _():
        xn_ref[...] = _rmsnorm(x_ref[...], g_ref[...]).astype(BF16)

    def down_proj():
        gu = _dot(xn_ref[...], wgu_ref[...])
        gate = gu[:, :tf]
        h = (gate * jax.nn.sigmoid(gate) * gu[:, tf:]).astype(BF16)
        return _dot(h, wd_ref[...])

    @pl.when(j == 0)
    def _():
        o_ref[...] = x_ref[...] + down_proj()

    @pl.when(j > 0)
    def _():
        o_ref[...] += down_proj()

    if final:
        @pl.when(j == nj - 1)
        def _():
            o_ref[...] = _rmsnorm(o_ref[...], fg_ref[...])


def _ffn(x, g, w_gu_p, w_down_p, fg, *, final):
    s, d = x.shape
    nj = w_gu_p.shape[0]
    assert w_gu_p.shape == (nj, d, 2 * FFN_COLS) and w_down_p.shape == (nj * FFN_COLS, d)
    tm = min(FFN_ROWS, s)
    ni = s // tm
    x_index = lambda i, j: (jnp.minimum(i + jnp.minimum(j, 1), ni - 1), 0)
    return pl.pallas_call(
        functools.partial(_ffn_kernel, nj=nj, final=final),
        grid=(ni, nj),
        in_specs=[
            pl.BlockSpec((tm, d), x_index),
            pl.BlockSpec((1, d), lambda i, j: (0, 0)),
            pl.BlockSpec((None, d, 2 * FFN_COLS), lambda i, j: (j, 0, 0)),
            pl.BlockSpec((FFN_COLS, d), lambda i, j: (j, 0)),
            pl.BlockSpec((1, d), lambda i, j: (0, 0)),
        ],
        out_specs=pl.BlockSpec((tm, d), lambda i, j: (i, 0)),
        out_shape=jax.ShapeDtypeStruct((s, d), F32),
        scratch_shapes=[pltpu.VMEM((tm, d), BF16)],
        compiler_params=pltpu.CompilerParams(
            dimension_semantics=("arbitrary", "arbitrary"), vmem_limit_bytes=VMEM_LIMIT),
        name="ffn",
    )(x, g, w_gu_p, w_down_p, fg)


def _inproj_kernel(x_ref, g_ref, w_ref, sn_ref, sw_ref, sb_ref, band_ref, pw_ref, ps_ref,
                   qkv_ref, o_ref, ph_ref, plo_ref, *, tm, attn_width, scale):
    i = pl.program_id(0)
    ch = SGU_CHUNK
    width = SGU_GROUPS * GROUP_DIM
    xn = _rmsnorm(x_ref[...], g_ref[...]).astype(BF16)
    proj = _dot(xn, w_ref[:, 3 * attn_width:])
    u = jax.nn.gelu(proj[:, :width])
    v = jax.nn.gelu(proj[:, width:2 * width])
    p = proj[:, 2 * width:]

    @pl.when(i == 0)
    def _():
        ph_ref[0:ch, :] = jnp.zeros((ch, width), BF16)
        plo_ref[0:ch, :] = jnp.zeros((ch, width), BF16)

    p_hi = p.astype(BF16)
    ph_ref[ch:ch + tm, :] = p_hi
    plo_ref[ch:ch + tm, :] = (p - p_hi.astype(F32)).astype(BF16)

    row = lax.broadcasted_iota(jnp.int32, (ch, ch), 0)
    col = lax.broadcasted_iota(jnp.int32, (ch, ch), 1)
    tril = col <= row
    units = [(grp, c) for grp in range(SGU_GROUPS) for c in range(tm // ch)]
    lanes = lambda grp: slice(grp * GROUP_DIM, (grp + 1) * GROUP_DIM)
    rows = lambda c: slice(c * ch, (c + 1) * ch)
    vg = [_rmsnorm(v[:, lanes(grp)], sn_ref[:, lanes(grp)]).astype(BF16) for grp in range(SGU_GROUPS)]
    w_causal = [jnp.where(tril, sw_ref[grp], 0.0).astype(BF16) for grp in range(SGU_GROUPS)]
    mixed = {}
    for grp in range(SGU_GROUPS):
        for c in range(0, tm // ch, 2):
            pair = _dot(w_causal[grp], jnp.concatenate([vg[grp][rows(c)], vg[grp][rows(c + 1)]], axis=1))
            mixed[grp, c], mixed[grp, c + 1] = pair[:, :GROUP_DIM], pair[:, GROUP_DIM:]
    wsum = {}
    for grp, c in units:
        ext = slice(c * ch, (c + 2) * ch)
        pair = _dot(band_ref[grp], jnp.concatenate([ph_ref[ext, lanes(grp)], plo_ref[ext, lanes(grp)]], axis=1))
        wsum[grp, c] = pair[:, :GROUP_DIM] + pair[:, GROUP_DIM:]
    for grp, c in units:
        o_ref[rows(c), lanes(grp)] = (u[rows(c), lanes(grp)] * (mixed[grp, c] + sb_ref[:, lanes(grp)])).astype(BF16)
    dlt = {}
    for grp, c in units:
        pos = i * tm + c * ch + row
        count = jnp.minimum(pos + 1, POOL_WINDOWS[grp]).astype(F32)
        dlt[grp, c] = (wsum[grp, c] / count - p[rows(c), lanes(grp)]).astype(BF16)
    for grp, c in units:
        y = _dot(dlt[grp, c], pw_ref[grp]) * ps_ref[:, lanes(grp)]
        o_ref[rows(c), width + grp * GROUP_DIM:width + (grp + 1) * GROUP_DIM] = y.astype(BF16)

    ph_ref[0:ch, :] = ph_ref[tm:tm + ch, :]
    plo_ref[0:ch, :] = plo_ref[tm:tm + ch, :]

    for c in range(3 * attn_width // QKV_COLS):
        cols = slice(c * QKV_COLS, (c + 1) * QKV_COLS)
        acc = _dot(xn, w_ref[:, cols])
        if (c + 1) * QKV_COLS <= attn_width:
            acc = acc * scale
        qkv_ref[:, cols] = acc.astype(BF16)


def _pool_bands():
    t = np.arange(SGU_CHUNK)[:, None] + SGU_CHUNK
    j = np.arange(2 * SGU_CHUNK)[None, :]
    bands = [((t - j >= 0) & (t - j < w)).astype(np.float32) for w in POOL_WINDOWS]
    return jnp.asarray(np.stack(bands), dtype=BF16)


def _inproj(x, g, w_in, attn_width, sgu_norm, sgu_w, sgu_bias, pool_w, pool_scale):
    s, d = x.shape
    width = SGU_GROUPS * GROUP_DIM
    n_in = w_in.shape[1]
    assert n_in == 3 * attn_width + 3 * width and attn_width % QKV_COLS == 0
    tm = min(PROJ_ROWS, s)
    full = lambda shape: pl.BlockSpec(shape, lambda i: (0,) * len(shape))
    return pl.pallas_call(
        functools.partial(_inproj_kernel, tm=tm, attn_width=attn_width,
                          scale=LOG2_E / math.sqrt(HEAD_DIM)),
        grid=(s // tm,),
        in_specs=[
            pl.BlockSpec((tm, d), lambda i: (i, 0)),
            full((1, d)),
            pl.BlockSpec((d, n_in), lambda i: (0, 0), pipeline_mode=pl.Buffered(1)),
            full((1, width)),
            full((SGU_GROUPS, SGU_CHUNK, SGU_CHUNK)),
            full((SGU_CHUNK, width)),
            full((SGU_GROUPS, SGU_CHUNK, 2 * SGU_CHUNK)),
            full((SGU_GROUPS, GROUP_DIM, GROUP_DIM)),
            full((1, width)),
        ],
        out_specs=[pl.BlockSpec((tm, 3 * attn_width), lambda i: (i, 0)),
                   pl.BlockSpec((tm, 2 * width), lambda i: (i, 0))],
        out_shape=[jax.ShapeDtypeStruct((s, 3 * attn_width), BF16),
                   jax.ShapeDtypeStruct((s, 2 * width), BF16)],
        scratch_shapes=[pltpu.VMEM((tm + SGU_CHUNK, width), BF16),
                        pltpu.VMEM((tm + SGU_CHUNK, width), BF16)],
        compiler_params=pltpu.CompilerParams(
            dimension_semantics=("arbitrary",), vmem_limit_bytes=VMEM_LIMIT),
        name="in_proj",
    )(x, g, w_in, sgu_norm, sgu_w, sgu_bias, _pool_bands(), pool_w, pool_scale)


def _attn_block(q, ks, vs, m_neg, r, causal):
    log_own, sp16, total = _attn_softplus(_attn_scores(q, ks), causal)
    a = _attn_weights(log_own, _dot(sp16, m_neg), r, causal)
    return _dot(a, vs), (total if r is None else r + total)


def _attn_scores(q, ks):
    return lax.dot_general(q, ks, (((1,), (1,)), ((), ())), preferred_element_type=F32)


def _attn_softplus(z2, causal):
    neg_abs = pltpu.bitcast(pltpu.bitcast(z2, jnp.uint32) | jnp.uint32(0x80000000), F32)
    sp = jnp.maximum(z2, 0.0) + jnp.log(1.0 + jnp.exp2(neg_abs)) * LOG2_E
    if causal is not None:
        sp = jnp.where(causal, sp, 0.0)
    total = jnp.broadcast_to(-jnp.sum(sp, axis=1, keepdims=True), (z2.shape[0], HEAD_DIM))
    return z2 - sp, sp.astype(BF16), total


def _attn_weights(log_own, later, r, causal):
    log_a = log_own + later
    if r is not None:
        log_a = log_a + jnp.concatenate([r] * (log_own.shape[1] // HEAD_DIM), axis=1)
    a = jnp.exp2(log_a)
    if causal is not None:
        a = jnp.where(causal, a, 0.0)
    return a.astype(BF16)


def _attn_kernel(q_ref, k_ref, v_ref, m_ref, o_ref, r_ref, acc_ref, *, tq):
    i = pl.program_id(1)
    m_neg = m_ref[...]
    heads = r_ref.shape[0]
    row = lax.broadcasted_iota(jnp.int32, (tq, tq), 0)
    col = lax.broadcasted_iota(jnp.int32, (tq, tq), 1)
    causal = col < row
    diag = pl.multiple_of(i * tq, tq)
    prev = pl.multiple_of(jnp.maximum(i - 1, 0) * tq, tq)
    no_prev = jnp.where(i == 0, jnp.float32(-1e30), jnp.float32(0.0))

    rmax0 = None
    for h in range(heads):
        hs = slice(h * HEAD_DIM, (h + 1) * HEAD_DIM)
        q = q_ref[:, hs]
        pv0, r = _attn_block(q, k_ref[pl.ds(diag, tq), hs], v_ref[pl.ds(diag, tq), hs], m_neg, None, causal)
        pv1, r = _attn_block(q, k_ref[pl.ds(prev, tq), hs], v_ref[pl.ds(prev, tq), hs], m_neg,
                             r + no_prev, None)
        acc_ref[h] = pv0 + pv1
        r_ref[h] = r
        hmax = jnp.max(r)
        rmax0 = hmax if rmax0 is None else jnp.maximum(rmax0, hmax)

    def cond(carry):
        j, rmax = carry
        return jnp.logical_and(j >= 0, rmax > LOG2_ZERO_BOUND)

    def body(carry):
        j, _ = carry
        start = pl.multiple_of(j * tq, tq)
        rmax = None
        for h in range(heads):
            hs = slice(h * HEAD_DIM, (h + 1) * HEAD_DIM)
            pv, r = _attn_block(q_ref[:, hs], k_ref[pl.ds(start, tq), hs], v_ref[pl.ds(start, tq), hs],
                                m_neg, r_ref[h], None)
            acc_ref[h] += pv
            r_ref[h] = r
            hmax = jnp.max(r)
            rmax = hmax if rmax is None else jnp.maximum(rmax, hmax)
        return j - 1, rmax

    lax.while_loop(cond, body, (i - 2, rmax0))
    for h in range(heads):
        o_ref[:, h * HEAD_DIM:(h + 1) * HEAD_DIM] = acc_ref[h].astype(BF16)


def _cumsum_matrix(ch):
    j = np.arange(ch)[:, None]
    s = np.arange(ch)[None, :]
    return jnp.asarray(-(j > s).astype(np.float32), dtype=BF16)


def _attn_near_kernel(q_ref, kd_ref, kp_ref, vd_ref, vp_ref, m_ref, *rest, tq, n_heads, cast_bodies):
    n_cast = len(cast_bodies)
    cast_in, (o_ref, rmax_ref), cast_out = rest[:n_cast], rest[n_cast:n_cast + 2], rest[n_cast + 2:]
    i = pl.program_id(0)
    for body, w_ref, wo_ref in zip(cast_bodies, cast_in, cast_out, strict=True):
        body(i, w_ref, wo_ref)
    m_neg = m_ref[...]
    row = lax.broadcasted_iota(jnp.int32, (tq, tq), 0)
    col = lax.broadcasted_iota(jnp.int32, (tq, tq), 1)
    causal = col < row
    no_prev = jnp.where(i == 0, jnp.float32(-1e30), jnp.float32(0.0))
    heads = [slice(h * HEAD_DIM, (h + 1) * HEAD_DIM) for h in range(n_heads)]
    z_d, z_p, sp_d, sp_p, from_d, from_p, a_d, a_p = ({} for _ in range(8))
    rmax = []

    def stage(s, h):
        hs = heads[h]
        if s == 0:
            z_d[h] = _attn_scores(q_ref[:, hs], kd_ref[:, hs])
            z_p[h] = _attn_scores(q_ref[:, hs], kp_ref[:, hs])
        elif s == 1:
            sp_d[h] = _attn_softplus(z_d[h], causal)
            sp_p[h] = _attn_softplus(z_p[h], None)
        elif s == 2:
            from_d[h] = _dot(sp_d[h][1], m_neg)
            from_p[h] = _dot(sp_p[h][1], m_neg)
        elif s == 3:
            a_d[h] = _attn_weights(sp_d[h][0], from_d[h], None, causal)
            a_p[h] = _attn_weights(sp_p[h][0], from_p[h], sp_d[h][2] + no_prev, None)
        else:
            o_ref[:, hs] = (_dot(a_d[h], vd_ref[:, hs]) + _dot(a_p[h], vp_ref[:, hs])).astype(BF16)
            rmax.append(sp_d[h][2] + no_prev + sp_p[h][2])

    for t in range(n_heads + 4):
        for h in range(n_heads):
            if 0 <= t - h < 5:
                stage(t - h, h)
    rm = functools.reduce(jnp.maximum, rmax)
    rm = jnp.max(rm.reshape(tq // SUBLANES, SUBLANES, HEAD_DIM), axis=0)
    rmax_ref[0] = jnp.where(i <= 1, jnp.float32(-1e30), rm)


def _attention_near(qkv, n_heads, cast_jobs):
    s = qkv.shape[0]
    tq = ATTN_Q
    width = n_heads * HEAD_DIM
    assert s % tq == 0 and qkv.shape[1] == 3 * width
    nq = s // tq
    prev = lambda i: jnp.maximum(i - 1, 0)
    jobs = [make(nq) for make in cast_jobs]
    outs = pl.pallas_call(
        functools.partial(_attn_near_kernel, tq=tq, n_heads=n_heads,
                          cast_bodies=tuple(job.body for job in jobs)),
        grid=(nq,),
        in_specs=[
            pl.BlockSpec((tq, width), lambda i: (i, 0)),
            pl.BlockSpec((tq, width), lambda i: (i, 1)),
            pl.BlockSpec((tq, width), lambda i: (prev(i), 1)),
            pl.BlockSpec((tq, width), lambda i: (i, 2)),
            pl.BlockSpec((tq, width), lambda i: (prev(i), 2)),
            pl.BlockSpec((tq, tq), lambda i: (0, 0)),
        ] + [job.in_spec for job in jobs],
        out_specs=[pl.BlockSpec((tq, width), lambda i: (i, 0)),
                   pl.BlockSpec((1, SUBLANES, HEAD_DIM), lambda i: (i, 0, 0))] + [job.out_spec for job in jobs],
        out_shape=[jax.ShapeDtypeStruct((s, width), BF16),
                   jax.ShapeDtypeStruct((nq, SUBLANES, HEAD_DIM), F32)] + [job.out_shape for job in jobs],
        compiler_params=pltpu.CompilerParams(
            dimension_semantics=("arbitrary",), vmem_limit_bytes=VMEM_LIMIT),
        name="stickbreak_attn_near",
    )(qkv, qkv, qkv, qkv, qkv, _cumsum_matrix(tq), *[job.w for job in jobs])
    return outs[0], outs[1], outs[2:]


def _attention(qkv, n_heads, cast_jobs):
    y_near, rmax, cast = _attention_near(qkv, n_heads, cast_jobs)
    need_far = jnp.max(rmax) > LOG2_ZERO_BOUND
    y = lax.cond(need_far, lambda qkv_, y_: _attention_walk(qkv_, n_heads),
                 lambda qkv_, y_: y_, qkv, y_near)
    return y, cast


def _attention_walk(qkv, n_heads):
    s = qkv.shape[0]
    tq = ATTN_Q
    assert s % tq == 0 and s >= 2 * tq and n_heads % ATTN_HEADS == 0
    width = ATTN_HEADS * HEAD_DIM
    groups = n_heads // ATTN_HEADS
    return pl.pallas_call(
        functools.partial(_attn_kernel, tq=tq),
        grid=(groups, s // tq),
        in_specs=[
            pl.BlockSpec((tq, width), lambda g, i: (i, g)),
            pl.BlockSpec((s, width), lambda g, i: (0, groups + g)),
            pl.BlockSpec((s, width), lambda g, i: (0, 2 * groups + g)),
            pl.BlockSpec((tq, tq), lambda g, i: (0, 0)),
        ],
        out_specs=pl.BlockSpec((tq, width), lambda g, i: (i, g)),
        out_shape=jax.ShapeDtypeStruct((s, n_heads * HEAD_DIM), BF16),
        scratch_shapes=[pltpu.VMEM((ATTN_HEADS, tq, HEAD_DIM), F32),
                        pltpu.VMEM((ATTN_HEADS, tq, HEAD_DIM), F32)],
        compiler_params=pltpu.CompilerParams(
            dimension_semantics=("parallel", "arbitrary"), vmem_limit_bytes=VMEM_LIMIT),
        name="stickbreak_attn",
    )(qkv, qkv, qkv, _cumsum_matrix(tq))


def _outproj_kernel(x_ref, ya_ref, ybc_ref, wa_ref, wb_ref, o_ref):
    o_ref[...] = x_ref[...] + _dot(ya_ref[...], wa_ref[...]) + _dot(ybc_ref[...], wb_ref[...])


def _outproj(x, y_a, y_bc, w_out):
    s, d = x.shape
    ka = y_a.shape[1]
    kb = y_bc.shape[1]
    assert ka == kb
    tm = min(PROJ_ROWS, s)
    return pl.pallas_call(
        _outproj_kernel,
        grid=(s // tm,),
        in_specs=[
            pl.BlockSpec((tm, d), lambda i: (i, 0)),
            pl.BlockSpec((tm, ka), lambda i: (i, 0)),
            pl.BlockSpec((tm, kb), lambda i: (i, 0)),
            pl.BlockSpec((ka, d), lambda i: (0, 0)),
            pl.BlockSpec((kb, d), lambda i: (1, 0)),
        ],
        out_specs=pl.BlockSpec((tm, d), lambda i: (i, 0)),
        out_shape=jax.ShapeDtypeStruct((s, d), F32),
        compiler_params=pltpu.CompilerParams(
            dimension_semantics=("parallel",), vmem_limit_bytes=VMEM_LIMIT),
        name="out_proj",
    )(x, y_a, y_bc, w_out, w_out)


def _cast_kernel(x_ref, o_ref):
    o_ref[...] = x_ref[...].astype(BF16)


def _cast_gate_up_kernel(x_ref, o_ref, *, d_ff):
    nj, rows, two_tf = o_ref.shape
    tf = two_tf // 2
    for j in range(nj):
        n = min(tf, d_ff - j * tf)
        o_ref[j, :, :n] = x_ref[:, j * tf:j * tf + n].astype(BF16)
        o_ref[j, :, tf:tf + n] = x_ref[:, d_ff + j * tf:d_ff + j * tf + n].astype(BF16)
        if n < tf:
            zeros = jnp.zeros((rows, tf - n), BF16)
            o_ref[j, :, n:tf] = zeros
            o_ref[j, :, tf + n:] = zeros


def _cast_down_kernel(x_ref, o_ref, *, d_ff):
    o_ref[:d_ff, :] = (x_ref[...] * FFN_RES_WEIGHT).astype(BF16)
    if o_ref.shape[0] > d_ff:
        o_ref[d_ff:, :] = jnp.zeros((o_ref.shape[0] - d_ff, o_ref.shape[1]), BF16)


def _cast_call(body, w, layer, in_block, out_block, out_shape, grid, index, name, out_index=None):
    return pl.pallas_call(
        body,
        grid=(grid,),
        in_specs=[pl.BlockSpec((None,) + in_block, lambda i: (layer,) + index(i))],
        out_specs=pl.BlockSpec(out_block, out_index or index),
        out_shape=jax.ShapeDtypeStruct(out_shape, BF16),
        compiler_params=pltpu.CompilerParams(
            dimension_semantics=("parallel",), vmem_limit_bytes=VMEM_LIMIT),
        name=name,
    )(w)


def _cast_plain(w, layer):
    _, rows, cols = w.shape
    rb = CAST_ROWS
    assert rows % rb == 0
    return _cast_call(_cast_kernel, w, layer, (rb, cols), (rb, cols), (rows, cols), rows // rb,
                      lambda i: (i, 0), "cast_bf16")


def _cast_gate_up(w_gu, layer):
    _, d, two_ff = w_gu.shape
    d_ff = two_ff // 2
    nj = (d_ff + FFN_COLS - 1) // FFN_COLS
    rb = CAST_ROWS // 2
    assert d % rb == 0
    return _cast_call(functools.partial(_cast_gate_up_kernel, d_ff=d_ff), w_gu, layer,
                      (rb, two_ff), (nj, rb, 2 * FFN_COLS), (nj, d, 2 * FFN_COLS), d // rb,
                      lambda i: (i, 0), "cast_gate_up", out_index=lambda i: (0, i, 0))


def _cast_down(w_down, layer):
    _, d_ff, d = w_down.shape
    d_ffp = d_ff + (-d_ff) % FFN_COLS
    assert d % CAST_COLS == 0
    return _cast_call(functools.partial(_cast_down_kernel, d_ff=d_ff), w_down, layer,
                      (d_ff, CAST_COLS), (d_ffp, CAST_COLS), (d_ffp, d), d // CAST_COLS,
                      lambda i: (0, i), "cast_down")


class _CastJob(NamedTuple):
    w: jax.Array
    in_spec: pl.BlockSpec
    out_spec: pl.BlockSpec
    out_shape: jax.ShapeDtypeStruct
    body: Callable


def _gate_up_job(w_gu, layer, n):
    _, d, two_ff = w_gu.shape
    d_ff = two_ff // 2
    nj = (d_ff + FFN_COLS - 1) // FFN_COLS
    rb = d // n
    assert d % n == 0 and rb % BF16_ROWS == 0
    return _CastJob(w_gu, pl.BlockSpec((None, rb, two_ff), lambda i: (layer, i, 0)),
                    pl.BlockSpec((nj, rb, 2 * FFN_COLS), lambda i: (0, i, 0)),
                    jax.ShapeDtypeStruct((nj, d, 2 * FFN_COLS), BF16),
                    lambda i, w_ref, o_ref: _cast_gate_up_kernel(w_ref, o_ref, d_ff=d_ff))


def _plain_job(w, layer, n):
    _, rows, cols = w.shape
    rb = rows // n
    assert rows % n == 0 and rb % BF16_ROWS == 0
    return _CastJob(w, pl.BlockSpec((None, rb, cols), lambda i: (layer, i, 0)),
                    pl.BlockSpec((rb, cols), lambda i: (i, 0)),
                    jax.ShapeDtypeStruct((rows, cols), BF16),
                    lambda i, w_ref, o_ref: _cast_kernel(w_ref, o_ref))


def _cast_down_rows_kernel(i, x_ref, o_ref, *, n_valid, n_blocks):
    if n_blocks == n_valid:
        o_ref[...] = (x_ref[...] * FFN_RES_WEIGHT).astype(BF16)
        return

    @pl.when(i < n_valid)
    def _():
        o_ref[...] = (x_ref[...] * FFN_RES_WEIGHT).astype(BF16)

    @pl.when(i >= n_valid)
    def _():
        o_ref[...] = jnp.zeros_like(o_ref)


def _down_job(w_down, layer, n):
    _, d_ff, d = w_down.shape
    d_ffp = d_ff + (-d_ff) % FFN_COLS
    rb = math.gcd(d_ff, d_ffp)
    n_valid, n_blocks = d_ff // rb, d_ffp // rb
    assert rb % BF16_ROWS == 0 and n_blocks <= n
    return _CastJob(w_down, pl.BlockSpec((None, rb, d), lambda i: (layer, jnp.minimum(i, n_valid - 1), 0)),
                    pl.BlockSpec((rb, d), lambda i: (jnp.minimum(i, n_blocks - 1), 0)),
                    jax.ShapeDtypeStruct((d_ffp, d), BF16),
                    functools.partial(_cast_down_rows_kernel, n_valid=n_valid, n_blocks=n_blocks))


def kernel(x, ffn1_norm, ffn1_w_gu, ffn1_w_down, mix_norm, w_in, sgu_norm, sgu_w, sgu_b, pool_w,
           pool_scale, w_out, ffn2_norm, ffn2_w_gu, ffn2_w_down, final_norm):
    b, s, d = x.shape
    assert b == 1
    depth = w_in.shape[0]
    attn_width = w_out.shape[1] // 2
    n_heads = attn_width // HEAD_DIM
    width = SGU_GROUPS * GROUP_DIM
    h = x.reshape(s, d)
    fg = final_norm.reshape(1, d)
    wgu1 = _cast_gate_up(ffn1_w_gu, 0)
    wd1 = _cast_down(ffn1_w_down, 0)
    w_in_l = _cast_plain(w_in, 0)
    for l in range(depth):
        mix_g = mix_norm[l].reshape(1, d)
        sgu_bias = jnp.repeat(sgu_b[l].T, GROUP_DIM, axis=1)

        h = _ffn(h, ffn1_norm[l].reshape(1, d), wgu1, wd1, fg, final=False)
        qkv, y_bc = _inproj(h, mix_g, w_in_l, attn_width, sgu_norm[l].reshape(1, width),
                            sgu_w[l], sgu_bias, pool_w[l].astype(BF16), pool_scale[l].reshape(1, width))
        jobs = [functools.partial(_plain_job, w_out, l),
                functools.partial(_gate_up_job, ffn2_w_gu, l),
                functools.partial(_down_job, ffn2_w_down, l)]
        if l + 1 < depth:
            jobs += [functools.partial(_plain_job, w_in, l + 1),
                     functools.partial(_gate_up_job, ffn1_w_gu, l + 1),
                     functools.partial(_down_job, ffn1_w_down, l + 1)]
        y_a, cast = _attention(qkv, n_heads, jobs)
        w_out_l, wgu2, wd2 = cast[:3]
        h = _outproj(h, y_a, y_bc, w_out_l)
        h = _ffn(h, ffn2_norm[l].reshape(1, d), wgu2, wd2, fg, final=(l == depth - 1))
        if l + 1 < depth:
            w_in_l, wgu1, wd1 = cast[3:]
    return h.reshape(b, s, d)
```

```python
import functools
import math
from typing import Callable, NamedTuple

import numpy as np
import jax
import jax.numpy as jnp
from jax import lax
from jax.experimental import pallas as pl
from jax.experimental.pallas import tpu as pltpu

F32 = jnp.float32
BF16 = jnp.bfloat16

HEAD_DIM = 128
LANES = 128
SUBLANES = 8
BF16_ROWS = 16
SGU_GROUPS = 4
SGU_CHUNK = 128
POOL_WINDOWS = (2, 4, 8, 16)
GROUP_DIM = 128
EPS = 1e-6
FFN_RES_WEIGHT = 0.5
assert math.frexp(FFN_RES_WEIGHT)[0] == 0.5

V7X_VMEM_BYTES = 64 * 1024 * 1024
VMEM_LIMIT = V7X_VMEM_BYTES - 6 * 1024 * 1024

FFN_ROWS = 1024
FFN_COLS = 512
PROJ_ROWS = 512
QKV_COLS = 1024
CAST_ROWS = 512
CAST_COLS = 512
ATTN_Q = 256
ATTN_HEADS = 2
LOG2_E = 1.4426950408889634
LOG2_ZERO_BOUND = -151.0


def _rmsnorm(x, g):
    inv = lax.rsqrt(jnp.mean(x * x, axis=-1, keepdims=True) + EPS)
    return x * inv * g


def _dot(a, b):
    return jnp.dot(a, b, preferred_element_type=F32)


def _ffn_kernel(x_ref, gains_ref, wgu_ref, wd_ref, o_ref, xn_ref, *, nj, final):
    j = pl.program_id(1)
    tf = wd_ref.shape[0]

    @pl.when(j == 0)
    def _():
        xn_ref[...] = _rmsnorm(x_ref[...], gains_ref[0:1, :]).astype(BF16)

    def down_proj():
        gu = _dot(xn_ref[...], wgu_ref[...])
        gate = gu[:, :tf]
        h = (gate * jax.nn.sigmoid(gate) * gu[:, tf:]).astype(BF16)
        return _dot(h, wd_ref[...])

    @pl.when(j == 0)
    def _():
        o_ref[...] = x_ref[...] + down_proj()

    @pl.when(j > 0)
    def _():
        o_ref[...] += down_proj()

    if final:
        @pl.when(j == nj - 1)
        def _():
            o_ref[...] = _rmsnorm(o_ref[...], gains_ref[1:2, :])


def _ffn(x, gains, w_gu_p, w_down_p, *, final):
    s, d = x.shape
    nj = w_gu_p.shape[0]
    assert w_gu_p.shape == (nj, d, 2 * FFN_COLS) and w_down_p.shape == (nj * FFN_COLS, d)
    tm = min(FFN_ROWS, s)
    ni = s // tm
    x_index = lambda i, j: (jnp.minimum(i + jnp.minimum(j, 1), ni - 1), 0)
    return pl.pallas_call(
        functools.partial(_ffn_kernel, nj=nj, final=final),
        grid=(ni, nj),
        in_specs=[
            pl.BlockSpec((tm, d), x_index),
            pl.BlockSpec((2, d), lambda i, j: (0, 0)),
            pl.BlockSpec((None, d, 2 * FFN_COLS), lambda i, j: (j, 0, 0)),
            pl.BlockSpec((FFN_COLS, d), lambda i, j: (j, 0)),
        ],
        out_specs=pl.BlockSpec((tm, d), lambda i, j: (i, 0)),
        out_shape=jax.ShapeDtypeStruct((s, d), F32),
        scratch_shapes=[pltpu.VMEM((tm, d), BF16)],
        compiler_params=pltpu.CompilerParams(
            dimension_semantics=("arbitrary", "arbitrary"), vmem_limit_bytes=VMEM_LIMIT),
        name="ffn",
    )(x, gains, w_gu_p, w_down_p)


def _inproj_kernel(x_ref, g_ref, w_ref, sn_ref, sw_ref, sb_ref, band_ref, pw_ref, ps_ref,
                   qkv_ref, o_ref, ph_ref, plo_ref, *, tm, attn_width, scale):
    i = pl.program_id(0)
    ch = SGU_CHUNK
    width = SGU_GROUPS * GROUP_DIM
    xn = _rmsnorm(x_ref[...], g_ref[...]).astype(BF16)
    proj = _dot(xn, w_ref[:, 3 * attn_width:])
    u = jax.nn.gelu(proj[:, :width])
    v = jax.nn.gelu(proj[:, width:2 * width])
    p = proj[:, 2 * width:]

    @pl.when(i == 0)
    def _():
        ph_ref[0:ch, :] = jnp.zeros((ch, width), BF16)
        plo_ref[0:ch, :] = jnp.zeros((ch, width), BF16)

    p_hi = p.astype(BF16)
    ph_ref[ch:ch + tm, :] = p_hi
    plo_ref[ch:ch + tm, :] = (p - p_hi.astype(F32)).astype(BF16)

    row = lax.broadcasted_iota(jnp.int32, (ch, ch), 0)
    col = lax.broadcasted_iota(jnp.int32, (ch, ch), 1)
    tril = col <= row
    units = [(grp, c) for grp in range(SGU_GROUPS) for c in range(tm // ch)]
    lanes = lambda grp: slice(grp * GROUP_DIM, (grp + 1) * GROUP_DIM)
    rows = lambda c: slice(c * ch, (c + 1) * ch)
    vg = [_rmsnorm(v[:, lanes(grp)], sn_ref[:, lanes(grp)]).astype(BF16) for grp in range(SGU_GROUPS)]
    w_causal = [jnp.where(tril, sw_ref[grp], 0.0).astype(BF16) for grp in range(SGU_GROUPS)]
    mixed = {}
    for grp in range(SGU_GROUPS):
        for c in range(0, tm // ch, 2):
            pair = _dot(w_causal[grp], jnp.concatenate([vg[grp][rows(c)], vg[grp][rows(c + 1)]], axis=1))
            mixed[grp, c], mixed[grp, c + 1] = pair[:, :GROUP_DIM], pair[:, GROUP_DIM:]
    wsum = {}
    for grp, c in units:
        ext = slice(c * ch, (c + 2) * ch)
        pair = _dot(band_ref[grp], jnp.concatenate([ph_ref[ext, lanes(grp)], plo_ref[ext, lanes(grp)]], axis=1))
        wsum[grp, c] = pair[:, :GROUP_DIM] + pair[:, GROUP_DIM:]
    for grp, c in units:
        o_ref[rows(c), lanes(grp)] = (u[rows(c), lanes(grp)] * (mixed[grp, c] + sb_ref[:, lanes(grp)])).astype(BF16)
    dlt = {}
    for grp, c in units:
        pos = i * tm + c * ch + row
        count = jnp.minimum(pos + 1, POOL_WINDOWS[grp]).astype(F32)
        dlt[grp, c] = (wsum[grp, c] / count - p[rows(c), lanes(grp)]).astype(BF16)
    for grp, c in units:
        y = _dot(dlt[grp, c], pw_ref[grp]) * ps_ref[:, lanes(grp)]
        o_ref[rows(c), width + grp * GROUP_DIM:width + (grp + 1) * GROUP_DIM] = y.astype(BF16)

    ph_ref[0:ch, :] = ph_ref[tm:tm + ch, :]
    plo_ref[0:ch, :] = plo_ref[tm:tm + ch, :]

    for c in range(3 * attn_width // QKV_COLS):
        cols = slice(c * QKV_COLS, (c + 1) * QKV_COLS)
        acc = _dot(xn, w_ref[:, cols])
        if (c + 1) * QKV_COLS <= attn_width:
            acc = acc * scale
        qkv_ref[:, cols] = acc.astype(BF16)


def _pool_bands():
    t = np.arange(SGU_CHUNK)[:, None] + SGU_CHUNK
    j = np.arange(2 * SGU_CHUNK)[None, :]
    bands = [((t - j >= 0) & (t - j < w)).astype(np.float32) for w in POOL_WINDOWS]
    return jnp.asarray(np.stack(bands), dtype=BF16)


def _inproj(x, g, w_in, attn_width, sgu_norm, sgu_w, sgu_bias, pool_w, pool_scale):
    s, d = x.shape
    width = SGU_GROUPS * GROUP_DIM
    n_in = w_in.shape[1]
    assert n_in == 3 * attn_width + 3 * width and attn_width % QKV_COLS == 0
    tm = min(PROJ_ROWS, s)
    full = lambda shape: pl.BlockSpec(shape, lambda i: (0,) * len(shape))
    return pl.pallas_call(
        functools.partial(_inproj_kernel, tm=tm, attn_width=attn_width,
                          scale=LOG2_E / math.sqrt(HEAD_DIM)),
        grid=(s // tm,),
        in_specs=[
            pl.BlockSpec((tm, d), lambda i: (i, 0)),
            full((1, d)),
            pl.BlockSpec((d, n_in), lambda i: (0, 0), pipeline_mode=pl.Buffered(1)),
            full((1, width)),
            full((SGU_GROUPS, SGU_CHUNK, SGU_CHUNK)),
            full((SGU_CHUNK, width)),
            full((SGU_GROUPS, SGU_CHUNK, 2 * SGU_CHUNK)),
            full((SGU_GROUPS, GROUP_DIM, GROUP_DIM)),
            full((1, width)),
        ],
        out_specs=[pl.BlockSpec((tm, 3 * attn_width), lambda i: (i, 0)),
                   pl.BlockSpec((tm, 2 * width), lambda i: (i, 0))],
        out_shape=[jax.ShapeDtypeStruct((s, 3 * attn_width), BF16),
                   jax.ShapeDtypeStruct((s, 2 * width), BF16)],
        scratch_shapes=[pltpu.VMEM((tm + SGU_CHUNK, width), BF16),
                        pltpu.VMEM((tm + SGU_CHUNK, width), BF16)],
        compiler_params=pltpu.CompilerParams(
            dimension_semantics=("arbitrary",), vmem_limit_bytes=VMEM_LIMIT),
        name="in_proj",
    )(x, g, w_in, sgu_norm, sgu_w, sgu_bias, _pool_bands(), pool_w, pool_scale)


def _attn_block(q, ks, vs, m_neg, r, causal):
    log_own, sp16, total = _attn_softplus(_attn_scores(q, ks), causal)
    a = _attn_weights(log_own, _dot(sp16, m_neg), r, causal)
    return _dot(a, vs), (total if r is None else r + total)


def _attn_scores(q, ks):
    return lax.dot_general(q, ks, (((1,), (1,)), ((), ())), preferred_element_type=F32)


def _attn_softplus(z2, causal):
    neg_abs = pltpu.bitcast(pltpu.bitcast(z2, jnp.uint32) | jnp.uint32(0x80000000), F32)
    sp = jnp.maximum(z2, 0.0) + jnp.log(1.0 + jnp.exp2(neg_abs)) * LOG2_E
    if causal is not None:
        sp = jnp.where(causal, sp, 0.0)
    total = jnp.broadcast_to(-jnp.sum(sp, axis=1, keepdims=True), (z2.shape[0], HEAD_DIM))
    return z2 - sp, sp.astype(BF16), total


def _attn_weights(log_own, later, r, causal):
    log_a = log_own + later
    if r is not None:
        log_a = log_a + jnp.concatenate([r] * (log_own.shape[1] // HEAD_DIM), axis=1)
    a = jnp.exp2(log_a)
    if causal is not None:
        a = jnp.where(causal, a, 0.0)
    return a.astype(BF16)


def _attn_kernel(q_ref, k_ref, v_ref, m_ref, o_ref, r_ref, acc_ref, *, tq):
    i = pl.program_id(1)
    m_neg = m_ref[...]
    heads = r_ref.shape[0]
    row = lax.broadcasted_iota(jnp.int32, (tq, tq), 0)
    col = lax.broadcasted_iota(jnp.int32, (tq, tq), 1)
    causal = col < row
    diag = pl.multiple_of(i * tq, tq)
    prev = pl.multiple_of(jnp.maximum(i - 1, 0) * tq, tq)
    no_prev = jnp.where(i == 0, jnp.float32(-1e30), jnp.float32(0.0))

    rmax0 = None
    for h in range(heads):
        hs = slice(h * HEAD_DIM, (h + 1) * HEAD_DIM)
        q = q_ref[:, hs]
        pv0, r = _attn_block(q, k_ref[pl.ds(diag, tq), hs], v_ref[pl.ds(diag, tq), hs], m_neg, None, causal)
        pv1, r = _attn_block(q, k_ref[pl.ds(prev, tq), hs], v_ref[pl.ds(prev, tq), hs], m_neg,
                             r + no_prev, None)
        acc_ref[h] = pv0 + pv1
        r_ref[h] = r
        hmax = jnp.max(r)
        rmax0 = hmax if rmax0 is None else jnp.maximum(rmax0, hmax)

    def cond(carry):
        j, rmax = carry
        return jnp.logical_and(j >= 0, rmax > LOG2_ZERO_BOUND)

    def body(carry):
        j, _ = carry
        start = pl.multiple_of(j * tq, tq)
        rmax = None
        for h in range(heads):
            hs = slice(h * HEAD_DIM, (h + 1) * HEAD_DIM)
            pv, r = _attn_block(q_ref[:, hs], k_ref[pl.ds(start, tq), hs], v_ref[pl.ds(start, tq), hs],
                                m_neg, r_ref[h], None)
            acc_ref[h] += pv
            r_ref[h] = r
            hmax = jnp.max(r)
            rmax = hmax if rmax is None else jnp.maximum(rmax, hmax)
        return j - 1, rmax

    lax.while_loop(cond, body, (i - 2, rmax0))
    for h in range(heads):
        o_ref[:, h * HEAD_DIM:(h + 1) * HEAD_DIM] = acc_ref[h].astype(BF16)


def _cumsum_matrix(ch):
    j = np.arange(ch)[:, None]
    s = np.arange(ch)[None, :]
    return jnp.asarray(-(j > s).astype(np.float32), dtype=BF16)


def _attn_near_kernel(q_ref, kd_ref, kp_ref, vd_ref, vp_ref, m_ref, *rest, tq, n_heads, cast_bodies):
    n_cast = len(cast_bodies)
    cast_in, (o_ref, rmax_ref), cast_out = rest[:n_cast], rest[n_cast:n_cast + 2], rest[n_cast + 2:]
    i = pl.program_id(0)
    for body, w_ref, wo_ref in zip(cast_bodies, cast_in, cast_out, strict=True):
        body(i, w_ref, wo_ref)
    m_neg = m_ref[...]
    row = lax.broadcasted_iota(jnp.int32, (tq, tq), 0)
    col = lax.broadcasted_iota(jnp.int32, (tq, tq), 1)
    causal = col < row
    no_prev = jnp.where(i == 0, jnp.float32(-1e30), jnp.float32(0.0))
    heads = [slice(h * HEAD_DIM, (h + 1) * HEAD_DIM) for h in range(n_heads)]
    z_d, z_p, sp_d, sp_p, from_d, from_p, a_d, a_p = ({} for _ in range(8))
    rmax = []

    def stage(s, h):
        hs = heads[h]
        if s == 0:
            z_d[h] = _attn_scores(q_ref[:, hs], kd_ref[:, hs])
            z_p[h] = _attn_scores(q_ref[:, hs], kp_ref[:, hs])
        elif s == 1:
            sp_d[h] = _attn_softplus(z_d[h], causal)
            sp_p[h] = _attn_softplus(z_p[h], None)
        elif s == 2:
            from_d[h] = _dot(sp_d[h][1], m_neg)
            from_p[h] = _dot(sp_p[h][1], m_neg)
        elif s == 3:
            a_d[h] = _attn_weights(sp_d[h][0], from_d[h], None, causal)
            a_p[h] = _attn_weights(sp_p[h][0], from_p[h], sp_d[h][2] + no_prev, None)
        else:
            o_ref[:, hs] = (_dot(a_d[h], vd_ref[:, hs]) + _dot(a_p[h], vp_ref[:, hs])).astype(BF16)
            rmax.append(sp_d[h][2] + no_prev + sp_p[h][2])

    for t in range(n_heads + 4):
        for h in range(n_heads):
            if 0 <= t - h < 5:
                stage(t - h, h)
    rm = functools.reduce(jnp.maximum, rmax)
    rm = jnp.max(rm.reshape(tq // SUBLANES, SUBLANES, HEAD_DIM), axis=0)
    rmax_ref[0] = jnp.where(i <= 1, jnp.float32(-1e30), rm)


def _attention_near(qkv, n_heads, cast_jobs):
    s = qkv.shape[0]
    tq = ATTN_Q
    width = n_heads * HEAD_DIM
    assert s % tq == 0 and qkv.shape[1] == 3 * width
    nq = s // tq
    prev = lambda i: jnp.maximum(i - 1, 0)
    jobs = [make(nq) for make in cast_jobs]
    outs = pl.pallas_call(
        functools.partial(_attn_near_kernel, tq=tq, n_heads=n_heads,
                          cast_bodies=tuple(job.body for job in jobs)),
        grid=(nq,),
        in_specs=[
            pl.BlockSpec((tq, width), lambda i: (i, 0)),
            pl.BlockSpec((tq, width), lambda i: (i, 1)),
            pl.BlockSpec((tq, width), lambda i: (prev(i), 1)),
            pl.BlockSpec((tq, width), lambda i: (i, 2)),
            pl.BlockSpec((tq, width), lambda i: (prev(i), 2)),
            pl.BlockSpec((tq, tq), lambda i: (0, 0)),
        ] + [job.in_spec for job in jobs],
        out_specs=[pl.BlockSpec((tq, width), lambda i: (i, 0)),
                   pl.BlockSpec((1, SUBLANES, HEAD_DIM), lambda i: (i, 0, 0))] + [job.out_spec for job in jobs],
        out_shape=[jax.ShapeDtypeStruct((s, width), BF16),
                   jax.ShapeDtypeStruct((nq, SUBLANES, HEAD_DIM), F32)] + [job.out_shape for job in jobs],
        compiler_params=pltpu.CompilerParams(
            dimension_semantics=("arbitrary",), vmem_limit_bytes=VMEM_LIMIT),
        name="stickbreak_attn_near",
    )(qkv, qkv, qkv, qkv, qkv, _cumsum_matrix(tq), *[job.w for job in jobs])
    return outs[0], outs[1], outs[2:]


def _attention(qkv, n_heads, cast_jobs):
    y_near, rmax, cast = _attention_near(qkv, n_heads, cast_jobs)
    need_far = jnp.max(rmax) > LOG2_ZERO_BOUND
    y = lax.cond(need_far, lambda qkv_, y_: _attention_walk(qkv_, n_heads),
                 lambda qkv_, y_: y_, qkv, y_near)
    return y, cast


def _attention_walk(qkv, n_heads):
    s = qkv.shape[0]
    tq = ATTN_Q
    assert s % tq == 0 and s >= 2 * tq and n_heads % ATTN_HEADS == 0
    width = ATTN_HEADS * HEAD_DIM
    groups = n_heads // ATTN_HEADS
    return pl.pallas_call(
        functools.partial(_attn_kernel, tq=tq),
        grid=(groups, s // tq),
        in_specs=[
            pl.BlockSpec((tq, width), lambda g, i: (i, g)),
            pl.BlockSpec((s, width), lambda g, i: (0, groups + g)),
            pl.BlockSpec((s, width), lambda g, i: (0, 2 * groups + g)),
            pl.BlockSpec((tq, tq), lambda g, i: (0, 0)),
        ],
        out_specs=pl.BlockSpec((tq, width), lambda g, i: (i, g)),
        out_shape=jax.ShapeDtypeStruct((s, n_heads * HEAD_DIM), BF16),
        scratch_shapes=[pltpu.VMEM((ATTN_HEADS, tq, HEAD_DIM), F32),
                        pltpu.VMEM((ATTN_HEADS, tq, HEAD_DIM), F32)],
        compiler_params=pltpu.CompilerParams(
            dimension_semantics=("parallel", "arbitrary"), vmem_limit_bytes=VMEM_LIMIT),
        name="stickbreak_attn",
    )(qkv, qkv, qkv, _cumsum_matrix(tq))


def _outproj_kernel(x_ref, ya_ref, ybc_ref, wa_ref, wb_ref, o_ref):
    o_ref[...] = x_ref[...] + _dot(ya_ref[...], wa_ref[...]) + _dot(ybc_ref[...], wb_ref[...])


def _outproj(x, y_a, y_bc, w_out):
    s, d = x.shape
    ka = y_a.shape[1]
    kb = y_bc.shape[1]
    assert ka == kb
    tm = min(PROJ_ROWS, s)
    return pl.pallas_call(
        _outproj_kernel,
        grid=(s // tm,),
        in_specs=[
            pl.BlockSpec((tm, d), lambda i: (i, 0)),
            pl.BlockSpec((tm, ka), lambda i: (i, 0)),
            pl.BlockSpec((tm, kb), lambda i: (i, 0)),
            pl.BlockSpec((ka, d), lambda i: (0, 0)),
            pl.BlockSpec((kb, d), lambda i: (1, 0)),
        ],
        out_specs=pl.BlockSpec((tm, d), lambda i: (i, 0)),
        out_shape=jax.ShapeDtypeStruct((s, d), F32),
        compiler_params=pltpu.CompilerParams(
            dimension_semantics=("parallel",), vmem_limit_bytes=VMEM_LIMIT),
        name="out_proj",
    )(x, y_a, y_bc, w_out, w_out)


def _cast_kernel(x_ref, o_ref):
    o_ref[...] = x_ref[...].astype(BF16)


def _cast_gate_up_kernel(x_ref, o_ref, *, d_ff):
    nj, rows, two_tf = o_ref.shape
    tf = two_tf // 2
    for j in range(nj):
        n = min(tf, d_ff - j * tf)
        o_ref[j, :, :n] = x_ref[:, j * tf:j * tf + n].astype(BF16)
        o_ref[j, :, tf:tf + n] = x_ref[:, d_ff + j * tf:d_ff + j * tf + n].astype(BF16)
        if n < tf:
            zeros = jnp.zeros((rows, tf - n), BF16)
            o_ref[j, :, n:tf] = zeros
            o_ref[j, :, tf + n:] = zeros


def _cast_down_kernel(x_ref, o_ref, *, d_ff):
    o_ref[:d_ff, :] = (x_ref[...] * FFN_RES_WEIGHT).astype(BF16)
    if o_ref.shape[0] > d_ff:
        o_ref[d_ff:, :] = jnp.zeros((o_ref.shape[0] - d_ff, o_ref.shape[1]), BF16)


def _cast_call(body, w, layer, in_block, out_block, out_shape, grid, index, name, out_index=None):
    return pl.pallas_call(
        body,
        grid=(grid,),
        in_specs=[pl.BlockSpec((None,) + in_block, lambda i: (layer,) + index(i))],
        out_specs=pl.BlockSpec(out_block, out_index or index),
        out_shape=jax.ShapeDtypeStruct(out_shape, BF16),
        compiler_params=pltpu.CompilerParams(
            dimension_semantics=("parallel",), vmem_limit_bytes=VMEM_LIMIT),
        name=name,
    )(w)


def _cast_plain(w, layer):
    _, rows, cols = w.shape
    rb = CAST_ROWS
    assert rows % rb == 0
    return _cast_call(_cast_kernel, w, layer, (rb, cols), (rb, cols), (rows, cols), rows // rb,
                      lambda i: (i, 0), "cast_bf16")


def _cast_gate_up(w_gu, layer):
    _, d, two_ff = w_gu.shape
    d_ff = two_ff // 2
    nj = (d_ff + FFN_COLS - 1) // FFN_COLS
    rb = CAST_ROWS // 2
    assert d % rb == 0
    return _cast_call(functools.partial(_cast_gate_up_kernel, d_ff=d_ff), w_gu, layer,
                      (rb, two_ff), (nj, rb, 2 * FFN_COLS), (nj, d, 2 * FFN_COLS), d // rb,
                      lambda i: (i, 0), "cast_gate_up", out_index=lambda i: (0, i, 0))


def _cast_down(w_down, layer):
    _, d_ff, d = w_down.shape
    d_ffp = d_ff + (-d_ff) % FFN_COLS
    assert d % CAST_COLS == 0
    return _cast_call(functools.partial(_cast_down_kernel, d_ff=d_ff), w_down, layer,
                      (d_ff, CAST_COLS), (d_ffp, CAST_COLS), (d_ffp, d), d // CAST_COLS,
                      lambda i: (0, i), "cast_down")


class _CastJob(NamedTuple):
    w: jax.Array
    in_spec: pl.BlockSpec
    out_spec: pl.BlockSpec
    out_shape: jax.ShapeDtypeStruct
    body: Callable


def _gate_up_job(w_gu, layer, n):
    _, d, two_ff = w_gu.shape
    d_ff = two_ff // 2
    nj = (d_ff + FFN_COLS - 1) // FFN_COLS
    rb = d // n
    assert d % n == 0 and rb % BF16_ROWS == 0
    return _CastJob(w_gu, pl.BlockSpec((None, rb, two_ff), lambda i: (layer, i, 0)),
                    pl.BlockSpec((nj, rb, 2 * FFN_COLS), lambda i: (0, i, 0)),
                    jax.ShapeDtypeStruct((nj, d, 2 * FFN_COLS), BF16),
                    lambda i, w_ref, o_ref: _cast_gate_up_kernel(w_ref, o_ref, d_ff=d_ff))


def _plain_job(w, layer, n):
    _, rows, cols = w.shape
    rb = rows // n
    assert rows % n == 0 and rb % BF16_ROWS == 0
    return _CastJob(w, pl.BlockSpec((None, rb, cols), lambda i: (layer, i, 0)),
                    pl.BlockSpec((rb, cols), lambda i: (i, 0)),
                    jax.ShapeDtypeStruct((rows, cols), BF16),
                    lambda i, w_ref, o_ref: _cast_kernel(w_ref, o_ref))


def _cast_down_rows_kernel(i, x_ref, o_ref, *, n_valid, n_blocks):
    if n_blocks == n_valid:
        o_ref[...] = (x_ref[...] * FFN_RES_WEIGHT).astype(BF16)
        return

    @pl.when(i < n_valid)
    def _():
        o_ref[...] = (x_ref[...] * FFN_RES_WEIGHT).astype(BF16)

    @pl.when(i >= n_valid)
    def _():
        o_ref[...] = jnp.zeros_like(o_ref)


def _down_job(w_down, layer, n):
    _, d_ff, d = w_down.shape
    d_ffp = d_ff + (-d_ff) % FFN_COLS
    rb = math.gcd(d_ff, d_ffp)
    n_valid, n_blocks = d_ff // rb, d_ffp // rb
    assert rb % BF16_ROWS == 0 and n_blocks <= n
    return _CastJob(w_down, pl.BlockSpec((None, rb, d), lambda i: (layer, jnp.minimum(i, n_valid - 1), 0)),
                    pl.BlockSpec((rb, d), lambda i: (jnp.minimum(i, n_blocks - 1), 0)),
                    jax.ShapeDtypeStruct((d_ffp, d), BF16),
                    functools.partial(_cast_down_rows_kernel, n_valid=n_valid, n_blocks=n_blocks))


def kernel(x, ffn1_norm, ffn1_w_gu, ffn1_w_down, mix_norm, w_in, sgu_norm, sgu_w, sgu_b, pool_w,
           pool_scale, w_out, ffn2_norm, ffn2_w_gu, ffn2_w_down, final_norm):
    b, s, d = x.shape
    assert b == 1
    depth = w_in.shape[0]
    attn_width = w_out.shape[1] // 2
    n_heads = attn_width // HEAD_DIM
    width = SGU_GROUPS * GROUP_DIM
    h = x.reshape(s, d)
    wgu1 = _cast_gate_up(ffn1_w_gu, 0)
    wd1 = _cast_down(ffn1_w_down, 0)
    w_in_l = _cast_plain(w_in, 0)
    for l in range(depth):
        mix_g = mix_norm[l].reshape(1, d)
        sgu_bias = jnp.repeat(sgu_b[l].T, GROUP_DIM, axis=1)

        h = _ffn(h, jnp.stack([ffn1_norm[l], final_norm]), wgu1, wd1, final=False)
        qkv, y_bc = _inproj(h, mix_g, w_in_l, attn_width, sgu_norm[l].reshape(1, width),
                            sgu_w[l], sgu_bias, pool_w[l].astype(BF16), pool_scale[l].reshape(1, width))
        jobs = [functools.partial(_plain_job, w_out, l),
                functools.partial(_gate_up_job, ffn2_w_gu, l),
                functools.partial(_down_job, ffn2_w_down, l)]
        if l + 1 < depth:
            jobs += [functools.partial(_plain_job, w_in, l + 1),
                     functools.partial(_gate_up_job, ffn1_w_gu, l + 1),
                     functools.partial(_down_job, ffn1_w_down, l + 1)]
        y_a, cast = _attention(qkv, n_heads, jobs)
        w_out_l, wgu2, wd2 = cast[:3]
        h = _outproj(h, y_a, y_bc, w_out_l)
        h = _ffn(h, jnp.stack([ffn2_norm[l], final_norm]), wgu2, wd2, final=(l == depth - 1))
        if l + 1 < depth:
            w_in_l, wgu1, wd1 = cast[3:]
    return h.reshape(b, s, d)
```

```python
import functools
import math
from typing import Callable, NamedTuple

import numpy as np
import jax
import jax.numpy as jnp
from jax import lax
from jax.experimental import pallas as pl
from jax.experimental.pallas import tpu as pltpu

F32 = jnp.float32
BF16 = jnp.bfloat16

HEAD_DIM = 128
LANES = 128
SUBLANES = 8
BF16_ROWS = 16
SGU_GROUPS = 4
SGU_CHUNK = 128
POOL_WINDOWS = (2, 4, 8, 16)
GROUP_DIM = 128
EPS = 1e-6
FFN_RES_WEIGHT = 0.5
assert math.frexp(FFN_RES_WEIGHT)[0] == 0.5

V7X_VMEM_BYTES = 64 * 1024 * 1024
VMEM_LIMIT = V7X_VMEM_BYTES - 6 * 1024 * 1024

FFN_ROWS = 1024
FFN_COLS = 512
PROJ_ROWS = 512
QKV_COLS = 1024
CAST_ROWS = 512
CAST_COLS = 512
ATTN_Q = 256
ATTN_HEADS = 2
LOG2_E = 1.4426950408889634
LOG2_ZERO_BOUND = -151.0


def _rmsnorm(x, g):
    inv = lax.rsqrt(jnp.mean(x * x, axis=-1, keepdims=True) + EPS)
    return x * inv * g


def _dot(a, b):
    return jnp.dot(a, b, preferred_element_type=F32)


def _ffn_kernel(x_ref, gains_ref, wgu_ref, wd_ref, o_ref, xn_ref, *, nj, final):
    j = pl.program_id(1)
    tf = wd_ref.shape[0]

    @pl.when(j == 0)
    def _():
        xn_ref[...] = _rmsnorm(x_ref[...], gains_ref[0:1, :]).astype(BF16)

    def down_proj():
        gu = _dot(xn_ref[...], wgu_ref[...])
        gate = gu[:, :tf]
        h = (gate * jax.nn.sigmoid(gate) * gu[:, tf:]).astype(BF16)
        return _dot(h, wd_ref[...])

    @pl.when(j == 0)
    def _():
        o_ref[...] = x_ref[...] + down_proj()

    @pl.when(j > 0)
    def _():
        o_ref[...] += down_proj()

    if final:
        @pl.when(j == nj - 1)
        def _():
            o_ref[...] = _rmsnorm(o_ref[...], gains_ref[1:2, :])


def _ffn(x, gains, w_gu_p, w_down_p, *, final):
    s, d = x.shape
    nj = w_gu_p.shape[0]
    assert w_gu_p.shape == (nj, d, 2 * FFN_COLS) and w_down_p.shape == (nj * FFN_COLS, d)
    tm = min(FFN_ROWS, s)
    ni = s // tm
    x_index = lambda i, j: (jnp.minimum(i + jnp.minimum(j, 1), ni - 1), 0)
    return pl.pallas_call(
        functools.partial(_ffn_kernel, nj=nj, final=final),
        grid=(ni, nj),
        in_specs=[
            pl.BlockSpec((tm, d), x_index),
            pl.BlockSpec((2, d), lambda i, j: (0, 0)),
            pl.BlockSpec((None, d, 2 * FFN_COLS), lambda i, j: (j, 0, 0)),
            pl.BlockSpec((FFN_COLS, d), lambda i, j: (j, 0)),
        ],
        out_specs=pl.BlockSpec((tm, d), lambda i, j: (i, 0)),
        out_shape=jax.ShapeDtypeStruct((s, d), F32),
        scratch_shapes=[pltpu.VMEM((tm, d), BF16)],
        compiler_params=pltpu.CompilerParams(
            dimension_semantics=("arbitrary", "arbitrary"), vmem_limit_bytes=VMEM_LIMIT),
        name="ffn",
    )(x, gains, w_gu_p, w_down_p)


def _inproj_kernel(x_ref, g_ref, w_ref, sn_ref, sw_ref, sb_ref, band_ref, pw_ref, ps_ref,
                   qkv_ref, o_ref, ph_ref, plo_ref, *, tm, attn_width, scale):
    i = pl.program_id(0)
    ch = SGU_CHUNK
    width = SGU_GROUPS * GROUP_DIM
    xn = _rmsnorm(x_ref[...], g_ref[...]).astype(BF16)
    proj = _dot(xn, w_ref[:, 3 * attn_width:])
    u = jax.nn.gelu(proj[:, :width])
    v = jax.nn.gelu(proj[:, width:2 * width])
    p = proj[:, 2 * width:]

    @pl.when(i == 0)
    def _():
        ph_ref[0:ch, :] = jnp.zeros((ch, width), BF16)
        plo_ref[0:ch, :] = jnp.zeros((ch, width), BF16)

    p_hi = p.astype(BF16)
    ph_ref[ch:ch + tm, :] = p_hi
    plo_ref[ch:ch + tm, :] = (p - p_hi.astype(F32)).astype(BF16)

    row = lax.broadcasted_iota(jnp.int32, (ch, ch), 0)
    col = lax.broadcasted_iota(jnp.int32, (ch, ch), 1)
    tril = col <= row
    units = [(grp, c) for grp in range(SGU_GROUPS) for c in range(tm // ch)]
    lanes = lambda grp: slice(grp * GROUP_DIM, (grp + 1) * GROUP_DIM)
    rows = lambda c: slice(c * ch, (c + 1) * ch)
    vg = [_rmsnorm(v[:, lanes(grp)], sn_ref[:, lanes(grp)]).astype(BF16) for grp in range(SGU_GROUPS)]
    w_causal = [jnp.where(tril, sw_ref[grp], 0.0).astype(BF16) for grp in range(SGU_GROUPS)]
    mixed = {}
    for grp in range(SGU_GROUPS):
        for c in range(0, tm // ch, 2):
            pair = _dot(w_causal[grp], jnp.concatenate([vg[grp][rows(c)], vg[grp][rows(c + 1)]], axis=1))
            mixed[grp, c], mixed[grp, c + 1] = pair[:, :GROUP_DIM], pair[:, GROUP_DIM:]
    wsum = {}
    for grp, c in units:
        ext = slice(c * ch, (c + 2) * ch)
        pair = _dot(band_ref[grp], jnp.concatenate([ph_ref[ext, lanes(grp)], plo_ref[ext, lanes(grp)]], axis=1))
        wsum[grp, c] = pair[:, :GROUP_DIM] + pair[:, GROUP_DIM:]
    for grp, c in units:
        o_ref[rows(c), lanes(grp)] = (u[rows(c), lanes(grp)] * (mixed[grp, c] + sb_ref[:, lanes(grp)])).astype(BF16)
    dlt = {}
    for grp, c in units:
        pos = i * tm + c * ch + row
        count = jnp.minimum(pos + 1, POOL_WINDOWS[grp]).astype(F32)
        dlt[grp, c] = (wsum[grp, c] / count - p[rows(c), lanes(grp)]).astype(BF16)
    for grp, c in units:
        y = _dot(dlt[grp, c], pw_ref[grp]) * ps_ref[:, lanes(grp)]
        o_ref[rows(c), width + grp * GROUP_DIM:width + (grp + 1) * GROUP_DIM] = y.astype(BF16)

    ph_ref[0:ch, :] = ph_ref[tm:tm + ch, :]
    plo_ref[0:ch, :] = plo_ref[tm:tm + ch, :]

    for c in range(3 * attn_width // QKV_COLS):
        cols = slice(c * QKV_COLS, (c + 1) * QKV_COLS)
        acc = _dot(xn, w_ref[:, cols])
        if (c + 1) * QKV_COLS <= attn_width:
            acc = acc * scale
        qkv_ref[:, cols] = acc.astype(BF16)


def _pool_bands():
    t = np.arange(SGU_CHUNK)[:, None] + SGU_CHUNK
    j = np.arange(2 * SGU_CHUNK)[None, :]
    bands = [((t - j >= 0) & (t - j < w)).astype(np.float32) for w in POOL_WINDOWS]
    return jnp.asarray(np.stack(bands), dtype=BF16)


def _inproj(x, g, w_in, attn_width, sgu_norm, sgu_w, sgu_bias, pool_w, pool_scale):
    s, d = x.shape
    width = SGU_GROUPS * GROUP_DIM
    n_in = w_in.shape[1]
    assert n_in == 3 * attn_width + 3 * width and attn_width % QKV_COLS == 0
    tm = min(PROJ_ROWS, s)
    full = lambda shape: pl.BlockSpec(shape, lambda i: (0,) * len(shape))
    return pl.pallas_call(
        functools.partial(_inproj_kernel, tm=tm, attn_width=attn_width,
                          scale=LOG2_E / math.sqrt(HEAD_DIM)),
        grid=(s // tm,),
        in_specs=[
            pl.BlockSpec((tm, d), lambda i: (i, 0)),
            full((1, d)),
            pl.BlockSpec((d, n_in), lambda i: (0, 0), pipeline_mode=pl.Buffered(1)),
            full((1, width)),
            full((SGU_GROUPS, SGU_CHUNK, SGU_CHUNK)),
            full((SGU_CHUNK, width)),
            full((SGU_GROUPS, SGU_CHUNK, 2 * SGU_CHUNK)),
            full((SGU_GROUPS, GROUP_DIM, GROUP_DIM)),
            full((1, width)),
        ],
        out_specs=[pl.BlockSpec((tm, 3 * attn_width), lambda i: (i, 0)),
                   pl.BlockSpec((tm, 2 * width), lambda i: (i, 0))],
        out_shape=[jax.ShapeDtypeStruct((s, 3 * attn_width), BF16),
                   jax.ShapeDtypeStruct((s, 2 * width), BF16)],
        scratch_shapes=[pltpu.VMEM((tm + SGU_CHUNK, width), BF16),
                        pltpu.VMEM((tm + SGU_CHUNK, width), BF16)],
        compiler_params=pltpu.CompilerParams(
            dimension_semantics=("arbitrary",), vmem_limit_bytes=VMEM_LIMIT),
        name="in_proj",
    )(x, g, w_in, sgu_norm, sgu_w, sgu_bias, _pool_bands(), pool_w, pool_scale)


def _attn_block(q, ks, vs, m_neg, r, causal):
    log_own, sp16, total = _attn_softplus(_attn_scores(q, ks), causal)
    a = _attn_weights(log_own, _dot(sp16, m_neg), r, causal)
    return _dot(a, vs), (total if r is None else r + total)


def _attn_scores(q, ks):
    return lax.dot_general(q, ks, (((1,), (1,)), ((), ())), preferred_element_type=F32)


def _attn_softplus(z2, causal):
    sp = jnp.maximum(z2, 0.0) + jnp.log(1.0 + jnp.exp2(-jnp.abs(z2))) * LOG2_E
    if causal is not None:
        sp = jnp.where(causal, sp, 0.0)
    total = jnp.broadcast_to(-jnp.sum(sp, axis=1, keepdims=True), (z2.shape[0], HEAD_DIM))
    return z2 - sp, sp.astype(BF16), total


def _attn_weights(log_own, later, r, causal):
    log_a = log_own + later
    if r is not None:
        log_a = log_a + jnp.concatenate([r] * (log_own.shape[1] // HEAD_DIM), axis=1)
    a = jnp.exp2(log_a)
    if causal is not None:
        a = jnp.where(causal, a, 0.0)
    return a.astype(BF16)


def _attn_kernel(q_ref, k_ref, v_ref, m_ref, o_ref, r_ref, acc_ref, *, tq):
    i = pl.program_id(1)
    m_neg = m_ref[...]
    heads = r_ref.shape[0]
    row = lax.broadcasted_iota(jnp.int32, (tq, tq), 0)
    col = lax.broadcasted_iota(jnp.int32, (tq, tq), 1)
    causal = col < row
    diag = pl.multiple_of(i * tq, tq)
    prev = pl.multiple_of(jnp.maximum(i - 1, 0) * tq, tq)
    no_prev = jnp.where(i == 0, jnp.float32(-1e30), jnp.float32(0.0))

    rmax0 = None
    for h in range(heads):
        hs = slice(h * HEAD_DIM, (h + 1) * HEAD_DIM)
        q = q_ref[:, hs]
        pv0, r = _attn_block(q, k_ref[pl.ds(diag, tq), hs], v_ref[pl.ds(diag, tq), hs], m_neg, None, causal)
        pv1, r = _attn_block(q, k_ref[pl.ds(prev, tq), hs], v_ref[pl.ds(prev, tq), hs], m_neg,
                             r + no_prev, None)
        acc_ref[h] = pv0 + pv1
        r_ref[h] = r
        hmax = jnp.max(r)
        rmax0 = hmax if rmax0 is None else jnp.maximum(rmax0, hmax)

    def cond(carry):
        j, rmax = carry
        return jnp.logical_and(j >= 0, rmax > LOG2_ZERO_BOUND)

    def body(carry):
        j, _ = carry
        start = pl.multiple_of(j * tq, tq)
        rmax = None
        for h in range(heads):
            hs = slice(h * HEAD_DIM, (h + 1) * HEAD_DIM)
            pv, r = _attn_block(q_ref[:, hs], k_ref[pl.ds(start, tq), hs], v_ref[pl.ds(start, tq), hs],
                                m_neg, r_ref[h], None)
            acc_ref[h] += pv
            r_ref[h] = r
            hmax = jnp.max(r)
            rmax = hmax if rmax is None else jnp.maximum(rmax, hmax)
        return j - 1, rmax

    lax.while_loop(cond, body, (i - 2, rmax0))
    for h in range(heads):
        o_ref[:, h * HEAD_DIM:(h + 1) * HEAD_DIM] = acc_ref[h].astype(BF16)


def _cumsum_matrix(ch):
    j = np.arange(ch)[:, None]
    s = np.arange(ch)[None, :]
    return jnp.asarray(-(j > s).astype(np.float32), dtype=BF16)


def _attn_near_kernel(q_ref, kd_ref, kp_ref, vd_ref, vp_ref, m_ref, *rest, tq, n_heads, cast_bodies):
    n_cast = len(cast_bodies)
    cast_in, (o_ref, rmax_ref), cast_out = rest[:n_cast], rest[n_cast:n_cast + 2], rest[n_cast + 2:]
    i = pl.program_id(0)
    for body, w_ref, wo_ref in zip(cast_bodies, cast_in, cast_out, strict=True):
        body(i, w_ref, wo_ref)
    m_neg = m_ref[...]
    row = lax.broadcasted_iota(jnp.int32, (tq, tq), 0)
    col = lax.broadcasted_iota(jnp.int32, (tq, tq), 1)
    causal = col < row
    no_prev = jnp.where(i == 0, jnp.float32(-1e30), jnp.float32(0.0))
    heads = [slice(h * HEAD_DIM, (h + 1) * HEAD_DIM) for h in range(n_heads)]
    z_d, z_p, sp_d, sp_p, from_d, from_p, a_d, a_p = ({} for _ in range(8))
    rmax = []

    def stage(s, h):
        hs = heads[h]
        if s == 0:
            z_d[h] = _attn_scores(q_ref[:, hs], kd_ref[:, hs])
            z_p[h] = _attn_scores(q_ref[:, hs], kp_ref[:, hs])
        elif s == 1:
            sp_d[h] = _attn_softplus(z_d[h], causal)
            sp_p[h] = _attn_softplus(z_p[h], None)
        elif s == 2:
            from_d[h] = _dot(sp_d[h][1], m_neg)
            from_p[h] = _dot(sp_p[h][1], m_neg)
        elif s == 3:
            a_d[h] = _attn_weights(sp_d[h][0], from_d[h], None, causal)
            a_p[h] = _attn_weights(sp_p[h][0], from_p[h], sp_d[h][2] + no_prev, None)
        else:
            o_ref[:, hs] = (_dot(a_d[h], vd_ref[:, hs]) + _dot(a_p[h], vp_ref[:, hs])).astype(BF16)
            rmax.append(sp_d[h][2] + no_prev + sp_p[h][2])

    for t in range(n_heads + 4):
        for h in range(n_heads):
            if 0 <= t - h < 5:
                stage(t - h, h)
    rm = functools.reduce(jnp.maximum, rmax)
    rm = jnp.max(rm.reshape(tq // SUBLANES, SUBLANES, HEAD_DIM), axis=0)
    rmax_ref[0] = jnp.where(i <= 1, jnp.float32(-1e30), rm)


def _attention_near(qkv, n_heads, cast_jobs):
    s = qkv.shape[0]
    tq = ATTN_Q
    width = n_heads * HEAD_DIM
    assert s % tq == 0 and qkv.shape[1] == 3 * width
    nq = s // tq
    prev = lambda i: jnp.maximum(i - 1, 0)
    jobs = [make(nq) for make in cast_jobs]
    outs = pl.pallas_call(
        functools.partial(_attn_near_kernel, tq=tq, n_heads=n_heads,
                          cast_bodies=tuple(job.body for job in jobs)),
        grid=(nq,),
        in_specs=[
            pl.BlockSpec((tq, width), lambda i: (i, 0)),
            pl.BlockSpec((tq, width), lambda i: (i, 1)),
            pl.BlockSpec((tq, width), lambda i: (prev(i), 1)),
            pl.BlockSpec((tq, width), lambda i: (i, 2)),
            pl.BlockSpec((tq, width), lambda i: (prev(i), 2)),
            pl.BlockSpec((tq, tq), lambda i: (0, 0)),
        ] + [job.in_spec for job in jobs],
        out_specs=[pl.BlockSpec((tq, width), lambda i: (i, 0)),
                   pl.BlockSpec((1, SUBLANES, HEAD_DIM), lambda i: (i, 0, 0))] + [job.out_spec for job in jobs],
        out_shape=[jax.ShapeDtypeStruct((s, width), BF16),
                   jax.ShapeDtypeStruct((nq, SUBLANES, HEAD_DIM), F32)] + [job.out_shape for job in jobs],
        compiler_params=pltpu.CompilerParams(
            dimension_semantics=("arbitrary",), vmem_limit_bytes=VMEM_LIMIT),
        name="stickbreak_attn_near",
    )(qkv, qkv, qkv, qkv, qkv, _cumsum_matrix(tq), *[job.w for job in jobs])
    return outs[0], outs[1], outs[2:]


def _attention(qkv, n_heads, cast_jobs):
    y_near, rmax, cast = _attention_near(qkv, n_heads, cast_jobs)
    need_far = jnp.max(rmax) > LOG2_ZERO_BOUND
    y = lax.cond(need_far, lambda qkv_, y_: _attention_walk(qkv_, n_heads),
                 lambda qkv_, y_: y_, qkv, y_near)
    return y, cast


def _attention_walk(qkv, n_heads):
    s = qkv.shape[0]
    tq = ATTN_Q
    assert s % tq == 0 and s >= 2 * tq and n_heads % ATTN_HEADS == 0
    width = ATTN_HEADS * HEAD_DIM
    groups = n_heads // ATTN_HEADS
    return pl.pallas_call(
        functools.partial(_attn_kernel, tq=tq),
        grid=(groups, s // tq),
        in_specs=[
            pl.BlockSpec((tq, width), lambda g, i: (i, g)),
            pl.BlockSpec((s, width), lambda g, i: (0, groups + g)),
            pl.BlockSpec((s, width), lambda g, i: (0, 2 * groups + g)),
            pl.BlockSpec((tq, tq), lambda g, i: (0, 0)),
        ],
        out_specs=pl.BlockSpec((tq, width), lambda g, i: (i, g)),
        out_shape=jax.ShapeDtypeStruct((s, n_heads * HEAD_DIM), BF16),
        scratch_shapes=[pltpu.VMEM((ATTN_HEADS, tq, HEAD_DIM), F32),
                        pltpu.VMEM((ATTN_HEADS, tq, HEAD_DIM), F32)],
        compiler_params=pltpu.CompilerParams(
            dimension_semantics=("parallel", "arbitrary"), vmem_limit_bytes=VMEM_LIMIT),
        name="stickbreak_attn",
    )(qkv, qkv, qkv, _cumsum_matrix(tq))


def _outproj_kernel(x_ref, ya_ref, ybc_ref, wa_ref, wb_ref, o_ref):
    o_ref[...] = x_ref[...] + _dot(ya_ref[...], wa_ref[...]) + _dot(ybc_ref[...], wb_ref[...])


def _outproj(x, y_a, y_bc, w_out):
    s, d = x.shape
    ka = y_a.shape[1]
    kb = y_bc.shape[1]
    assert ka == kb
    tm = min(PROJ_ROWS, s)
    return pl.pallas_call(
        _outproj_kernel,
        grid=(s // tm,),
        in_specs=[
            pl.BlockSpec((tm, d), lambda i: (i, 0)),
            pl.BlockSpec((tm, ka), lambda i: (i, 0)),
            pl.BlockSpec((tm, kb), lambda i: (i, 0)),
            pl.BlockSpec((ka, d), lambda i: (0, 0)),
            pl.BlockSpec((kb, d), lambda i: (1, 0)),
        ],
        out_specs=pl.BlockSpec((tm, d), lambda i: (i, 0)),
        out_shape=jax.ShapeDtypeStruct((s, d), F32),
        compiler_params=pltpu.CompilerParams(
            dimension_semantics=("parallel",), vmem_limit_bytes=VMEM_LIMIT),
        name="out_proj",
    )(x, y_a, y_bc, w_out, w_out)


def _cast_kernel(x_ref, o_ref):
    o_ref[...] = x_ref[...].astype(BF16)


def _cast_gate_up_kernel(x_ref, o_ref, *, d_ff):
    nj, rows, two_tf = o_ref.shape
    tf = two_tf // 2
    for j in range(nj):
        n = min(tf, d_ff - j * tf)
        o_ref[j, :, :n] = x_ref[:, j * tf:j * tf + n].astype(BF16)
        o_ref[j, :, tf:tf + n] = x_ref[:, d_ff + j * tf:d_ff + j * tf + n].astype(BF16)
        if n < tf:
            zeros = jnp.zeros((rows, tf - n), BF16)
            o_ref[j, :, n:tf] = zeros
            o_ref[j, :, tf + n:] = zeros


def _cast_down_kernel(x_ref, o_ref, *, d_ff):
    o_ref[:d_ff, :] = (x_ref[...] * FFN_RES_WEIGHT).astype(BF16)
    if o_ref.shape[0] > d_ff:
        o_ref[d_ff:, :] = jnp.zeros((o_ref.shape[0] - d_ff, o_ref.shape[1]), BF16)


def _cast_call(body, w, layer, in_block, out_block, out_shape, grid, index, name, out_index=None):
    return pl.pallas_call(
        body,
        grid=(grid,),
        in_specs=[pl.BlockSpec((None,) + in_block, lambda i: (layer,) + index(i))],
        out_specs=pl.BlockSpec(out_block, out_index or index),
        out_shape=jax.ShapeDtypeStruct(out_shape, BF16),
        compiler_params=pltpu.CompilerParams(
            dimension_semantics=("parallel",), vmem_limit_bytes=VMEM_LIMIT),
        name=name,
    )(w)


def _cast_plain(w, layer):
    _, rows, cols = w.shape
    rb = CAST_ROWS
    assert rows % rb == 0
    return _cast_call(_cast_kernel, w, layer, (rb, cols), (rb, cols), (rows, cols), rows // rb,
                      lambda i: (i, 0), "cast_bf16")


def _cast_gate_up(w_gu, layer):
    _, d, two_ff = w_gu.shape
    d_ff = two_ff // 2
    nj = (d_ff + FFN_COLS - 1) // FFN_COLS
    rb = CAST_ROWS // 2
    assert d % rb == 0
    return _cast_call(functools.partial(_cast_gate_up_kernel, d_ff=d_ff), w_gu, layer,
                      (rb, two_ff), (nj, rb, 2 * FFN_COLS), (nj, d, 2 * FFN_COLS), d // rb,
                      lambda i: (i, 0), "cast_gate_up", out_index=lambda i: (0, i, 0))


def _cast_down(w_down, layer):
    _, d_ff, d = w_down.shape
    d_ffp = d_ff + (-d_ff) % FFN_COLS
    assert d % CAST_COLS == 0
    return _cast_call(functools.partial(_cast_down_kernel, d_ff=d_ff), w_down, layer,
                      (d_ff, CAST_COLS), (d_ffp, CAST_COLS), (d_ffp, d), d // CAST_COLS,
                      lambda i: (0, i), "cast_down")


class _CastJob(NamedTuple):
    w: jax.Array
    in_spec: pl.BlockSpec
    out_spec: pl.BlockSpec
    out_shape: jax.ShapeDtypeStruct
    body: Callable


def _gate_up_job(w_gu, layer, n):
    _, d, two_ff = w_gu.shape
    d_ff = two_ff // 2
    nj = (d_ff + FFN_COLS - 1) // FFN_COLS
    rb = d // n
    assert d % n == 0 and rb % BF16_ROWS == 0
    return _CastJob(w_gu, pl.BlockSpec((None, rb, two_ff), lambda i: (layer, i, 0)),
                    pl.BlockSpec((nj, rb, 2 * FFN_COLS), lambda i: (0, i, 0)),
                    jax.ShapeDtypeStruct((nj, d, 2 * FFN_COLS), BF16),
                    lambda i, w_ref, o_ref: _cast_gate_up_kernel(w_ref, o_ref, d_ff=d_ff))


def _plain_job(w, layer, n):
    _, rows, cols = w.shape
    rb = rows // n
    assert rows % n == 0 and rb % BF16_ROWS == 0
    return _CastJob(w, pl.BlockSpec((None, rb, cols), lambda i: (layer, i, 0)),
                    pl.BlockSpec((rb, cols), lambda i: (i, 0)),
                    jax.ShapeDtypeStruct((rows, cols), BF16),
                    lambda i, w_ref, o_ref: _cast_kernel(w_ref, o_ref))


def _cast_down_rows_kernel(i, x_ref, o_ref, *, n_valid, n_blocks):
    if n_blocks == n_valid:
        o_ref[...] = (x_ref[...] * FFN_RES_WEIGHT).astype(BF16)
        return

    @pl.when(i < n_valid)
    def _():
        o_ref[...] = (x_ref[...] * FFN_RES_WEIGHT).astype(BF16)

    @pl.when(i >= n_valid)
    def _():
        o_ref[...] = jnp.zeros_like(o_ref)


def _down_job(w_down, layer, n):
    _, d_ff, d = w_down.shape
    d_ffp = d_ff + (-d_ff) % FFN_COLS
    rb = math.gcd(d_ff, d_ffp)
    n_valid, n_blocks = d_ff // rb, d_ffp // rb
    assert rb % BF16_ROWS == 0 and n_blocks <= n
    return _CastJob(w_down, pl.BlockSpec((None, rb, d), lambda i: (layer, jnp.minimum(i, n_valid - 1), 0)),
                    pl.BlockSpec((rb, d), lambda i: (jnp.minimum(i, n_blocks - 1), 0)),
                    jax.ShapeDtypeStruct((d_ffp, d), BF16),
                    functools.partial(_cast_down_rows_kernel, n_valid=n_valid, n_blocks=n_blocks))


def kernel(x, ffn1_norm, ffn1_w_gu, ffn1_w_down, mix_norm, w_in, sgu_norm, sgu_w, sgu_b, pool_w,
           pool_scale, w_out, ffn2_norm, ffn2_w_gu, ffn2_w_down, final_norm):
    b, s, d = x.shape
    assert b == 1
    depth = w_in.shape[0]
    attn_width = w_out.shape[1] // 2
    n_heads = attn_width // HEAD_DIM
    width = SGU_GROUPS * GROUP_DIM
    h = x.reshape(s, d)
    wgu1 = _cast_gate_up(ffn1_w_gu, 0)
    wd1 = _cast_down(ffn1_w_down, 0)
    w_in_l = _cast_plain(w_in, 0)
    for l in range(depth):
        mix_g = mix_norm[l].reshape(1, d)
        sgu_bias = jnp.repeat(sgu_b[l].T, GROUP_DIM, axis=1)

        h = _ffn(h, jnp.stack([ffn1_norm[l], final_norm]), wgu1, wd1, final=False)
        qkv, y_bc = _inproj(h, mix_g, w_in_l, attn_width, sgu_norm[l].reshape(1, width),
                            sgu_w[l], sgu_bias, pool_w[l].astype(BF16), pool_scale[l].reshape(1, width))
        jobs = [functools.partial(_plain_job, w_out, l),
                functools.partial(_gate_up_job, ffn2_w_gu, l),
                functools.partial(_down_job, ffn2_w_down, l)]
        if l + 1 < depth:
            jobs += [functools.partial(_plain_job, w_in, l + 1),
                     functools.partial(_gate_up_job, ffn1_w_gu, l + 1),
                     functools.partial(_down_job, ffn1_w_down, l + 1)]
        y_a, cast = _attention(qkv, n_heads, jobs)
        w_out_l, wgu2, wd2 = cast[:3]
        h = _outproj(h, y_a, y_bc, w_out_l)
        h = _ffn(h, jnp.stack([ffn2_norm[l], final_norm]), wgu2, wd2, final=(l == depth - 1))
        if l + 1 < depth:
            w_in_l, wgu1, wd1 = cast[3:]
    return h.reshape(b, s, d)
```

```python
import functools
import math
from typing import Callable, NamedTuple

import numpy as np
import jax
import jax.numpy as jnp
from jax import lax
from jax.experimental import pallas as pl
from jax.experimental.pallas import tpu as pltpu

F32 = jnp.float32
BF16 = jnp.bfloat16

HEAD_DIM = 128
LANES = 128
SUBLANES = 8
BF16_ROWS = 16
SGU_GROUPS = 4
SGU_CHUNK = 128
POOL_WINDOWS = (2, 4, 8, 16)
GROUP_DIM = 128
EPS = 1e-6
FFN_RES_WEIGHT = 0.5
assert math.frexp(FFN_RES_WEIGHT)[0] == 0.5

V7X_VMEM_BYTES = 64 * 1024 * 1024
VMEM_LIMIT = V7X_VMEM_BYTES - 6 * 1024 * 1024

FFN_ROWS = 1024
FFN_COLS = 512
PROJ_ROWS = 512
QKV_COLS = 1024
CAST_ROWS = 512
CAST_COLS = 512
ATTN_Q = 256
ATTN_HEADS = 2
LOG2_E = 1.4426950408889634
EXP2_MAX = 126.0
LOG2_ZERO_BOUND = -151.0


def _rmsnorm(x, g):
    inv = lax.rsqrt(jnp.mean(x * x, axis=-1, keepdims=True) + EPS)
    return x * inv * g


def _dot(a, b):
    return jnp.dot(a, b, preferred_element_type=F32)


def _ffn_kernel(x_ref, gains_ref, wgu_ref, wd_ref, o_ref, xn_ref, *, nj, final):
    j = pl.program_id(1)
    tf = wd_ref.shape[0]

    @pl.when(j == 0)
    def _():
        xn_ref[...] = _rmsnorm(x_ref[...], gains_ref[0:1, :]).astype(BF16)

    def down_proj():
        gu = _dot(xn_ref[...], wgu_ref[...])
        gate = gu[:, :tf]
        h = (gate * jax.nn.sigmoid(gate) * gu[:, tf:]).astype(BF16)
        return _dot(h, wd_ref[...])

    @pl.when(j == 0)
    def _():
        o_ref[...] = x_ref[...] + down_proj()

    @pl.when(j > 0)
    def _():
        o_ref[...] += down_proj()

    if final:
        @pl.when(j == nj - 1)
        def _():
            o_ref[...] = _rmsnorm(o_ref[...], gains_ref[1:2, :])


def _ffn(x, gains, w_gu_p, w_down_p, *, final):
    s, d = x.shape
    nj = w_gu_p.shape[0]
    assert w_gu_p.shape == (nj, d, 2 * FFN_COLS) and w_down_p.shape == (nj * FFN_COLS, d)
    tm = min(FFN_ROWS, s)
    ni = s // tm
    x_index = lambda i, j: (jnp.minimum(i + jnp.minimum(j, 1), ni - 1), 0)
    return pl.pallas_call(
        functools.partial(_ffn_kernel, nj=nj, final=final),
        grid=(ni, nj),
        in_specs=[
            pl.BlockSpec((tm, d), x_index),
            pl.BlockSpec((2, d), lambda i, j: (0, 0)),
            pl.BlockSpec((None, d, 2 * FFN_COLS), lambda i, j: (j, 0, 0)),
            pl.BlockSpec((FFN_COLS, d), lambda i, j: (j, 0)),
        ],
        out_specs=pl.BlockSpec((tm, d), lambda i, j: (i, 0)),
        out_shape=jax.ShapeDtypeStruct((s, d), F32),
        scratch_shapes=[pltpu.VMEM((tm, d), BF16)],
        compiler_params=pltpu.CompilerParams(
            dimension_semantics=("arbitrary", "arbitrary"), vmem_limit_bytes=VMEM_LIMIT),
        name="ffn",
    )(x, gains, w_gu_p, w_down_p)


def _inproj_kernel(x_ref, g_ref, w_ref, sn_ref, sw_ref, sb_ref, band_ref, pw_ref, ps_ref,
                   qkv_ref, o_ref, ph_ref, plo_ref, *, tm, attn_width, scale):
    i = pl.program_id(0)
    ch = SGU_CHUNK
    width = SGU_GROUPS * GROUP_DIM
    xn = _rmsnorm(x_ref[...], g_ref[...]).astype(BF16)
    proj = _dot(xn, w_ref[:, 3 * attn_width:])
    u = jax.nn.gelu(proj[:, :width])
    v = jax.nn.gelu(proj[:, width:2 * width])
    p = proj[:, 2 * width:]

    @pl.when(i == 0)
    def _():
        ph_ref[0:ch, :] = jnp.zeros((ch, width), BF16)
        plo_ref[0:ch, :] = jnp.zeros((ch, width), BF16)

    p_hi = p.astype(BF16)
    ph_ref[ch:ch + tm, :] = p_hi
    plo_ref[ch:ch + tm, :] = (p - p_hi.astype(F32)).astype(BF16)

    row = lax.broadcasted_iota(jnp.int32, (ch, ch), 0)
    col = lax.broadcasted_iota(jnp.int32, (ch, ch), 1)
    tril = col <= row
    units = [(grp, c) for grp in range(SGU_GROUPS) for c in range(tm // ch)]
    lanes = lambda grp: slice(grp * GROUP_DIM, (grp + 1) * GROUP_DIM)
    rows = lambda c: slice(c * ch, (c + 1) * ch)
    vg = [_rmsnorm(v[:, lanes(grp)], sn_ref[:, lanes(grp)]).astype(BF16) for grp in range(SGU_GROUPS)]
    w_causal = [jnp.where(tril, sw_ref[grp], 0.0).astype(BF16) for grp in range(SGU_GROUPS)]
    mixed = {}
    for grp in range(SGU_GROUPS):
        for c in range(0, tm // ch, 2):
            pair = _dot(w_causal[grp], jnp.concatenate([vg[grp][rows(c)], vg[grp][rows(c + 1)]], axis=1))
            mixed[grp, c], mixed[grp, c + 1] = pair[:, :GROUP_DIM], pair[:, GROUP_DIM:]
    wsum = {}
    for grp, c in units:
        ext = slice(c * ch, (c + 2) * ch)
        pair = _dot(band_ref[grp], jnp.concatenate([ph_ref[ext, lanes(grp)], plo_ref[ext, lanes(grp)]], axis=1))
        wsum[grp, c] = pair[:, :GROUP_DIM] + pair[:, GROUP_DIM:]
    for grp, c in units:
        o_ref[rows(c), lanes(grp)] = (u[rows(c), lanes(grp)] * (mixed[grp, c] + sb_ref[:, lanes(grp)])).astype(BF16)
    dlt = {}
    for grp, c in units:
        pos = i * tm + c * ch + row
        count = jnp.minimum(pos + 1, POOL_WINDOWS[grp]).astype(F32)
        dlt[grp, c] = (wsum[grp, c] / count - p[rows(c), lanes(grp)]).astype(BF16)
    for grp, c in units:
        y = _dot(dlt[grp, c], pw_ref[grp]) * ps_ref[:, lanes(grp)]
        o_ref[rows(c), width + grp * GROUP_DIM:width + (grp + 1) * GROUP_DIM] = y.astype(BF16)

    ph_ref[0:ch, :] = ph_ref[tm:tm + ch, :]
    plo_ref[0:ch, :] = plo_ref[tm:tm + ch, :]

    for c in range(3 * attn_width // QKV_COLS):
        cols = slice(c * QKV_COLS, (c + 1) * QKV_COLS)
        acc = _dot(xn, w_ref[:, cols])
        if (c + 1) * QKV_COLS <= attn_width:
            acc = acc * scale
        qkv_ref[:, cols] = acc.astype(BF16)


def _pool_bands():
    t = np.arange(SGU_CHUNK)[:, None] + SGU_CHUNK
    j = np.arange(2 * SGU_CHUNK)[None, :]
    bands = [((t - j >= 0) & (t - j < w)).astype(np.float32) for w in POOL_WINDOWS]
    return jnp.asarray(np.stack(bands), dtype=BF16)


def _inproj(x, g, w_in, attn_width, sgu_norm, sgu_w, sgu_bias, pool_w, pool_scale):
    s, d = x.shape
    width = SGU_GROUPS * GROUP_DIM
    n_in = w_in.shape[1]
    assert n_in == 3 * attn_width + 3 * width and attn_width % QKV_COLS == 0
    tm = min(PROJ_ROWS, s)
    full = lambda shape: pl.BlockSpec(shape, lambda i: (0,) * len(shape))
    return pl.pallas_call(
        functools.partial(_inproj_kernel, tm=tm, attn_width=attn_width,
                          scale=LOG2_E / math.sqrt(HEAD_DIM)),
        grid=(s // tm,),
        in_specs=[
            pl.BlockSpec((tm, d), lambda i: (i, 0)),
            full((1, d)),
            pl.BlockSpec((d, n_in), lambda i: (0, 0), pipeline_mode=pl.Buffered(1)),
            full((1, width)),
            full((SGU_GROUPS, SGU_CHUNK, SGU_CHUNK)),
            full((SGU_CHUNK, width)),
            full((SGU_GROUPS, SGU_CHUNK, 2 * SGU_CHUNK)),
            full((SGU_GROUPS, GROUP_DIM, GROUP_DIM)),
            full((1, width)),
        ],
        out_specs=[pl.BlockSpec((tm, 3 * attn_width), lambda i: (i, 0)),
                   pl.BlockSpec((tm, 2 * width), lambda i: (i, 0))],
        out_shape=[jax.ShapeDtypeStruct((s, 3 * attn_width), BF16),
                   jax.ShapeDtypeStruct((s, 2 * width), BF16)],
        scratch_shapes=[pltpu.VMEM((tm + SGU_CHUNK, width), BF16),
                        pltpu.VMEM((tm + SGU_CHUNK, width), BF16)],
        compiler_params=pltpu.CompilerParams(
            dimension_semantics=("arbitrary",), vmem_limit_bytes=VMEM_LIMIT),
        name="in_proj",
    )(x, g, w_in, sgu_norm, sgu_w, sgu_bias, _pool_bands(), pool_w, pool_scale)


def _attn_block(q, ks, vs, m_neg, r, causal):
    log_own, sp16, total = _attn_softplus(_attn_scores(q, ks), causal)
    a = _attn_weights(log_own, _dot(sp16, m_neg), r, causal)
    return _dot(a, vs), (total if r is None else r + total)


def _attn_scores(q, ks):
    return lax.dot_general(q, ks, (((1,), (1,)), ((), ())), preferred_element_type=F32)


def _attn_softplus(z2, causal):
    sp = jnp.maximum(jnp.log(1.0 + jnp.exp2(jnp.minimum(z2, EXP2_MAX))) * LOG2_E, z2)
    if causal is not None:
        sp = jnp.where(causal, sp, 0.0)
    total = jnp.broadcast_to(-jnp.sum(sp, axis=1, keepdims=True), (z2.shape[0], HEAD_DIM))
    return z2 - sp, sp.astype(BF16), total


def _attn_weights(log_own, later, r, causal):
    log_a = log_own + later
    if r is not None:
        log_a = log_a + jnp.concatenate([r] * (log_own.shape[1] // HEAD_DIM), axis=1)
    a = jnp.exp2(log_a)
    if causal is not None:
        a = jnp.where(causal, a, 0.0)
    return a.astype(BF16)


def _attn_kernel(q_ref, k_ref, v_ref, m_ref, o_ref, r_ref, acc_ref, *, tq):
    i = pl.program_id(1)
    m_neg = m_ref[...]
    heads = r_ref.shape[0]
    row = lax.broadcasted_iota(jnp.int32, (tq, tq), 0)
    col = lax.broadcasted_iota(jnp.int32, (tq, tq), 1)
    causal = col < row
    diag = pl.multiple_of(i * tq, tq)
    prev = pl.multiple_of(jnp.maximum(i - 1, 0) * tq, tq)
    no_prev = jnp.where(i == 0, jnp.float32(-1e30), jnp.float32(0.0))

    rmax0 = None
    for h in range(heads):
        hs = slice(h * HEAD_DIM, (h + 1) * HEAD_DIM)
        q = q_ref[:, hs]
        pv0, r = _attn_block(q, k_ref[pl.ds(diag, tq), hs], v_ref[pl.ds(diag, tq), hs], m_neg, None, causal)
        pv1, r = _attn_block(q, k_ref[pl.ds(prev, tq), hs], v_ref[pl.ds(prev, tq), hs], m_neg,
                             r + no_prev, None)
        acc_ref[h] = pv0 + pv1
        r_ref[h] = r
        hmax = jnp.max(r)
        rmax0 = hmax if rmax0 is None else jnp.maximum(rmax0, hmax)

    def cond(carry):
        j, rmax = carry
        return jnp.logical_and(j >= 0, rmax > LOG2_ZERO_BOUND)

    def body(carry):
        j, _ = carry
        start = pl.multiple_of(j * tq, tq)
        rmax = None
        for h in range(heads):
            hs = slice(h * HEAD_DIM, (h + 1) * HEAD_DIM)
            pv, r = _attn_block(q_ref[:, hs], k_ref[pl.ds(start, tq), hs], v_ref[pl.ds(start, tq), hs],
                                m_neg, r_ref[h], None)
            acc_ref[h] += pv
            r_ref[h] = r
            hmax = jnp.max(r)
            rmax = hmax if rmax is None else jnp.maximum(rmax, hmax)
        return j - 1, rmax

    lax.while_loop(cond, body, (i - 2, rmax0))
    for h in range(heads):
        o_ref[:, h * HEAD_DIM:(h + 1) * HEAD_DIM] = acc_ref[h].astype(BF16)


def _cumsum_matrix(ch):
    j = np.arange(ch)[:, None]
    s = np.arange(ch)[None, :]
    return jnp.asarray(-(j > s).astype(np.float32), dtype=BF16)


def _attn_near_kernel(q_ref, kd_ref, kp_ref, vd_ref, vp_ref, m_ref, *rest, tq, n_heads, cast_bodies):
    n_cast = len(cast_bodies)
    cast_in, (o_ref, rmax_ref), cast_out = rest[:n_cast], rest[n_cast:n_cast + 2], rest[n_cast + 2:]
    i = pl.program_id(0)
    for body, w_ref, wo_ref in zip(cast_bodies, cast_in, cast_out, strict=True):
        body(i, w_ref, wo_ref)
    m_neg = m_ref[...]
    row = lax.broadcasted_iota(jnp.int32, (tq, tq), 0)
    col = lax.broadcasted_iota(jnp.int32, (tq, tq), 1)
    causal = col < row
    no_prev = jnp.where(i == 0, jnp.float32(-1e30), jnp.float32(0.0))
    heads = [slice(h * HEAD_DIM, (h + 1) * HEAD_DIM) for h in range(n_heads)]
    z_d, z_p, sp_d, sp_p, from_d, from_p, a_d, a_p = ({} for _ in range(8))
    rmax = []

    def stage(s, h):
        hs = heads[h]
        if s == 0:
            z_d[h] = _attn_scores(q_ref[:, hs], kd_ref[:, hs])
            z_p[h] = _attn_scores(q_ref[:, hs], kp_ref[:, hs])
        elif s == 1:
            sp_d[h] = _attn_softplus(z_d[h], causal)
            sp_p[h] = _attn_softplus(z_p[h], None)
        elif s == 2:
            from_d[h] = _dot(sp_d[h][1], m_neg)
            from_p[h] = _dot(sp_p[h][1], m_neg)
        elif s == 3:
            a_d[h] = _attn_weights(sp_d[h][0], from_d[h], None, causal)
            a_p[h] = _attn_weights(sp_p[h][0], from_p[h], sp_d[h][2] + no_prev, None)
        else:
            o_ref[:, hs] = (_dot(a_d[h], vd_ref[:, hs]) + _dot(a_p[h], vp_ref[:, hs])).astype(BF16)
            rmax.append(sp_d[h][2] + no_prev + sp_p[h][2])

    for t in range(n_heads + 4):
        for h in range(n_heads):
            if 0 <= t - h < 5:
                stage(t - h, h)
    rm = functools.reduce(jnp.maximum, rmax)
    rm = jnp.max(rm.reshape(tq // SUBLANES, SUBLANES, HEAD_DIM), axis=0)
    rmax_ref[0] = jnp.where(i <= 1, jnp.float32(-1e30), rm)


def _attention_near(qkv, n_heads, cast_jobs):
    s = qkv.shape[0]
    tq = ATTN_Q
    width = n_heads * HEAD_DIM
    assert s % tq == 0 and qkv.shape[1] == 3 * width
    nq = s // tq
    prev = lambda i: jnp.maximum(i - 1, 0)
    jobs = [make(nq) for make in cast_jobs]
    outs = pl.pallas_call(
        functools.partial(_attn_near_kernel, tq=tq, n_heads=n_heads,
                          cast_bodies=tuple(job.body for job in jobs)),
        grid=(nq,),
        in_specs=[
            pl.BlockSpec((tq, width), lambda i: (i, 0)),
            pl.BlockSpec((tq, width), lambda i: (i, 1)),
            pl.BlockSpec((tq, width), lambda i: (prev(i), 1)),
            pl.BlockSpec((tq, width), lambda i: (i, 2)),
            pl.BlockSpec((tq, width), lambda i: (prev(i), 2)),
            pl.BlockSpec((tq, tq), lambda i: (0, 0)),
        ] + [job.in_spec for job in jobs],
        out_specs=[pl.BlockSpec((tq, width), lambda i: (i, 0)),
                   pl.BlockSpec((1, SUBLANES, HEAD_DIM), lambda i: (i, 0, 0))] + [job.out_spec for job in jobs],
        out_shape=[jax.ShapeDtypeStruct((s, width), BF16),
                   jax.ShapeDtypeStruct((nq, SUBLANES, HEAD_DIM), F32)] + [job.out_shape for job in jobs],
        compiler_params=pltpu.CompilerParams(
            dimension_semantics=("arbitrary",), vmem_limit_bytes=VMEM_LIMIT),
        name="stickbreak_attn_near",
    )(qkv, qkv, qkv, qkv, qkv, _cumsum_matrix(tq), *[job.w for job in jobs])
    return outs[0], outs[1], outs[2:]


def _attention(qkv, n_heads, cast_jobs):
    y_near, rmax, cast = _attention_near(qkv, n_heads, cast_jobs)
    need_far = jnp.max(rmax) > LOG2_ZERO_BOUND
    y = lax.cond(need_far, lambda qkv_, y_: _attention_walk(qkv_, n_heads),
                 lambda qkv_, y_: y_, qkv, y_near)
    return y, cast


def _attention_walk(qkv, n_heads):
    s = qkv.shape[0]
    tq = ATTN_Q
    assert s % tq == 0 and s >= 2 * tq and n_heads % ATTN_HEADS == 0
    width = ATTN_HEADS * HEAD_DIM
    groups = n_heads // ATTN_HEADS
    return pl.pallas_call(
        functools.partial(_attn_kernel, tq=tq),
        grid=(groups, s // tq),
        in_specs=[
            pl.BlockSpec((tq, width), lambda g, i: (i, g)),
            pl.BlockSpec((s, width), lambda g, i: (0, groups + g)),
            pl.BlockSpec((s, width), lambda g, i: (0, 2 * groups + g)),
            pl.BlockSpec((tq, tq), lambda g, i: (0, 0)),
        ],
        out_specs=pl.BlockSpec((tq, width), lambda g, i: (i, g)),
        out_shape=jax.ShapeDtypeStruct((s, n_heads * HEAD_DIM), BF16),
        scratch_shapes=[pltpu.VMEM((ATTN_HEADS, tq, HEAD_DIM), F32),
                        pltpu.VMEM((ATTN_HEADS, tq, HEAD_DIM), F32)],
        compiler_params=pltpu.CompilerParams(
            dimension_semantics=("parallel", "arbitrary"), vmem_limit_bytes=VMEM_LIMIT),
        name="stickbreak_attn",
    )(qkv, qkv, qkv, _cumsum_matrix(tq))


def _outproj_kernel(x_ref, ya_ref, ybc_ref, wa_ref, wb_ref, o_ref):
    o_ref[...] = x_ref[...] + _dot(ya_ref[...], wa_ref[...]) + _dot(ybc_ref[...], wb_ref[...])


def _outproj(x, y_a, y_bc, w_out):
    s, d = x.shape
    ka = y_a.shape[1]
    kb = y_bc.shape[1]
    assert ka == kb
    tm = min(PROJ_ROWS, s)
    return pl.pallas_call(
        _outproj_kernel,
        grid=(s // tm,),
        in_specs=[
            pl.BlockSpec((tm, d), lambda i: (i, 0)),
            pl.BlockSpec((tm, ka), lambda i: (i, 0)),
            pl.BlockSpec((tm, kb), lambda i: (i, 0)),
            pl.BlockSpec((ka, d), lambda i: (0, 0)),
            pl.BlockSpec((kb, d), lambda i: (1, 0)),
        ],
        out_specs=pl.BlockSpec((tm, d), lambda i: (i, 0)),
        out_shape=jax.ShapeDtypeStruct((s, d), F32),
        compiler_params=pltpu.CompilerParams(
            dimension_semantics=("parallel",), vmem_limit_bytes=VMEM_LIMIT),
        name="out_proj",
    )(x, y_a, y_bc, w_out, w_out)


def _cast_kernel(x_ref, o_ref):
    o_ref[...] = x_ref[...].astype(BF16)


def _cast_gate_up_kernel(x_ref, o_ref, *, d_ff):
    nj, rows, two_tf = o_ref.shape
    tf = two_tf // 2
    for j in range(nj):
        n = min(tf, d_ff - j * tf)
        o_ref[j, :, :n] = x_ref[:, j * tf:j * tf + n].astype(BF16)
        o_ref[j, :, tf:tf + n] = x_ref[:, d_ff + j * tf:d_ff + j * tf + n].astype(BF16)
        if n < tf:
            zeros = jnp.zeros((rows, tf - n), BF16)
            o_ref[j, :, n:tf] = zeros
            o_ref[j, :, tf + n:] = zeros


def _cast_down_kernel(x_ref, o_ref, *, d_ff):
    o_ref[:d_ff, :] = (x_ref[...] * FFN_RES_WEIGHT).astype(BF16)
    if o_ref.shape[0] > d_ff:
        o_ref[d_ff:, :] = jnp.zeros((o_ref.shape[0] - d_ff, o_ref.shape[1]), BF16)


def _cast_call(body, w, layer, in_block, out_block, out_shape, grid, index, name, out_index=None):
    return pl.pallas_call(
        body,
        grid=(grid,),
        in_specs=[pl.BlockSpec((None,) + in_block, lambda i: (layer,) + index(i))],
        out_specs=pl.BlockSpec(out_block, out_index or index),
        out_shape=jax.ShapeDtypeStruct(out_shape, BF16),
        compiler_params=pltpu.CompilerParams(
            dimension_semantics=("parallel",), vmem_limit_bytes=VMEM_LIMIT),
        name=name,
    )(w)


def _cast_plain(w, layer):
    _, rows, cols = w.shape
    rb = CAST_ROWS
    assert rows % rb == 0
    return _cast_call(_cast_kernel, w, layer, (rb, cols), (rb, cols), (rows, cols), rows // rb,
                      lambda i: (i, 0), "cast_bf16")


def _cast_gate_up(w_gu, layer):
    _, d, two_ff = w_gu.shape
    d_ff = two_ff // 2
    nj = (d_ff + FFN_COLS - 1) // FFN_COLS
    rb = CAST_ROWS // 2
    assert d % rb == 0
    return _cast_call(functools.partial(_cast_gate_up_kernel, d_ff=d_ff), w_gu, layer,
                      (rb, two_ff), (nj, rb, 2 * FFN_COLS), (nj, d, 2 * FFN_COLS), d // rb,
                      lambda i: (i, 0), "cast_gate_up", out_index=lambda i: (0, i, 0))


def _cast_down(w_down, layer):
    _, d_ff, d = w_down.shape
    d_ffp = d_ff + (-d_ff) % FFN_COLS
    assert d % CAST_COLS == 0
    return _cast_call(functools.partial(_cast_down_kernel, d_ff=d_ff), w_down, layer,
                      (d_ff, CAST_COLS), (d_ffp, CAST_COLS), (d_ffp, d), d // CAST_COLS,
                      lambda i: (0, i), "cast_down")


class _CastJob(NamedTuple):
    w: jax.Array
    in_spec: pl.BlockSpec
    out_spec: pl.BlockSpec
    out_shape: jax.ShapeDtypeStruct
    body: Callable


def _gate_up_job(w_gu, layer, n):
    _, d, two_ff = w_gu.shape
    d_ff = two_ff // 2
    nj = (d_ff + FFN_COLS - 1) // FFN_COLS
    rb = d // n
    assert d % n == 0 and rb % BF16_ROWS == 0
    return _CastJob(w_gu, pl.BlockSpec((None, rb, two_ff), lambda i: (layer, i, 0)),
                    pl.BlockSpec((nj, rb, 2 * FFN_COLS), lambda i: (0, i, 0)),
                    jax.ShapeDtypeStruct((nj, d, 2 * FFN_COLS), BF16),
                    lambda i, w_ref, o_ref: _cast_gate_up_kernel(w_ref, o_ref, d_ff=d_ff))


def _plain_job(w, layer, n):
    _, rows, cols = w.shape
    rb = rows // n
    assert rows % n == 0 and rb % BF16_ROWS == 0
    return _CastJob(w, pl.BlockSpec((None, rb, cols), lambda i: (layer, i, 0)),
                    pl.BlockSpec((rb, cols), lambda i: (i, 0)),
                    jax.ShapeDtypeStruct((rows, cols), BF16),
                    lambda i, w_ref, o_ref: _cast_kernel(w_ref, o_ref))


def _cast_down_rows_kernel(i, x_ref, o_ref, *, n_valid, n_blocks):
    if n_blocks == n_valid:
        o_ref[...] = (x_ref[...] * FFN_RES_WEIGHT).astype(BF16)
        return

    @pl.when(i < n_valid)
    def _():
        o_ref[...] = (x_ref[...] * FFN_RES_WEIGHT).astype(BF16)

    @pl.when(i >= n_valid)
    def _():
        o_ref[...] = jnp.zeros_like(o_ref)


def _down_job(w_down, layer, n):
    _, d_ff, d = w_down.shape
    d_ffp = d_ff + (-d_ff) % FFN_COLS
    rb = math.gcd(d_ff, d_ffp)
    n_valid, n_blocks = d_ff // rb, d_ffp // rb
    assert rb % BF16_ROWS == 0 and n_blocks <= n
    return _CastJob(w_down, pl.BlockSpec((None, rb, d), lambda i: (layer, jnp.minimum(i, n_valid - 1), 0)),
                    pl.BlockSpec((rb, d), lambda i: (jnp.minimum(i, n_blocks - 1), 0)),
                    jax.ShapeDtypeStruct((d_ffp, d), BF16),
                    functools.partial(_cast_down_rows_kernel, n_valid=n_valid, n_blocks=n_blocks))


def kernel(x, ffn1_norm, ffn1_w_gu, ffn1_w_down, mix_norm, w_in, sgu_norm, sgu_w, sgu_b, pool_w,
           pool_scale, w_out, ffn2_norm, ffn2_w_gu, ffn2_w_down, final_norm):
    b, s, d = x.shape
    assert b == 1
    depth = w_in.shape[0]
    attn_width = w_out.shape[1] // 2
    n_heads = attn_width // HEAD_DIM
    width = SGU_GROUPS * GROUP_DIM
    h = x.reshape(s, d)
    wgu1 = _cast_gate_up(ffn1_w_gu, 0)
    wd1 = _cast_down(ffn1_w_down, 0)
    w_in_l = _cast_plain(w_in, 0)
    for l in range(depth):
        mix_g = mix_norm[l].reshape(1, d)
        sgu_bias = jnp.repeat(sgu_b[l].T, GROUP_DIM, axis=1)

        h = _ffn(h, jnp.stack([ffn1_norm[l], final_norm]), wgu1, wd1, final=False)
        qkv, y_bc = _inproj(h, mix_g, w_in_l, attn_width, sgu_norm[l].reshape(1, width),
                            sgu_w[l], sgu_bias, pool_w[l].astype(BF16), pool_scale[l].reshape(1, width))
        jobs = [functools.partial(_plain_job, w_out, l),
                functools.partial(_gate_up_job, ffn2_w_gu, l),
                functools.partial(_down_job, ffn2_w_down, l)]
        if l + 1 < depth:
            jobs += [functools.partial(_plain_job, w_in, l + 1),
                     functools.partial(_gate_up_job, ffn1_w_gu, l + 1),
                     functools.partial(_down_job, ffn1_w_down, l + 1)]
        y_a, cast = _attention(qkv, n_heads, jobs)
        w_out_l, wgu2, wd2 = cast[:3]
        h = _outproj(h, y_a, y_bc, w_out_l)
        h = _ffn(h, jnp.stack([ffn2_norm[l], final_norm]), wgu2, wd2, final=(l == depth - 1))
        if l + 1 < depth:
            w_in_l, wgu1, wd1 = cast[3:]
    return h.reshape(b, s, d)
```
